```python
import math
import jax, jax.numpy as jnp
from jax import lax
import numpy as np

D_MODEL = 1024
BATCH = 8
SEQ = 4096
DEPTH = 1
DEC_BATCH = 16
DEC_SEQ = 64
PAST_LEN = 2048

CHUNK = 64
Q_BLOCK = 128
HA = 8
DA = 64
HB = 16
DB = 64
LEFT_CHUNKS = 8
BAND_ROWS = LEFT_CHUNKS * CHUNK
MAX_REL = 128
T5_BUCKETS = 32
T5_MAX_DIST = 128
D_FF = 2816
CONV_W = 3
ALPHA = (2.0 * DEPTH) ** 0.25
BETA = (8.0 * DEPTH) ** -0.25
LN_EPS = 1e-5
RMS_EPS = 1e-5
NEG = -1e30

WA = HA * 2 * DA
WB = HB * DB
N_IN = 3 * WA + 3 * WB + 2 * D_MODEL

kernel_name = "hybrid_diffattn_chunkband_convffn_step"


def layer_norm(x, g, b):
    xf = x.astype(jnp.float32)
    mu = jnp.mean(xf, -1, keepdims=True)
    var = jnp.mean(jnp.square(xf - mu), -1, keepdims=True)
    y = (xf - mu) * lax.rsqrt(var + LN_EPS) * g.astype(jnp.float32) + b.astype(jnp.float32)
    return y.astype(x.dtype)


def head_rmsnorm(o, g):
    of = o.astype(jnp.float32)
    of = of * lax.rsqrt(jnp.mean(of * of, -1, keepdims=True) + RMS_EPS)
    return (of * g.astype(jnp.float32)).astype(o.dtype)


def t5_bucket(rel):
    nb = T5_BUCKETS // 2
    max_exact = nb // 2
    ret = jnp.where(rel > 0, nb, 0)
    n = jnp.abs(rel)
    nf = jnp.maximum(n, 1).astype(jnp.float32)
    large = max_exact + (jnp.log(nf / max_exact) / math.log(T5_MAX_DIST / max_exact)
                         * (nb - max_exact)).astype(jnp.int32)
    large = jnp.minimum(large, nb - 1)
    return ret + jnp.where(n < max_exact, n, large)


def split_proj(x, w_in):
    B, T = x.shape[:2]
    p = jnp.einsum('btd,dn->btn', x, w_in)
    cuts = [WA, 2 * WA, 3 * WA, 3 * WA + WB, 3 * WA + 2 * WB, 3 * WA + 3 * WB, 3 * WA + 3 * WB + D_MODEL]
    qa, ka, va, qb, kb, vb, ga, gb = jnp.split(p, cuts, axis=-1)
    qa = qa.reshape(B, T, HA, 2, DA)
    ka = ka.reshape(B, T, HA, 2, DA)
    va = va.reshape(B, T, HA, 2 * DA)
    qb = qb.reshape(B, T, HB, DB)
    kb = kb.reshape(B, T, HB, DB)
    vb = vb.reshape(B, T, HB, DB)
    return qa, ka, va, qb, kb, vb, ga, gb


def diff_attention(q, k, v, qpos, kpos, lam, t5_table):
    s = jnp.einsum('bqhmd,bkhmd->bhmqk', q, k).astype(jnp.float32) * (DA ** -0.5)
    rel = kpos[None, :] - qpos[:, None]
    bias = jnp.transpose(t5_table[t5_bucket(rel)].astype(jnp.float32), (2, 0, 1))
    mask = (kpos[None, :] // CHUNK) <= (qpos[:, None] // CHUNK)
    s = jnp.where(mask, s + bias[None, :, None], NEG)
    p = jax.nn.softmax(s, axis=-1)
    w = p[:, :, 0] - lam * p[:, :, 1]
    return jnp.einsum('bhqk,bkhe->bqhe', w.astype(v.dtype), v)


def band_attention(q, k, v, qpos, kpos, rel_table):
    s = jnp.einsum('bqhd,bkhd->bhqk', q, k).astype(jnp.float32) * (DB ** -0.5)
    rel = jnp.clip(kpos[None, :] - qpos[:, None], -MAX_REL, MAX_REL) + MAX_REL
    bias = jnp.transpose(rel_table[rel].astype(jnp.float32), (2, 0, 1))
    qc = qpos[:, None] // CHUNK
    kc = kpos[None, :] // CHUNK
    mask = (kpos[None, :] >= 0) & (kc <= qc) & (kc >= qc - LEFT_CHUNKS)
    s = jnp.where(mask, s + bias[None], NEG)
    p = jax.nn.softmax(s, axis=-1)
    return jnp.einsum('bhqk,bkhd->bqhd', p.astype(v.dtype), v)


def prompt_diff(qa, ka, va, lam, t5_table):
    B, S = qa.shape[:2]
    nblk = S // Q_BLOCK
    kpos = jnp.arange(S)
    qblocks = jnp.moveaxis(qa.reshape(B, nblk, Q_BLOCK, HA, 2, DA), 1, 0)

    def one_block(args):
        qblk, i = args
        qpos = i * Q_BLOCK + jnp.arange(Q_BLOCK)
        return diff_attention(qblk, ka, va, qpos, kpos, lam, t5_table)

    o = lax.map(one_block, (qblocks, jnp.arange(nblk)))
    return jnp.moveaxis(o, 0, 1).reshape(B, S, HA, 2 * DA)


def prompt_band(qb, kb, vb, rel_table):
    B, S = qb.shape[:2]
    nc = S // CHUNK
    pad = ((0, 0), (BAND_ROWS, 0), (0, 0), (0, 0))
    kp = jnp.pad(kb, pad)
    vp = jnp.pad(vb, pad)
    qchunks = jnp.moveaxis(qb.reshape(B, nc, CHUNK, HB, DB), 1, 0)

    def one_chunk(args):
        qch, c = args
        start = c * CHUNK
        kband = lax.dynamic_slice_in_dim(kp, start, BAND_ROWS + CHUNK, axis=1)
        vband = lax.dynamic_slice_in_dim(vp, start, BAND_ROWS + CHUNK, axis=1)
        qpos = start + jnp.arange(CHUNK)
        kpos = start - BAND_ROWS + jnp.arange(BAND_ROWS + CHUNK)
        return band_attention(qch, kband, vband, qpos, kpos, rel_table)

    o = lax.map(one_chunk, (qchunks, jnp.arange(nc)))
    return jnp.moveaxis(o, 0, 1).reshape(B, S, HB, DB)


def conv_ffn(h, conv_prev, w_up, w_gate, conv_w, conv_b, w_down):
    T = h.shape[1]
    u = jnp.einsum('btd,df->btf', h, w_up)
    g = jnp.einsum('btd,df->btf', h, w_gate)
    ext = jnp.concatenate([conv_prev.astype(u.dtype), u], axis=1)
    uc = conv_b + ext[:, 0:T] * conv_w[0]
    for j in range(1, CONV_W):
        uc = uc + ext[:, j:j + T] * conv_w[j]
    out = jnp.einsum('btf,fd->btd', jax.nn.gelu(uc) * g, w_down)
    return out, ext[:, -(CONV_W - 1):]


def block_tail(x, oa, ob, ga, gb, conv_prev, lam_init, subln_g, w_pa, w_pb, w_out,
               ln1_g, ln1_b, w_up, w_gate, conv_w, conv_b, w_down, ln2_g, ln2_b):
    B, T = x.shape[:2]
    ya = (head_rmsnorm(oa, subln_g) * (1.0 - lam_init)).reshape(B, T, WA) @ w_pa
    yb = ob.reshape(B, T, WB) @ w_pb
    mixed = (jax.nn.sigmoid(ga) * ya + jax.nn.sigmoid(gb) * yb) @ w_out
    h = layer_norm(ALPHA * x + mixed, ln1_g, ln1_b)
    f, conv_state = conv_ffn(h, conv_prev, w_up, w_gate, conv_w, conv_b, w_down)
    return layer_norm(ALPHA * h + f, ln2_g, ln2_b), conv_state


def setup_inputs(seed: int = 0) -> dict:
    key = jax.random.key(seed)
    ks = jax.random.split(key, 32)
    f32 = jnp.float32
    PB = min(BAND_ROWS, PAST_LEN)
    nrm = lambda k, s: jax.random.normal(k, s, f32)
    col_scale = jnp.concatenate([
        jnp.ones((2 * WA,), f32), jnp.full((WA,), BETA, f32),
        jnp.ones((2 * WB,), f32), jnp.full((WB,), BETA, f32),
        jnp.ones((2 * D_MODEL,), f32)])
    return {
        "x_prompt": nrm(ks[0], (BATCH, SEQ, D_MODEL)),
        "x_sample": nrm(ks[1], (DEC_BATCH, DEC_SEQ, D_MODEL)),
        "cache_a_k": nrm(ks[2], (DEPTH, DEC_BATCH, PAST_LEN, HA, 2, DA)),
        "cache_a_v": nrm(ks[3], (DEPTH, DEC_BATCH, PAST_LEN, HA, 2 * DA)) * BETA,
        "cache_b_k": nrm(ks[4], (DEPTH, DEC_BATCH, PB, HB, DB)),
        "cache_b_v": nrm(ks[5], (DEPTH, DEC_BATCH, PB, HB, DB)) * BETA,
        "cache_conv": nrm(ks[6], (DEPTH, DEC_BATCH, CONV_W - 1, D_FF)) * BETA,
        "t5_table": nrm(ks[7], (T5_BUCKETS, HA)) * 0.5,
        "w_in": nrm(ks[8], (DEPTH, D_MODEL, N_IN)) * (D_MODEL ** -0.5) * col_scale,
        "lambda_q1": nrm(ks[9], (DEPTH, DA)) * 0.1,
        "lambda_k1": nrm(ks[10], (DEPTH, DA)) * 0.1,
        "lambda_q2": nrm(ks[11], (DEPTH, DA)) * 0.1,
        "lambda_k2": nrm(ks[12], (DEPTH, DA)) * 0.1,
        "subln_g": 1.0 + 0.01 * nrm(ks[13], (DEPTH, 2 * DA)),
        "rel_table_b": nrm(ks[14], (DEPTH, 2 * MAX_REL + 1, HB)) * 0.5,
        "w_pa": nrm(ks[15], (DEPTH, WA, D_MODEL)) * (WA ** -0.5) * BETA,
        "w_pb": nrm(ks[16], (DEPTH, WB, D_MODEL)) * (WB ** -0.5) * BETA,
        "w_out": nrm(ks[17], (DEPTH, D_MODEL, D_MODEL)) * (D_MODEL ** -0.5) * BETA,
        "ln1_g": 1.0 + 0.01 * nrm(ks[18], (DEPTH, D_MODEL)),
        "ln1_b": 0.01 * nrm(ks[19], (DEPTH, D_MODEL)),
        "w_up": nrm(ks[20], (DEPTH, D_MODEL, D_FF)) * (D_MODEL ** -0.5) * BETA,
        "w_gate": nrm(ks[21], (DEPTH, D_MODEL, D_FF)) * (D_MODEL ** -0.5),
        "conv_w": nrm(ks[22], (DEPTH, CONV_W, D_FF)) * (CONV_W ** -0.5),
        "conv_b": 0.01 * nrm(ks[23], (DEPTH, D_FF)),
        "w_down": nrm(ks[24], (DEPTH, D_FF, D_MODEL)) * (D_FF ** -0.5) * BETA,
        "ln2_g": 1.0 + 0.01 * nrm(ks[25], (DEPTH, D_MODEL)),
        "ln2_b": 0.01 * nrm(ks[26], (DEPTH, D_MODEL)),
    }


def reference(x_prompt, x_sample, cache_a_k, cache_a_v, cache_b_k, cache_b_v, cache_conv,
              t5_table, w_in, lambda_q1, lambda_k1, lambda_q2, lambda_k2, subln_g, rel_table_b,
              w_pa, w_pb, w_out, ln1_g, ln1_b, w_up, w_gate, conv_w, conv_b, w_down, ln2_g, ln2_b):
    f32 = jnp.float32
    xp, xs = x_prompt, x_sample
    S = xp.shape[1]
    T = xs.shape[1]
    pa_k, pa_v, pb_k, pb_v, p_conv = [], [], [], [], []
    sa_k, sa_v, sb_k, sb_v, s_conv = [], [], [], [], []
    for l in range(DEPTH):
        lam_init = 0.8 - 0.6 * math.exp(-0.3 * l)
        lam = (jnp.exp(jnp.sum(lambda_q1[l].astype(f32) * lambda_k1[l].astype(f32)))
               - jnp.exp(jnp.sum(lambda_q2[l].astype(f32) * lambda_k2[l].astype(f32))) + lam_init)
        tail_w = (lam_init, subln_g[l], w_pa[l], w_pb[l], w_out[l], ln1_g[l], ln1_b[l],
                  w_up[l], w_gate[l], conv_w[l], conv_b[l], w_down[l], ln2_g[l], ln2_b[l])

        qa, ka, va, qb, kb, vb, ga, gb = split_proj(xp, w_in[l])
        oa = prompt_diff(qa, ka, va, lam, t5_table)
        ob = prompt_band(qb, kb, vb, rel_table_b[l])
        conv0 = jnp.zeros((xp.shape[0], CONV_W - 1, D_FF), xp.dtype)
        xp, conv_p = block_tail(xp, oa, ob, ga, gb, conv0, *tail_w)
        keep = min(BAND_ROWS, S)
        pa_k.append(ka)
        pa_v.append(va)
        pb_k.append(kb[:, S - keep:])
        pb_v.append(vb[:, S - keep:])
        p_conv.append(conv_p)

        P = cache_a_k.shape[2]
        PB = cache_b_k.shape[2]
        qa, ka, va, qb, kb, vb, ga, gb = split_proj(xs, w_in[l])
        qpos = P + jnp.arange(T)
        ka_all = jnp.concatenate([cache_a_k[l].astype(ka.dtype), ka], axis=1)
        va_all = jnp.concatenate([cache_a_v[l].astype(va.dtype), va], axis=1)
        oa = diff_attention(qa, ka_all, va_all, qpos, jnp.arange(P + T), lam, t5_table)
        kb_all = jnp.concatenate([cache_b_k[l].astype(kb.dtype), kb], axis=1)
        vb_all = jnp.concatenate([cache_b_v[l].astype(vb.dtype), vb], axis=1)
        kpos_b = jnp.concatenate([jnp.arange(P - PB, P), P + jnp.arange(T)])
        ob = band_attention(qb, kb_all, vb_all, qpos, kpos_b, rel_table_b[l])
        xs, conv_s = block_tail(xs, oa, ob, ga, gb, cache_conv[l], *tail_w)
        sa_k.append(ka)
        sa_v.append(va)
        sb_k.append(kb)
        sb_v.append(vb)
        s_conv.append(conv_s)

    return (xp, xs,
            jnp.stack(pa_k), jnp.stack(pa_v), jnp.stack(pb_k), jnp.stack(pb_v), jnp.stack(p_conv),
            jnp.stack(sa_k), jnp.stack(sa_v), jnp.stack(sb_k), jnp.stack(sb_v), jnp.stack(s_conv))
```

```python
import functools
import math

import jax
import jax.numpy as jnp
from jax import lax
from jax.experimental import pallas as pl
from jax.experimental.pallas import tpu as pltpu

F32 = jnp.float32
BF16 = jnp.bfloat16

CHUNK = 64
HA = 8
DA = 64
HB = 16
DB = 64
LEFT_CHUNKS = 8
MAX_REL = 128
T5_BUCKETS = 32
T5_MAX_DIST = 128
CONV_W = 3
LN_EPS = 1e-5
RMS_EPS = 1e-5
NEG = -1e30
LOG2E = math.log2(math.e)

LANES = 128
ATT_T = 256
VMEM_LIMIT = 56 * 1024 * 1024


def _cparams(sem, vmem=VMEM_LIMIT):
    return pltpu.CompilerParams(dimension_semantics=sem, vmem_limit_bytes=vmem)


def _toeplitz(vec, t):
    x = jnp.broadcast_to(vec, (t, 2 * t))
    row = lax.broadcasted_iota(jnp.int32, (t, 2 * t), 0)
    shift = 1
    while shift < t:
        x = jnp.where((row & shift) != 0, pltpu.roll(x, shift, 1), x)
        shift *= 2
    return x[:, t:]


def _t5_bucket(rel):
    nb = T5_BUCKETS // 2
    max_exact = nb // 2
    ret = jnp.where(rel > 0, nb, 0)
    n = jnp.abs(rel)
    nf = jnp.maximum(n, 1).astype(F32)
    large = max_exact + (jnp.log(nf / max_exact) / math.log(T5_MAX_DIST / max_exact)
                         * (nb - max_exact)).astype(jnp.int32)
    large = jnp.minimum(large, nb - 1)
    return ret + jnp.where(n < max_exact, n, large)


def _bias_a_kernel(tab_ref, out_ref, *, t):
    h = pl.program_id(0)
    r = lax.broadcasted_iota(jnp.int32, (1, 2 * t), 1)
    far = tab_ref[h * T5_BUCKETS + T5_BUCKETS // 2 - 1]

    def band(rel):
        bucket = _t5_bucket(rel)
        acc = jnp.zeros(rel.shape, F32)
        for j in range(T5_BUCKETS):
            acc = jnp.where(bucket == j, tab_ref[h * T5_BUCKETS + j], acc)
        return (acc - far) * LOG2E

    diag = _toeplitz(band(r - t), t)
    prev = _toeplitz(band(r - 2 * t), t)
    qi = lax.broadcasted_iota(jnp.int32, (t, t), 0)
    ki = lax.broadcasted_iota(jnp.int32, (t, t), 1)
    out_ref[0, 0] = jnp.where((ki // CHUNK) <= (qi // CHUNK), diag, NEG)
    out_ref[0, 1] = prev


def _bias_b_kernel(tab_ref, out_ref, *, t):
    h = pl.program_id(0)
    nrel = 2 * MAX_REL + 1
    r = lax.broadcasted_iota(jnp.int32, (1, 2 * t), 1)
    far = tab_ref[h * nrel]

    def band(rel):
        idx = jnp.clip(rel, -MAX_REL, MAX_REL) + MAX_REL

        def body(j, acc):
            return jnp.where(idx == j, tab_ref[h * nrel + j], acc)

        acc = lax.fori_loop(0, nrel, body, jnp.zeros(rel.shape, F32))
        return (acc - far) * LOG2E

    diag = _toeplitz(band(r - t), t)
    prev = _toeplitz(band(r - 2 * t), t)
    qi = lax.broadcasted_iota(jnp.int32, (t, t), 0)
    ki = lax.broadcasted_iota(jnp.int32, (t, t), 1)
    qc = qi // CHUNK
    out_ref[0, 0] = jnp.where((ki // CHUNK) <= qc, diag, NEG)
    out_ref[0, 1] = prev
    out_ref[0, 2] = jnp.where((ki // CHUNK) - 2 * (t // CHUNK) >= qc - LEFT_CHUNKS, 0.0, NEG)


def _bias_tiles(t5_table, rel_table):
    t = ATT_T
    assert t >= T5_MAX_DIST and t >= MAX_REL and t % CHUNK == 0
    assert 2 * t == LEFT_CHUNKS * CHUNK
    smem = pl.BlockSpec(memory_space=pltpu.SMEM)
    da = pl.pallas_call(
        functools.partial(_bias_a_kernel, t=t),
        grid=(HA,),
        in_specs=[smem],
        out_specs=pl.BlockSpec((1, 2, t, t), lambda h: (h, 0, 0, 0)),
        out_shape=jax.ShapeDtypeStruct((HA, 2, t, t), F32),
        compiler_params=_cparams(("parallel",)),
        name="bias_a",
    )(t5_table.astype(F32).T.reshape(-1))
    db = pl.pallas_call(
        functools.partial(_bias_b_kernel, t=t),
        grid=(HB,),
        in_specs=[smem],
        out_specs=pl.BlockSpec((1, 3, t, t), lambda h: (h, 0, 0, 0)),
        out_shape=jax.ShapeDtypeStruct((HB, 3, t, t), F32),
        compiler_params=_cparams(("parallel",)),
        name="bias_b",
    )(rel_table.astype(F32).T.reshape(-1))
    return da, db


def _proj_kernel(x_ref, w_ref, p_ref, ka_ref, va_ref, kb_ref, vb_ref, *, d, q_scale):
    xb = x_ref[...].astype(BF16)
    f32_outs = {1: ka_ref, 2: va_ref, 4: kb_ref, 5: vb_ref}
    for c in range(8):
        acc = jnp.dot(xb, w_ref[:, c * d:(c + 1) * d], preferred_element_type=F32)
        if c in f32_outs:
            f32_outs[c][...] = acc
        if c in (0, 3):
            acc = acc * q_scale
        p_ref[:, c * d:(c + 1) * d] = acc.astype(BF16)


def _project(x2d, w_bf16, rows_per_seq, keep, tm):
    m, d = x2d.shape
    n = w_bf16.shape[1]
    assert n == 8 * d and d == HA * 2 * DA == HB * DB
    assert m % tm == 0 and rows_per_seq % tm == 0 and keep % tm == 0
    tiles_per_seq = rows_per_seq // tm
    keep_tiles = keep // tm
    first_keep = tiles_per_seq - keep_tiles

    def keep_map(i):
        return ((i // tiles_per_seq) * keep_tiles
                + jnp.maximum(i % tiles_per_seq - first_keep, 0), 0)

    row = lambda i: (i, 0)
    n_keep = (m // rows_per_seq) * keep
    return pl.pallas_call(
        functools.partial(_proj_kernel, d=d, q_scale=DA ** -0.5 * LOG2E),
        grid=(m // tm,),
        in_specs=[pl.BlockSpec((tm, d), row),
                  pl.BlockSpec((d, n), lambda i: (0, 0), pipeline_mode=pl.Buffered(1))],
        out_specs=[pl.BlockSpec((tm, n), row),
                   pl.BlockSpec((tm, d), row), pl.BlockSpec((tm, d), row),
                   pl.BlockSpec((tm, d), keep_map), pl.BlockSpec((tm, d), keep_map)],
        out_shape=[jax.ShapeDtypeStruct((m, n), BF16),
                   jax.ShapeDtypeStruct((m, d), F32), jax.ShapeDtypeStruct((m, d), F32),
                   jax.ShapeDtypeStruct((n_keep, d), F32), jax.ShapeDtypeStruct((n_keep, d), F32)],
        compiler_params=_cparams(("arbitrary",)),
        name="in_proj",
    )(x2d, w_bf16)


def _half_masks(shape):
    lane = lax.broadcasted_iota(jnp.int32, shape, len(shape) - 1)
    return lane < (LANES // 2), lane >= (LANES // 2)


def _qk(q, k):
    return lax.dot_general(q, k, (((1,), (1,)), ((), ())), preferred_element_type=F32)


def _softmax_pv(scores, values):
    m = functools.reduce(jnp.maximum, [jnp.max(s, axis=1, keepdims=True) for s in scores])
    l = None
    o = None
    for s, v in zip(scores, values):
        p = jnp.exp2(s - m)
        ls = jnp.sum(p, axis=1, keepdims=True)
        ov = jnp.dot(p.astype(BF16), v, preferred_element_type=F32)
        l = ls if l is None else l + ls
        o = ov if o is None else o + ov
    return o / l


def _lam(lq1, lk1, lq2, lk2, lam_init):
    e1 = jnp.exp(jnp.sum(lq1 * lk1, axis=1, keepdims=True))
    e2 = jnp.exp(jnp.sum(lq2 * lk2, axis=1, keepdims=True))
    return e1 - e2 + lam_init


def _diff_epilogue(o1, o2, lam, g, lam_init):
    o = o1 - lam * o2
    o = o * lax.rsqrt(jnp.mean(o * o, axis=1, keepdims=True) + RMS_EPS)
    return o * g * (1.0 - lam_init)


def _attn_a_kernel(lq1_ref, lk1_ref, lq2_ref, lk2_ref, g_ref, q_ref, k_ref, v_ref, d_ref, o_ref,
                   m_ref, l_ref, acc_ref, *, t, lam_init):
    qi = pl.program_id(2)
    q = q_ref[0]
    lo, hi = _half_masks(q.shape)
    zero = jnp.zeros_like(q)
    qs = (jnp.where(lo, q, zero), jnp.where(hi, q, zero))

    def block(kb, bias, first):
        start = pl.multiple_of(kb * t, t)
        k = k_ref[0, pl.ds(start, t), :]
        v = v_ref[0, pl.ds(start, t), :]
        for mp in range(2):
            s = _qk(qs[mp], k)
            if bias is not None:
                s = s + bias
            m_cur = jnp.max(s, axis=1, keepdims=True)
            if first:
                m_new = m_cur
            else:
                m_prev = m_ref[mp]
                m_new = jnp.maximum(m_prev, m_cur)
                alpha = jnp.exp2(m_prev - m_new)
            p = jnp.exp2(s - m_new)
            ls = jnp.sum(p, axis=1, keepdims=True)
            pv = jnp.dot(p.astype(BF16), v, preferred_element_type=F32)
            if first:
                l_ref[mp] = ls
                acc_ref[mp] = pv
            else:
                l_ref[mp] = alpha * l_ref[mp] + ls
                acc_ref[mp] = alpha * acc_ref[mp] + pv
            m_ref[mp] = m_new

    block(qi, d_ref[0, 0], True)

    @pl.when(qi >= 1)
    def _():
        block(qi - 1, d_ref[0, 1], False)

    def far(kb, carry):
        block(kb, None, False)
        return carry

    lax.fori_loop(0, jnp.maximum(qi - 1, 0), far, 0)

    lam = _lam(lq1_ref[...], lk1_ref[...], lq2_ref[...], lk2_ref[...], lam_init)
    o1 = acc_ref[0] / l_ref[0]
    o2 = acc_ref[1] / l_ref[1]
    o_ref[0] = _diff_epilogue(o1, o2, lam, g_ref[...], lam_init).astype(o_ref.dtype)


def _attn_a_prompt(p3, da, lams, subln_g, lam_init):
    b, s, n = p3.shape
    t = ATT_T
    assert s % t == 0 and n == 8 * HA * LANES
    vec = pl.BlockSpec((1, DA), lambda bi, h, qi: (0, 0))
    return pl.pallas_call(
        functools.partial(_attn_a_kernel, t=t, lam_init=lam_init),
        grid=(b, HA, s // t),
        in_specs=[vec, vec, vec, vec,
                  pl.BlockSpec((1, LANES), lambda bi, h, qi: (0, 0)),
                  pl.BlockSpec((1, t, LANES), lambda bi, h, qi: (bi, qi, h)),
                  pl.BlockSpec((1, s, LANES), lambda bi, h, qi: (bi, 0, HA + h)),
                  pl.BlockSpec((1, s, LANES), lambda bi, h, qi: (bi, 0, 2 * HA + h)),
                  pl.BlockSpec((1, 2, t, t), lambda bi, h, qi: (h, 0, 0, 0))],
        out_specs=pl.BlockSpec((1, t, LANES), lambda bi, h, qi: (bi, qi, h)),
        out_shape=jax.ShapeDtypeStruct((b, s, HA * LANES), BF16),
        scratch_shapes=[pltpu.VMEM((2, t, 1), F32), pltpu.VMEM((2, t, 1), F32),
                        pltpu.VMEM((2, t, LANES), F32)],
        compiler_params=_cparams(("parallel", "parallel", "arbitrary")),
        name="attn_a_prompt",
    )(*lams, subln_g, p3, p3, p3, da)


def _attn_b_kernel(q_ref, k_ref, v_ref, d_ref, o_ref, *, t):
    qi = pl.program_id(2)
    q = q_ref[0]
    lo, hi = _half_masks(q.shape)
    zero = jnp.zeros_like(q)
    qs = (jnp.where(lo, q, zero), jnp.where(hi, q, zero))
    ks, vs, valid = [], [], []
    for back in range(3):
        kb = jnp.maximum(qi - back, 0)
        start = pl.multiple_of(kb * t, t)
        ks.append(k_ref[0, pl.ds(start, t), :])
        vs.append(v_ref[0, pl.ds(start, t), :])
        valid.append(qi - back >= 0)
    outs = []
    for hh in range(2):
        scores = []
        for back in range(3):
            s = _qk(qs[hh], ks[back]) + d_ref[hh, back]
            if back:
                s = jnp.where(valid[back], s, NEG)
            scores.append(s)
        outs.append(_softmax_pv(scores, vs))
    olo, _ = _half_masks(outs[0].shape)
    o_ref[0] = jnp.where(olo, outs[0], outs[1]).astype(o_ref.dtype)


def _attn_b_prompt(p3, db):
    b, s, n = p3.shape
    t = ATT_T
    hp = HB // 2
    return pl.pallas_call(
        functools.partial(_attn_b_kernel, t=t),
        grid=(b, hp, s // t),
        in_specs=[pl.BlockSpec((1, t, LANES), lambda bi, h, qi: (bi, qi, 3 * HA + h)),
                  pl.BlockSpec((1, s, LANES), lambda bi, h, qi: (bi, 0, 3 * HA + hp + h)),
                  pl.BlockSpec((1, s, LANES), lambda bi, h, qi: (bi, 0, 3 * HA + 2 * hp + h)),
                  pl.BlockSpec((2, 3, t, t), lambda bi, h, qi: (h, 0, 0, 0))],
        out_specs=pl.BlockSpec((1, t, LANES), lambda bi, h, qi: (bi, qi, h)),
        out_shape=jax.ShapeDtypeStruct((b, s, hp * LANES), BF16),
        compiler_params=_cparams(("parallel", "parallel", "arbitrary")),
        name="attn_b_prompt",
    )(p3, p3, p3, db)


def _attn_a_sample_kernel(lq1_ref, lk1_ref, lq2_ref, lk2_ref, g_ref, q_ref, kn_ref, vn_ref,
                          kc_ref, vc_ref, d_ref, o_ref, *, t, lam_init):
    q = q_ref[0]
    tq = q.shape[0]
    past = kc_ref.shape[1]
    lo, hi = _half_masks(q.shape)
    zero = jnp.zeros_like(q)
    kc = kc_ref[0].astype(BF16)
    vc = vc_ref[0].astype(BF16)
    kn = kn_ref[0]
    vn = vn_ref[0]
    outs = []
    for msk in (lo, hi):
        qm = jnp.where(msk, q, zero)
        scores = [_qk(qm, kc[:past - t]),
                  _qk(qm, kc[past - t:]) + d_ref[0, 1],
                  _qk(qm, kn) + d_ref[0, 0, :, :tq]]
        outs.append(_softmax_pv(scores, [vc[:past - t], vc[past - t:], vn]))
    lam = _lam(lq1_ref[...], lk1_ref[...], lq2_ref[...], lk2_ref[...], lam_init)
    o_ref[0] = _diff_epilogue(outs[0], outs[1], lam, g_ref[...], lam_init).astype(o_ref.dtype)


def _attn_a_sample(p3, cache_k, cache_v, da, lams, subln_g, lam_init):
    b, tq, n = p3.shape
    past = cache_k.shape[1]
    t = ATT_T
    assert tq <= CHUNK and past % CHUNK == 0 and past > t and tq % 8 == 0
    vec = pl.BlockSpec((1, DA), lambda bi, h: (0, 0))
    return pl.pallas_call(
        functools.partial(_attn_a_sample_kernel, t=t, lam_init=lam_init),
        grid=(b, HA),
        in_specs=[vec, vec, vec, vec,
                  pl.BlockSpec((1, LANES), lambda bi, h: (0, 0)),
                  pl.BlockSpec((1, tq, LANES), lambda bi, h: (bi, 0, h)),
                  pl.BlockSpec((1, tq, LANES), lambda bi, h: (bi, 0, HA + h)),
                  pl.BlockSpec((1, tq, LANES), lambda bi, h: (bi, 0, 2 * HA + h)),
                  pl.BlockSpec((1, past, LANES), lambda bi, h: (bi, 0, h)),
                  pl.BlockSpec((1, past, LANES), lambda bi, h: (bi, 0, h)),
                  pl.BlockSpec((1, 2, tq, t), lambda bi, h: (h, 0, 0, 0))],
        out_specs=pl.BlockSpec((1, tq, LANES), lambda bi, h: (bi, 0, h)),
        out_shape=jax.ShapeDtypeStruct((b, tq, HA * LANES), BF16),
        compiler_params=_cparams(("parallel", "parallel")),
        name="attn_a_sample",
    )(*lams, subln_g, p3, p3, p3, cache_k, cache_v, da)


def _attn_b_sample_kernel(q_ref, kn_ref, vn_ref, kc_ref, vc_ref, d_ref, o_ref, *, t):
    q = q_ref[0]
    tq = q.shape[0]
    past = kc_ref.shape[1]
    lo, hi = _half_masks(q.shape)
    zero = jnp.zeros_like(q)
    kc = kc_ref[0].astype(BF16)
    vc = vc_ref[0].astype(BF16)
    kn = kn_ref[0]
    vn = vn_ref[0]
    outs = []
    for hh, msk in enumerate((lo, hi)):
        qm = jnp.where(msk, q, zero)
        scores = [_qk(qm, kc[past - t:]) + d_ref[hh, 1],
                  _qk(qm, kn) + d_ref[hh, 0, :, :tq]]
        values = [vc[past - t:], vn]
        if past > t:
            scores.insert(0, _qk(qm, kc[:past - t]))
            values.insert(0, vc[:past - t])
        outs.append(_softmax_pv(scores, values))
    olo, _ = _half_masks(outs[0].shape)
    o_ref[0] = jnp.where(olo, outs[0], outs[1]).astype(o_ref.dtype)


def _attn_b_sample(p3, cache_k, cache_v, db, past_len):
    b, tq, n = p3.shape
    pb = cache_k.shape[1]
    t = ATT_T
    hp = HB // 2
    assert tq <= CHUNK and past_len % CHUNK == 0 and pb == LEFT_CHUNKS * CHUNK and pb >= t
    return pl.pallas_call(
        functools.partial(_attn_b_sample_kernel, t=t),
        grid=(b, hp),
        in_specs=[pl.BlockSpec((1, tq, LANES), lambda bi, h: (bi, 0, 3 * HA + h)),
                  pl.BlockSpec((1, tq, LANES), lambda bi, h: (bi, 0, 3 * HA + hp + h)),
                  pl.BlockSpec((1, tq, LANES), lambda bi, h: (bi, 0, 3 * HA + 2 * hp + h)),
                  pl.BlockSpec((1, pb, LANES), lambda bi, h: (bi, 0, h)),
                  pl.BlockSpec((1, pb, LANES), lambda bi, h: (bi, 0, h)),
                  pl.BlockSpec((2, 3, tq, t), lambda bi, h: (h, 0, 0, 0))],
        out_specs=pl.BlockSpec((1, tq, LANES), lambda bi, h: (bi, 0, h)),
        out_shape=jax.ShapeDtypeStruct((b, tq, hp * LANES), BF16),
        compiler_params=_cparams(("parallel", "parallel")),
        name="attn_b_sample",
    )(p3, p3, p3, cache_k, cache_v, db)


def _layer_norm(x, g, b):
    mu = jnp.mean(x, axis=1, keepdims=True)
    xc = x - mu
    var = jnp.mean(xc * xc, axis=1, keepdims=True)
    return xc * lax.rsqrt(var + LN_EPS) * g + b


def _tail_kernel(x_ref, oa_ref, ob_ref, ga_ref, gb_ref, prev_ref,
                 wpa_ref, wpb_ref, wout_ref, ln1g_ref, ln1b_ref,
                 wup_ref, wgate_ref, cw_ref, cb_ref, wdown_ref, ln2g_ref, ln2b_ref,
                 y_ref, conv_ref, u_ref, *, tm, alpha):
    ti = pl.program_id(1)
    pad = 8
    hist = CONV_W - 1

    @pl.when(ti == 0)
    def _():
        u_ref[pad - hist:pad, :] = prev_ref[0]

    ya = jnp.dot(oa_ref[0], wpa_ref[...], preferred_element_type=F32)
    yb = jnp.dot(ob_ref[0], wpb_ref[...], preferred_element_type=F32)
    merged = (jax.nn.sigmoid(ga_ref[0].astype(F32)) * ya
              + jax.nn.sigmoid(gb_ref[0].astype(F32)) * yb)
    mixed = jnp.dot(merged.astype(BF16), wout_ref[...], preferred_element_type=F32)
    h = _layer_norm(alpha * x_ref[0] + mixed, ln1g_ref[...], ln1b_ref[...])
    hb = h.astype(BF16)
    u = jnp.dot(hb, wup_ref[...], preferred_element_type=F32)
    g = jnp.dot(hb, wgate_ref[...], preferred_element_type=F32)
    u_ref[pad:pad + tm, :] = u
    uc = cb_ref[...] + u_ref[pad - 2:pad - 2 + tm, :] * cw_ref[0:1, :]
    uc = uc + u_ref[pad - 1:pad - 1 + tm, :] * cw_ref[1:2, :]
    uc = uc + u * cw_ref[2:3, :]
    last = u_ref[pad + tm - hist:pad + tm, :]
    conv_ref[0] = last
    u_ref[pad - hist:pad, :] = last
    f = jnp.dot((jax.nn.gelu(uc) * g).astype(BF16), wdown_ref[...], preferred_element_type=F32)
    y_ref[0] = _layer_norm(alpha * h + f, ln2g_ref[...], ln2b_ref[...])


def _tail(x3, oa, ob, p3, conv_prev, w, tm, alpha):
    b, s, d = x3.shape
    dff = w["w_up"].shape[1]
    assert s % tm == 0 and tm >= CONV_W - 1
    const = lambda shape: pl.BlockSpec(shape, lambda bi, ti: (0,) * len(shape),
                                       pipeline_mode=pl.Buffered(1))
    act = lambda col: pl.BlockSpec((1, tm, d), lambda bi, ti: (bi, ti, col))
    return pl.pallas_call(
        functools.partial(_tail_kernel, tm=tm, alpha=alpha),
        grid=(b, s // tm),
        in_specs=[act(0), act(0), act(0), act(6), act(7),
                  pl.BlockSpec((1, CONV_W - 1, dff), lambda bi, ti: (bi, 0, 0)),
                  const((d, d)), const((d, d)), const((d, d)), const((1, d)), const((1, d)),
                  const((d, dff)), const((d, dff)), const((CONV_W, dff)), const((1, dff)),
                  const((dff, d)), const((1, d)), const((1, d))],
        out_specs=[pl.BlockSpec((1, tm, d), lambda bi, ti: (bi, ti, 0)),
                   pl.BlockSpec((1, CONV_W - 1, dff), lambda bi, ti: (bi, 0, 0))],
        out_shape=[jax.ShapeDtypeStruct((b, s, d), F32),
                   jax.ShapeDtypeStruct((b, CONV_W - 1, dff), F32)],
        scratch_shapes=[pltpu.VMEM((tm + 8, dff), F32)],
        compiler_params=_cparams(("parallel", "arbitrary")),
        name="tail",
    )(x3, oa, ob, p3, p3, conv_prev,
      w["w_pa"], w["w_pb"], w["w_out"], w["ln1_g"], w["ln1_b"],
      w["w_up"], w["w_gate"], w["conv_w"], w["conv_b"], w["w_down"], w["ln2_g"], w["ln2_b"])


def kernel(x_prompt, x_sample, cache_a_k, cache_a_v, cache_b_k, cache_b_v, cache_conv, t5_table, w_in, lambda_q1, lambda_k1, lambda_q2, lambda_k2, subln_g, rel_table_b, w_pa, w_pb, w_out, ln1_g, ln1_b, w_up, w_gate, conv_w, conv_b, w_down, ln2_g, ln2_b):
    depth = w_in.shape[0]
    alpha = (2.0 * depth) ** 0.25
    bp, s, d = x_prompt.shape
    bs, t_new, _ = x_sample.shape
    past = cache_a_k.shape[2]
    pb = cache_b_k.shape[2]
    dff = w_up.shape[2]
    keep = min(LEFT_CHUNKS * CHUNK, s)
    assert pb == min(LEFT_CHUNKS * CHUNK, past)

    xp, xs = x_prompt, x_sample
    outs_p = [[] for _ in range(5)]
    outs_s = [[] for _ in range(5)]
    for l in range(depth):
        lam_init = 0.8 - 0.6 * math.exp(-0.3 * l)
        lams = [v[l].astype(F32).reshape(1, DA) for v in (lambda_q1, lambda_k1, lambda_q2, lambda_k2)]
        g_sub = subln_g[l].astype(F32).reshape(1, 2 * DA)
        w = {
            "w_pa": w_pa[l].astype(BF16), "w_pb": w_pb[l].astype(BF16), "w_out": w_out[l].astype(BF16),
            "ln1_g": ln1_g[l].reshape(1, d), "ln1_b": ln1_b[l].reshape(1, d),
            "w_up": w_up[l].astype(BF16), "w_gate": w_gate[l].astype(BF16),
            "conv_w": conv_w[l], "conv_b": conv_b[l].reshape(1, dff),
            "w_down": w_down[l].astype(BF16),
            "ln2_g": ln2_g[l].reshape(1, d), "ln2_b": ln2_b[l].reshape(1, d),
        }
        w_in_b = w_in[l].astype(BF16)
        da, db = _bias_tiles(t5_table, rel_table_b[l])

        p, ka, va, kb, vb = _project(xp.reshape(bp * s, d), w_in_b, s, keep, ATT_T)
        p3 = p.reshape(bp, s, 8 * d)
        oa = _attn_a_prompt(p3, da, lams, g_sub, lam_init)
        ob = _attn_b_prompt(p3, db)
        conv0 = jnp.zeros((bp, CONV_W - 1, dff), F32)
        xp, conv_p = _tail(xp, oa, ob, p3, conv0, w, ATT_T, alpha)
        outs_p[0].append(ka.reshape(bp, s, HA, 2, DA))
        outs_p[1].append(va.reshape(bp, s, HA, 2 * DA))
        outs_p[2].append(kb.reshape(bp, keep, HB, DB))
        outs_p[3].append(vb.reshape(bp, keep, HB, DB))
        outs_p[4].append(conv_p)

        p, ka, va, kb, vb = _project(xs.reshape(bs * t_new, d), w_in_b, t_new, t_new, t_new)
        p3 = p.reshape(bs, t_new, 8 * d)
        oa = _attn_a_sample(p3, cache_a_k[l].reshape(bs, past, d), cache_a_v[l].reshape(bs, past, d),
                            da, lams, g_sub, lam_init)
        ob = _attn_b_sample(p3, cache_b_k[l].reshape(bs, pb, d), cache_b_v[l].reshape(bs, pb, d),
                            db, past)
        xs, conv_s = _tail(xs, oa, ob, p3, cache_conv[l].astype(F32), w, t_new, alpha)
        outs_s[0].append(ka.reshape(bs, t_new, HA, 2, DA))
        outs_s[1].append(va.reshape(bs, t_new, HA, 2 * DA))
        outs_s[2].append(kb.reshape(bs, t_new, HB, DB))
        outs_s[3].append(vb.reshape(bs, t_new, HB, DB))
        outs_s[4].append(conv_s)

    return (xp, xs, *[jnp.stack(o) for o in outs_p], *[jnp.stack(o) for o in outs_s])
```

```python
import functools
import math

import jax
import jax.numpy as jnp
from jax import lax
from jax.experimental import pallas as pl
from jax.experimental.pallas import tpu as pltpu

F32 = jnp.float32
BF16 = jnp.bfloat16

CHUNK = 64
HA = 8
DA = 64
HB = 16
DB = 64
LEFT_CHUNKS = 8
MAX_REL = 128
T5_BUCKETS = 32
T5_MAX_DIST = 128
CONV_W = 3
LN_EPS = 1e-5
RMS_EPS = 1e-5
NEG = -1e30
LOG2E = math.log2(math.e)

LANES = 128
ATT_T = 256
ATT_TA = 512
VMEM_LIMIT = 56 * 1024 * 1024


def _cparams(sem, vmem=VMEM_LIMIT):
    return pltpu.CompilerParams(dimension_semantics=sem, vmem_limit_bytes=vmem)


def _toeplitz(vec, t):
    x = jnp.broadcast_to(vec, (t, 2 * t))
    row = lax.broadcasted_iota(jnp.int32, (t, 2 * t), 0)
    shift = 1
    while shift < t:
        x = jnp.where((row & shift) != 0, pltpu.roll(x, shift, 1), x)
        shift *= 2
    return x[:, t:]


def _t5_bucket(rel):
    nb = T5_BUCKETS // 2
    max_exact = nb // 2
    ret = jnp.where(rel > 0, nb, 0)
    n = jnp.abs(rel)
    nf = jnp.maximum(n, 1).astype(F32)
    large = max_exact + (jnp.log(nf / max_exact) / math.log(T5_MAX_DIST / max_exact)
                         * (nb - max_exact)).astype(jnp.int32)
    large = jnp.minimum(large, nb - 1)
    return ret + jnp.where(n < max_exact, n, large)


def _chunk_ids(t):
    kc = lax.broadcasted_iota(jnp.int32, (t, t), 0) // CHUNK
    qc = lax.broadcasted_iota(jnp.int32, (t, t), 1) // CHUNK
    return kc, qc


def _bias_a_kernel(tab_ref, out_ref, *, t):
    h = pl.program_id(0)
    r = lax.broadcasted_iota(jnp.int32, (1, 2 * t), 1)
    far = tab_ref[h * T5_BUCKETS + T5_BUCKETS // 2 - 1]

    def band(rel):
        bucket = _t5_bucket(rel)
        acc = jnp.zeros(rel.shape, F32)
        for j in range(T5_BUCKETS):
            acc = jnp.where(bucket == j, tab_ref[h * T5_BUCKETS + j], acc)
        return (acc - far) * LOG2E

    kc, qc = _chunk_ids(t)
    out_ref[0, 0] = jnp.where(kc <= qc, _toeplitz(band(t - r), t), NEG)
    out_ref[0, 1] = _toeplitz(band(-r), t)


def _bias_b_kernel(tab_ref, out_ref, *, t):
    h = pl.program_id(0)
    nrel = 2 * MAX_REL + 1
    r = lax.broadcasted_iota(jnp.int32, (1, 2 * t), 1)
    far = tab_ref[h * nrel]

    def band(rel):
        idx = jnp.clip(rel, -MAX_REL, MAX_REL) + MAX_REL

        def body(j, acc):
            return jnp.where(idx == j, tab_ref[h * nrel + j], acc)

        acc = lax.fori_loop(0, nrel, body, jnp.zeros(rel.shape, F32))
        return (acc - far) * LOG2E

    kc, qc = _chunk_ids(t)
    out_ref[0, 0:t, :] = jnp.where(kc - 2 * (t // CHUNK) >= qc - LEFT_CHUNKS, 0.0, NEG)
    out_ref[0, t:2 * t, :] = _toeplitz(band(-r), t)
    out_ref[0, 2 * t:3 * t, :] = jnp.where(kc <= qc, _toeplitz(band(t - r), t), NEG)


def _bias_tiles(t5_table, rel_table):
    ta, tb = ATT_TA, ATT_T
    assert ta >= T5_MAX_DIST and ta % CHUNK == 0
    assert tb >= MAX_REL and 2 * tb == LEFT_CHUNKS * CHUNK
    smem = pl.BlockSpec(memory_space=pltpu.SMEM)
    da = pl.pallas_call(
        functools.partial(_bias_a_kernel, t=ta),
        grid=(HA,),
        in_specs=[smem],
        out_specs=pl.BlockSpec((1, 2, ta, ta), lambda h: (h, 0, 0, 0)),
        out_shape=jax.ShapeDtypeStruct((HA, 2, ta, ta), F32),
        compiler_params=_cparams(("parallel",)),
        name="bias_a",
    )(t5_table.astype(F32).T.reshape(-1))
    db = pl.pallas_call(
        functools.partial(_bias_b_kernel, t=tb),
        grid=(HB,),
        in_specs=[smem],
        out_specs=pl.BlockSpec((1, 3 * tb, tb), lambda h: (h, 0, 0)),
        out_shape=jax.ShapeDtypeStruct((HB, 3 * tb, tb), F32),
        compiler_params=_cparams(("parallel",)),
        name="bias_b",
    )(rel_table.astype(F32).T.reshape(-1))
    return da, db


def _proj_kernel(x_ref, w_ref, p_ref, ka_ref, va_ref, kb_ref, vb_ref, *, d, q_scale):
    xb = x_ref[...].astype(BF16)
    f32_outs = {1: ka_ref, 2: va_ref, 4: kb_ref, 5: vb_ref}
    for c in range(8):
        acc = jnp.dot(xb, w_ref[:, c * d:(c + 1) * d], preferred_element_type=F32)
        if c in f32_outs:
            f32_outs[c][...] = acc
        if c in (0, 3):
            acc = acc * q_scale
        p_ref[:, c * d:(c + 1) * d] = acc.astype(BF16)


def _project(x2d, w_bf16, rows_per_seq, keep, tm):
    m, d = x2d.shape
    n = w_bf16.shape[1]
    assert n == 8 * d and d == HA * 2 * DA == HB * DB
    assert m % tm == 0 and rows_per_seq % tm == 0 and keep % tm == 0
    tiles_per_seq = rows_per_seq // tm
    keep_tiles = keep // tm
    first_keep = tiles_per_seq - keep_tiles

    def keep_map(i):
        return ((i // tiles_per_seq) * keep_tiles
                + jnp.maximum(i % tiles_per_seq - first_keep, 0), 0)

    row = lambda i: (i, 0)
    n_keep = (m // rows_per_seq) * keep
    return pl.pallas_call(
        functools.partial(_proj_kernel, d=d, q_scale=DA ** -0.5 * LOG2E),
        grid=(m // tm,),
        in_specs=[pl.BlockSpec((tm, d), row),
                  pl.BlockSpec((d, n), lambda i: (0, 0), pipeline_mode=pl.Buffered(1))],
        out_specs=[pl.BlockSpec((tm, n), row),
                   pl.BlockSpec((tm, d), row), pl.BlockSpec((tm, d), row),
                   pl.BlockSpec((tm, d), keep_map), pl.BlockSpec((tm, d), keep_map)],
        out_shape=[jax.ShapeDtypeStruct((m, n), BF16),
                   jax.ShapeDtypeStruct((m, d), F32), jax.ShapeDtypeStruct((m, d), F32),
                   jax.ShapeDtypeStruct((n_keep, d), F32), jax.ShapeDtypeStruct((n_keep, d), F32)],
        compiler_params=_cparams(("arbitrary",)),
        name="in_proj",
    )(x2d, w_bf16)


def _half_masks(shape):
    lane = lax.broadcasted_iota(jnp.int32, shape, len(shape) - 1)
    return lane < (LANES // 2), lane >= (LANES // 2)


def _stack_halves(q):
    lo, hi = _half_masks(q.shape)
    zero = jnp.zeros_like(q)
    return jnp.concatenate([jnp.where(lo, q, zero), jnp.where(hi, q, zero)], axis=0)


def _qk(a, b):
    return lax.dot_general(a, b, (((1,), (1,)), ((), ())), preferred_element_type=F32)


def _transpose_blocks(v_ref, vt_ref, t):
    for j in range(vt_ref.shape[0]):
        vt_ref[j] = v_ref[0, j * t:(j + 1) * t, :].astype(F32).T.astype(vt_ref.dtype)


def _softmax_pv(scores, values):
    m = functools.reduce(jnp.maximum, [jnp.max(s, axis=1, keepdims=True) for s in scores])
    l = None
    o = None
    for s, v in zip(scores, values):
        p = jnp.exp2(s - m)
        ls = jnp.sum(p, axis=1, keepdims=True)
        ov = jnp.dot(p.astype(BF16), v, preferred_element_type=F32)
        l = ls if l is None else l + ls
        o = ov if o is None else o + ov
    return o / l


def _lam(lq1, lk1, lq2, lk2, lam_init):
    e1 = jnp.exp(jnp.sum(lq1 * lk1, axis=1, keepdims=True))
    e2 = jnp.exp(jnp.sum(lq2 * lk2, axis=1, keepdims=True))
    return e1 - e2 + lam_init


def _attn_a_kernel(lq1_ref, lk1_ref, lq2_ref, lk2_ref, g_ref, q_ref, k_ref, v_ref, d_ref, o_ref,
                   vt_ref, m_ref, l_ref, acc_ref, *, t, lam_init):
    qi = pl.program_id(2)

    @pl.when(qi == 0)
    def _():
        _transpose_blocks(v_ref, vt_ref, t)

    q2 = _stack_halves(q_ref[0])

    def block(kb, bias, first):
        start = pl.multiple_of(kb * t, t)
        s = _qk(k_ref[0, pl.ds(start, t), :], q2)
        if bias is not None:
            s = s + jnp.concatenate([bias, bias], axis=1)
        m_cur = jnp.max(s, axis=0, keepdims=True)
        if first:
            m_new = m_cur
        else:
            m_prev = m_ref[...]
            m_new = jnp.maximum(m_prev, m_cur)
            alpha = jnp.exp2(m_prev - m_new)
        p = jnp.exp2(s - m_new)
        ls = jnp.sum(p, axis=0, keepdims=True)
        pv = jnp.dot(vt_ref[kb], p.astype(BF16), preferred_element_type=F32)
        if first:
            l_ref[...] = ls
            acc_ref[...] = pv
        else:
            l_ref[...] = alpha * l_ref[...] + ls
            acc_ref[...] = alpha * acc_ref[...] + pv
        m_ref[...] = m_new

    block(qi, d_ref[0, 0], True)

    @pl.when(qi >= 1)
    def _():
        block(qi - 1, d_ref[0, 1], False)

    def far(kb, carry):
        block(kb, None, False)
        return carry

    lax.fori_loop(0, jnp.maximum(qi - 1, 0), far, 0)

    lam = _lam(lq1_ref[...], lk1_ref[...], lq2_ref[...], lk2_ref[...], lam_init)
    o = acc_ref[...] * (1.0 / l_ref[...])
    o = o[:, :t] - lam * o[:, t:]
    o = o * lax.rsqrt(jnp.mean(o * o, axis=0, keepdims=True) + RMS_EPS)
    o = o * (g_ref[...] * (1.0 - lam_init))
    o_ref[0] = o.T.astype(o_ref.dtype)


def _attn_a_prompt(p3, da, lams, g_col, lam_init):
    b, s, n = p3.shape
    t = ATT_TA
    assert s % t == 0 and n == 8 * HA * LANES
    vec = pl.BlockSpec((1, DA), lambda bi, h, qi: (0, 0))
    return pl.pallas_call(
        functools.partial(_attn_a_kernel, t=t, lam_init=lam_init),
        grid=(b, HA, s // t),
        in_specs=[vec, vec, vec, vec,
                  pl.BlockSpec((LANES, 1), lambda bi, h, qi: (0, 0)),
                  pl.BlockSpec((1, t, LANES), lambda bi, h, qi: (bi, qi, h)),
                  pl.BlockSpec((1, s, LANES), lambda bi, h, qi: (bi, 0, HA + h)),
                  pl.BlockSpec((1, s, LANES), lambda bi, h, qi: (bi, 0, 2 * HA + h)),
                  pl.BlockSpec((1, 2, t, t), lambda bi, h, qi: (h, 0, 0, 0))],
        out_specs=pl.BlockSpec((1, t, LANES), lambda bi, h, qi: (bi, qi, h)),
        out_shape=jax.ShapeDtypeStruct((b, s, HA * LANES), BF16),
        scratch_shapes=[pltpu.VMEM((s // t, LANES, t), BF16),
                        pltpu.VMEM((1, 2 * t), F32), pltpu.VMEM((1, 2 * t), F32),
                        pltpu.VMEM((LANES, 2 * t), F32)],
        compiler_params=_cparams(("parallel", "parallel", "arbitrary")),
        name="attn_a_prompt",
    )(*lams, g_col, p3, p3, p3, da)


def _attn_b_kernel(q_ref, k_ref, v_ref, d_ref, o_ref, vt_ref, *, t):
    qi = pl.program_id(2)

    @pl.when(qi == 0)
    def _():
        _transpose_blocks(v_ref, vt_ref, t)

    q2 = _stack_halves(q_ref[0])

    def run(first_blk, nblk):
        start = pl.multiple_of(first_blk * t, t)
        rows = slice((3 - nblk) * t, 3 * t)
        s = _qk(k_ref[0, pl.ds(start, nblk * t), :], q2)
        s = s + jnp.concatenate([d_ref[0, rows, :], d_ref[1, rows, :]], axis=1)
        p = jnp.exp2(s - jnp.max(s, axis=0, keepdims=True))
        l = jnp.sum(p, axis=0, keepdims=True)
        p = p.astype(BF16)
        o = None
        for j in range(nblk):
            pv = jnp.dot(vt_ref[first_blk + j], p[j * t:(j + 1) * t, :], preferred_element_type=F32)
            o = pv if o is None else o + pv
        o = o * (1.0 / l)
        first_head = lax.broadcasted_iota(jnp.int32, (LANES, t), 0) < DB
        o_ref[0] = jnp.where(first_head, o[:, :t], o[:, t:]).T.astype(o_ref.dtype)

    @pl.when(qi == 0)
    def _():
        run(0, 1)

    @pl.when(qi == 1)
    def _():
        run(0, 2)

    @pl.when(qi >= 2)
    def _():
        run(qi - 2, 3)


def _attn_b_prompt(p3, db):
    b, s, n = p3.shape
    t = ATT_T
    hp = HB // 2
    assert s % t == 0 and s // t >= 2
    return pl.pallas_call(
        functools.partial(_attn_b_kernel, t=t),
        grid=(b, hp, s // t),
        in_specs=[pl.BlockSpec((1, t, LANES), lambda bi, h, qi: (bi, qi, 3 * HA + h)),
                  pl.BlockSpec((1, s, LANES), lambda bi, h, qi: (bi, 0, 3 * HA + hp + h)),
                  pl.BlockSpec((1, s, LANES), lambda bi, h, qi: (bi, 0, 3 * HA + 2 * hp + h)),
                  pl.BlockSpec((2, 3 * t, t), lambda bi, h, qi: (h, 0, 0))],
        out_specs=pl.BlockSpec((1, t, LANES), lambda bi, h, qi: (bi, qi, h)),
        out_shape=jax.ShapeDtypeStruct((b, s, hp * LANES), BF16),
        scratch_shapes=[pltpu.VMEM((s // t, LANES, t), BF16)],
        compiler_params=_cparams(("parallel", "parallel", "arbitrary")),
        name="attn_b_prompt",
    )(p3, p3, p3, db)


def _attn_a_sample_kernel(lq1_ref, lk1_ref, lq2_ref, lk2_ref, g_ref, q_ref, kn_ref, vn_ref,
                          kc_ref, vc_ref, d_ref, o_ref, *, t, lam_init):
    q = q_ref[0]
    tq = q.shape[0]
    past = kc_ref.shape[1]
    lo, hi = _half_masks(q.shape)
    zero = jnp.zeros_like(q)
    kc = kc_ref[0].astype(BF16)
    vc = vc_ref[0].astype(BF16)
    kn = kn_ref[0]
    vn = vn_ref[0]
    bias_near = d_ref[0, 1].T[:tq]
    bias_new = d_ref[0, 0, :LANES, :].T[:tq, :tq]
    outs = []
    for msk in (lo, hi):
        qm = jnp.where(msk, q, zero)
        scores = [_qk(qm, kc[:past - t]),
                  _qk(qm, kc[past - t:]) + bias_near,
                  _qk(qm, kn) + bias_new]
        outs.append(_softmax_pv(scores, [vc[:past - t], vc[past - t:], vn]))
    lam = _lam(lq1_ref[...], lk1_ref[...], lq2_ref[...], lk2_ref[...], lam_init)
    o = outs[0] - lam * outs[1]
    o = o * lax.rsqrt(jnp.mean(o * o, axis=1, keepdims=True) + RMS_EPS)
    o_ref[0] = (o * g_ref[...] * (1.0 - lam_init)).astype(o_ref.dtype)


def _attn_a_sample(p3, cache_k, cache_v, da, lams, g_row, lam_init):
    b, tq, n = p3.shape
    past = cache_k.shape[1]
    t = ATT_TA
    assert tq <= CHUNK and past % CHUNK == 0 and past > t and tq % 8 == 0
    vec = pl.BlockSpec((1, DA), lambda bi, h: (0, 0))
    return pl.pallas_call(
        functools.partial(_attn_a_sample_kernel, t=t, lam_init=lam_init),
        grid=(b, HA),
        in_specs=[vec, vec, vec, vec,
                  pl.BlockSpec((1, LANES), lambda bi, h: (0, 0)),
                  pl.BlockSpec((1, tq, LANES), lambda bi, h: (bi, 0, h)),
                  pl.BlockSpec((1, tq, LANES), lambda bi, h: (bi, 0, HA + h)),
                  pl.BlockSpec((1, tq, LANES), lambda bi, h: (bi, 0, 2 * HA + h)),
                  pl.BlockSpec((1, past, LANES), lambda bi, h: (bi, 0, h)),
                  pl.BlockSpec((1, past, LANES), lambda bi, h: (bi, 0, h)),
                  pl.BlockSpec((1, 2, t, LANES), lambda bi, h: (h, 0, 0, 0))],
        out_specs=pl.BlockSpec((1, tq, LANES), lambda bi, h: (bi, 0, h)),
        out_shape=jax.ShapeDtypeStruct((b, tq, HA * LANES), BF16),
        compiler_params=_cparams(("parallel", "parallel")),
        name="attn_a_sample",
    )(*lams, g_row, p3, p3, p3, cache_k, cache_v, da)


def _attn_b_sample_kernel(q_ref, kn_ref, vn_ref, kc_ref, vc_ref, d_ref, o_ref, *, t):
    q = q_ref[0]
    tq = q.shape[0]
    past = kc_ref.shape[1]
    lo, hi = _half_masks(q.shape)
    zero = jnp.zeros_like(q)
    kc = kc_ref[0].astype(BF16)
    vc = vc_ref[0].astype(BF16)
    kn = kn_ref[0]
    vn = vn_ref[0]
    outs = []
    for hh, msk in enumerate((lo, hi)):
        qm = jnp.where(msk, q, zero)
        bias_near = d_ref[hh, t:2 * t, :].T[:tq]
        bias_new = d_ref[hh, 2 * t:2 * t + LANES, :].T[:tq, :tq]
        scores = [_qk(qm, kc[past - t:]) + bias_near,
                  _qk(qm, kn) + bias_new]
        values = [vc[past - t:], vn]
        if past > t:
            scores.insert(0, _qk(qm, kc[:past - t]))
            values.insert(0, vc[:past - t])
        outs.append(_softmax_pv(scores, values))
    olo, _ = _half_masks(outs[0].shape)
    o_ref[0] = jnp.where(olo, outs[0], outs[1]).astype(o_ref.dtype)


def _attn_b_sample(p3, cache_k, cache_v, db, past_len):
    b, tq, n = p3.shape
    pb = cache_k.shape[1]
    t = ATT_T
    hp = HB // 2
    assert tq <= CHUNK and past_len % CHUNK == 0 and pb == LEFT_CHUNKS * CHUNK and pb >= t
    return pl.pallas_call(
        functools.partial(_attn_b_sample_kernel, t=t),
        grid=(b, hp),
        in_specs=[pl.BlockSpec((1, tq, LANES), lambda bi, h: (bi, 0, 3 * HA + h)),
                  pl.BlockSpec((1, tq, LANES), lambda bi, h: (bi, 0, 3 * HA + hp + h)),
                  pl.BlockSpec((1, tq, LANES), lambda bi, h: (bi, 0, 3 * HA + 2 * hp + h)),
                  pl.BlockSpec((1, pb, LANES), lambda bi, h: (bi, 0, h)),
                  pl.BlockSpec((1, pb, LANES), lambda bi, h: (bi, 0, h)),
                  pl.BlockSpec((2, 3 * t, LANES), lambda bi, h: (h, 0, 0))],
        out_specs=pl.BlockSpec((1, tq, LANES), lambda bi, h: (bi, 0, h)),
        out_shape=jax.ShapeDtypeStruct((b, tq, hp * LANES), BF16),
        compiler_params=_cparams(("parallel", "parallel")),
        name="attn_b_sample",
    )(p3, p3, p3, cache_k, cache_v, db)


def _layer_norm(x, g, b):
    mu = jnp.mean(x, axis=1, keepdims=True)
    xc = x - mu
    var = jnp.mean(xc * xc, axis=1, keepdims=True)
    return xc * lax.rsqrt(var + LN_EPS) * g + b


def _tail_kernel(x_ref, oa_ref, ob_ref, ga_ref, gb_ref, prev_ref,
                 wpa_ref, wpb_ref, wout_ref, ln1g_ref, ln1b_ref,
                 wup_ref, wgate_ref, cw_ref, cb_ref, wdown_ref, ln2g_ref, ln2b_ref,
                 y_ref, conv_ref, u_ref, *, tm, alpha):
    ti = pl.program_id(1)
    pad = 8
    hist = CONV_W - 1

    @pl.when(ti == 0)
    def _():
        u_ref[pad - hist:pad, :] = prev_ref[0]

    ya = jnp.dot(oa_ref[0], wpa_ref[...], preferred_element_type=F32)
    yb = jnp.dot(ob_ref[0], wpb_ref[...], preferred_element_type=F32)
    merged = (jax.nn.sigmoid(ga_ref[0].astype(F32)) * ya
              + jax.nn.sigmoid(gb_ref[0].astype(F32)) * yb)
    mixed = jnp.dot(merged.astype(BF16), wout_ref[...], preferred_element_type=F32)
    h = _layer_norm(alpha * x_ref[0] + mixed, ln1g_ref[...], ln1b_ref[...])
    hb = h.astype(BF16)
    u = jnp.dot(hb, wup_ref[...], preferred_element_type=F32)
    g = jnp.dot(hb, wgate_ref[...], preferred_element_type=F32)
    u_ref[pad:pad + tm, :] = u
    uc = cb_ref[...] + u_ref[pad - 2:pad - 2 + tm, :] * cw_ref[0:1, :]
    uc = uc + u_ref[pad - 1:pad - 1 + tm, :] * cw_ref[1:2, :]
    uc = uc + u * cw_ref[2:3, :]
    last = u_ref[pad + tm - hist:pad + tm, :]
    conv_ref[0] = last
    u_ref[pad - hist:pad, :] = last
    f = jnp.dot((jax.nn.gelu(uc) * g).astype(BF16), wdown_ref[...], preferred_element_type=F32)
    y_ref[0] = _layer_norm(alpha * h + f, ln2g_ref[...], ln2b_ref[...])


def _tail(x3, oa, ob, p3, conv_prev, w, tm, alpha):
    b, s, d = x3.shape
    dff = w["w_up"].shape[1]
    assert s % tm == 0 and tm >= CONV_W - 1
    const = lambda shape: pl.BlockSpec(shape, lambda bi, ti: (0,) * len(shape),
                                       pipeline_mode=pl.Buffered(1))
    act = lambda col: pl.BlockSpec((1, tm, d), lambda bi, ti: (bi, ti, col))
    return pl.pallas_call(
        functools.partial(_tail_kernel, tm=tm, alpha=alpha),
        grid=(b, s // tm),
        in_specs=[act(0), act(0), act(0), act(6), act(7),
                  pl.BlockSpec((1, CONV_W - 1, dff), lambda bi, ti: (bi, 0, 0)),
                  const((d, d)), const((d, d)), const((d, d)), const((1, d)), const((1, d)),
                  const((d, dff)), const((d, dff)), const((CONV_W, dff)), const((1, dff)),
                  const((dff, d)), const((1, d)), const((1, d))],
        out_specs=[pl.BlockSpec((1, tm, d), lambda bi, ti: (bi, ti, 0)),
                   pl.BlockSpec((1, CONV_W - 1, dff), lambda bi, ti: (bi, 0, 0))],
        out_shape=[jax.ShapeDtypeStruct((b, s, d), F32),
                   jax.ShapeDtypeStruct((b, CONV_W - 1, dff), F32)],
        scratch_shapes=[pltpu.VMEM((tm + 8, dff), F32)],
        compiler_params=_cparams(("parallel", "arbitrary")),
        name="tail",
    )(x3, oa, ob, p3, p3, conv_prev,
      w["w_pa"], w["w_pb"], w["w_out"], w["ln1_g"], w["ln1_b"],
      w["w_up"], w["w_gate"], w["conv_w"], w["conv_b"], w["w_down"], w["ln2_g"], w["ln2_b"])


def kernel(x_prompt, x_sample, cache_a_k, cache_a_v, cache_b_k, cache_b_v, cache_conv, t5_table, w_in, lambda_q1, lambda_k1, lambda_q2, lambda_k2, subln_g, rel_table_b, w_pa, w_pb, w_out, ln1_g, ln1_b, w_up, w_gate, conv_w, conv_b, w_down, ln2_g, ln2_b):
    depth = w_in.shape[0]
    alpha = (2.0 * depth) ** 0.25
    bp, s, d = x_prompt.shape
    bs, t_new, _ = x_sample.shape
    past = cache_a_k.shape[2]
    pb = cache_b_k.shape[2]
    dff = w_up.shape[2]
    keep = min(LEFT_CHUNKS * CHUNK, s)
    assert pb == min(LEFT_CHUNKS * CHUNK, past)

    xp, xs = x_prompt, x_sample
    outs_p = [[] for _ in range(5)]
    outs_s = [[] for _ in range(5)]
    for l in range(depth):
        lam_init = 0.8 - 0.6 * math.exp(-0.3 * l)
        lams = [v[l].astype(F32).reshape(1, DA) for v in (lambda_q1, lambda_k1, lambda_q2, lambda_k2)]
        g_row = subln_g[l].astype(F32).reshape(1, 2 * DA)
        g_col = subln_g[l].astype(F32).reshape(2 * DA, 1)
        w = {
            "w_pa": w_pa[l].astype(BF16), "w_pb": w_pb[l].astype(BF16), "w_out": w_out[l].astype(BF16),
            "ln1_g": ln1_g[l].reshape(1, d), "ln1_b": ln1_b[l].reshape(1, d),
            "w_up": w_up[l].astype(BF16), "w_gate": w_gate[l].astype(BF16),
            "conv_w": conv_w[l], "conv_b": conv_b[l].reshape(1, dff),
            "w_down": w_down[l].astype(BF16),
            "ln2_g": ln2_g[l].reshape(1, d), "ln2_b": ln2_b[l].reshape(1, d),
        }
        w_in_b = w_in[l].astype(BF16)
        da, db = _bias_tiles(t5_table, rel_table_b[l])

        p, ka, va, kb, vb = _project(xp.reshape(bp * s, d), w_in_b, s, keep, ATT_T)
        p3 = p.reshape(bp, s, 8 * d)
        oa = _attn_a_prompt(p3, da, lams, g_col, lam_init)
        ob = _attn_b_prompt(p3, db)
        conv0 = jnp.zeros((bp, CONV_W - 1, dff), F32)
        xp, conv_p = _tail(xp, oa, ob, p3, conv0, w, ATT_T, alpha)
        outs_p[0].append(ka.reshape(bp, s, HA, 2, DA))
        outs_p[1].append(va.reshape(bp, s, HA, 2 * DA))
        outs_p[2].append(kb.reshape(bp, keep, HB, DB))
        outs_p[3].append(vb.reshape(bp, keep, HB, DB))
        outs_p[4].append(conv_p)

        p, ka, va, kb, vb = _project(xs.reshape(bs * t_new, d), w_in_b, t_new, t_new, t_new)
        p3 = p.reshape(bs, t_new, 8 * d)
        oa = _attn_a_sample(p3, cache_a_k[l].reshape(bs, past, d), cache_a_v[l].reshape(bs, past, d),
                            da, lams, g_row, lam_init)
        ob = _attn_b_sample(p3, cache_b_k[l].reshape(bs, pb, d), cache_b_v[l].reshape(bs, pb, d),
                            db, past)
        xs, conv_s = _tail(xs, oa, ob, p3, cache_conv[l].astype(F32), w, t_new, alpha)
        outs_s[0].append(ka.reshape(bs, t_new, HA, 2, DA))
        outs_s[1].append(va.reshape(bs, t_new, HA, 2 * DA))
        outs_s[2].append(kb.reshape(bs, t_new, HB, DB))
        outs_s[3].append(vb.reshape(bs, t_new, HB, DB))
        outs_s[4].append(conv_s)

    return (xp, xs, *[jnp.stack(o) for o in outs_p], *[jnp.stack(o) for o in outs_s])
```

```python
import functools
import math

import jax
import jax.numpy as jnp
from jax import lax
from jax.experimental import pallas as pl
from jax.experimental.pallas import tpu as pltpu

F32 = jnp.float32
BF16 = jnp.bfloat16

CHUNK = 64
HA = 8
DA = 64
HB = 16
DB = 64
LEFT_CHUNKS = 8
MAX_REL = 128
T5_BUCKETS = 32
T5_MAX_DIST = 128
CONV_W = 3
LN_EPS = 1e-5
RMS_EPS = 1e-5
NEG = -1e30
LOG2E = math.log2(math.e)

LANES = 128
ATT_T = 256
ATT_TA = 512
VMEM_LIMIT = 56 * 1024 * 1024


def _cparams(sem, vmem=VMEM_LIMIT):
    return pltpu.CompilerParams(dimension_semantics=sem, vmem_limit_bytes=vmem)


def _toeplitz(vec, t):
    x = jnp.broadcast_to(vec, (t, 2 * t))
    row = lax.broadcasted_iota(jnp.int32, (t, 2 * t), 0)
    shift = 1
    while shift < t:
        x = jnp.where((row & shift) != 0, pltpu.roll(x, shift, 1), x)
        shift *= 2
    return x[:, t:]


def _t5_bucket(rel):
    nb = T5_BUCKETS // 2
    max_exact = nb // 2
    ret = jnp.where(rel > 0, nb, 0)
    n = jnp.abs(rel)
    nf = jnp.maximum(n, 1).astype(F32)
    large = max_exact + (jnp.log(nf / max_exact) / math.log(T5_MAX_DIST / max_exact)
                         * (nb - max_exact)).astype(jnp.int32)
    large = jnp.minimum(large, nb - 1)
    return ret + jnp.where(n < max_exact, n, large)


def _chunk_ids(t):
    kc = lax.broadcasted_iota(jnp.int32, (t, t), 0) // CHUNK
    qc = lax.broadcasted_iota(jnp.int32, (t, t), 1) // CHUNK
    return kc, qc


def _bias_a_kernel(tab_ref, out_ref, *, t):
    h = pl.program_id(0)
    r = lax.broadcasted_iota(jnp.int32, (1, 2 * t), 1)
    far = tab_ref[h * T5_BUCKETS + T5_BUCKETS // 2 - 1]

    def band(rel):
        bucket = _t5_bucket(rel)
        acc = jnp.zeros(rel.shape, F32)
        for j in range(T5_BUCKETS):
            acc = jnp.where(bucket == j, tab_ref[h * T5_BUCKETS + j], acc)
        return (acc - far) * LOG2E

    kc, qc = _chunk_ids(t)
    out_ref[0, 0] = jnp.where(kc <= qc, _toeplitz(band(t - r), t), NEG)
    out_ref[0, 1] = _toeplitz(band(-r), t)


def _bias_b_kernel(tab_ref, out_ref, *, t):
    h = pl.program_id(0)
    nrel = 2 * MAX_REL + 1
    r = lax.broadcasted_iota(jnp.int32, (1, 2 * t), 1)
    far = tab_ref[h * nrel]

    def band(rel):
        idx = jnp.clip(rel, -MAX_REL, MAX_REL) + MAX_REL

        def body(j, acc):
            return jnp.where(idx == j, tab_ref[h * nrel + j], acc)

        acc = lax.fori_loop(0, nrel, body, jnp.zeros(rel.shape, F32))
        return (acc - far) * LOG2E

    kc, qc = _chunk_ids(t)
    out_ref[0, 0:t, :] = jnp.where(kc - 2 * (t // CHUNK) >= qc - LEFT_CHUNKS, 0.0, NEG)
    out_ref[0, t:2 * t, :] = _toeplitz(band(-r), t)
    out_ref[0, 2 * t:3 * t, :] = jnp.where(kc <= qc, _toeplitz(band(t - r), t), NEG)


def _bias_tiles(t5_table, rel_table):
    ta, tb = ATT_TA, ATT_T
    assert ta >= T5_MAX_DIST and ta % CHUNK == 0
    assert tb >= MAX_REL and 2 * tb == LEFT_CHUNKS * CHUNK
    smem = pl.BlockSpec(memory_space=pltpu.SMEM)
    da = pl.pallas_call(
        functools.partial(_bias_a_kernel, t=ta),
        grid=(HA,),
        in_specs=[smem],
        out_specs=pl.BlockSpec((1, 2, ta, ta), lambda h: (h, 0, 0, 0)),
        out_shape=jax.ShapeDtypeStruct((HA, 2, ta, ta), F32),
        compiler_params=_cparams(("parallel",)),
        name="bias_a",
    )(t5_table.astype(F32).T.reshape(-1))
    db = pl.pallas_call(
        functools.partial(_bias_b_kernel, t=tb),
        grid=(HB,),
        in_specs=[smem],
        out_specs=pl.BlockSpec((1, 3 * tb, tb), lambda h: (h, 0, 0)),
        out_shape=jax.ShapeDtypeStruct((HB, 3 * tb, tb), F32),
        compiler_params=_cparams(("parallel",)),
        name="bias_b",
    )(rel_table.astype(F32).T.reshape(-1))
    return da, db


def _proj_kernel(x_ref, w_ref, p_ref, ka_ref, va_ref, kb_ref, vb_ref, *, d, q_scale):
    xb = x_ref[...].astype(BF16)
    f32_outs = {1: ka_ref, 2: va_ref, 4: kb_ref, 5: vb_ref}
    for c in range(8):
        acc = jnp.dot(xb, w_ref[:, c * d:(c + 1) * d], preferred_element_type=F32)
        if c in f32_outs:
            f32_outs[c][...] = acc
        if c in (0, 3):
            acc = acc * q_scale
        p_ref[:, c * d:(c + 1) * d] = acc.astype(BF16)


def _proj_cache_kernel(x_ref, w_ref, p_ref, kat_ref, va_ref, kbt_ref, vbt_ref, *,
                       d, q_scale, tm, first_keep, tiles_per_seq):
    t = pl.program_id(0) % tiles_per_seq
    xb = x_ref[...].astype(BF16)
    for c in range(8):
        acc = jnp.dot(xb, w_ref[:, c * d:(c + 1) * d], preferred_element_type=F32)
        if c == 1:
            kat_ref[0] = acc.T
        elif c == 2:
            for h in range(HA):
                va_ref[pl.ds(h, tm, stride=HA), :] = acc[:, h * LANES:(h + 1) * LANES]
        elif c in (4, 5):
            out = kbt_ref if c == 4 else vbt_ref

            @pl.when(t >= first_keep)
            def _(acc=acc, out=out):
                out[0] = acc.T
        if c in (0, 3):
            acc = acc * q_scale
        p_ref[:, c * d:(c + 1) * d] = acc.astype(BF16)


def _project_prompt(x3, w_bf16, keep, tm):
    b, s, d = x3.shape
    n = w_bf16.shape[1]
    assert n == 8 * d and d == HA * 2 * DA == HB * DB == HA * LANES
    assert s % tm == 0 and keep % tm == 0 and tm % LANES == 0
    tiles_per_seq = s // tm
    first_keep = tiles_per_seq - keep // tm
    m = b * s
    row = lambda i: (i, 0)
    col = lambda i: (i // tiles_per_seq, 0, i % tiles_per_seq)
    keep_col = lambda i: (i // tiles_per_seq, 0, jnp.maximum(i % tiles_per_seq - first_keep, 0))
    return pl.pallas_call(
        functools.partial(_proj_cache_kernel, d=d, q_scale=DA ** -0.5 * LOG2E, tm=tm,
                          first_keep=first_keep, tiles_per_seq=tiles_per_seq),
        grid=(m // tm,),
        in_specs=[pl.BlockSpec((tm, d), row),
                  pl.BlockSpec((d, n), lambda i: (0, 0), pipeline_mode=pl.Buffered(1))],
        out_specs=[pl.BlockSpec((tm, n), row),
                   pl.BlockSpec((1, d, tm), col), pl.BlockSpec((tm * HA, LANES), row),
                   pl.BlockSpec((1, d, tm), keep_col), pl.BlockSpec((1, d, tm), keep_col)],
        out_shape=[jax.ShapeDtypeStruct((m, n), BF16),
                   jax.ShapeDtypeStruct((b, d, s), F32), jax.ShapeDtypeStruct((m * HA, LANES), F32),
                   jax.ShapeDtypeStruct((b, d, keep), F32), jax.ShapeDtypeStruct((b, d, keep), F32)],
        compiler_params=_cparams(("arbitrary",)),
        name="in_proj_prompt",
    )(x3.reshape(m, d), w_bf16)


def _project(x2d, w_bf16, rows_per_seq, keep, tm):
    m, d = x2d.shape
    n = w_bf16.shape[1]
    assert n == 8 * d and d == HA * 2 * DA == HB * DB
    assert m % tm == 0 and rows_per_seq % tm == 0 and keep % tm == 0
    tiles_per_seq = rows_per_seq // tm
    keep_tiles = keep // tm
    first_keep = tiles_per_seq - keep_tiles

    def keep_map(i):
        return ((i // tiles_per_seq) * keep_tiles
                + jnp.maximum(i % tiles_per_seq - first_keep, 0), 0)

    row = lambda i: (i, 0)
    n_keep = (m // rows_per_seq) * keep
    return pl.pallas_call(
        functools.partial(_proj_kernel, d=d, q_scale=DA ** -0.5 * LOG2E),
        grid=(m // tm,),
        in_specs=[pl.BlockSpec((tm, d), row),
                  pl.BlockSpec((d, n), lambda i: (0, 0), pipeline_mode=pl.Buffered(1))],
        out_specs=[pl.BlockSpec((tm, n), row),
                   pl.BlockSpec((tm, d), row), pl.BlockSpec((tm, d), row),
                   pl.BlockSpec((tm, d), keep_map), pl.BlockSpec((tm, d), keep_map)],
        out_shape=[jax.ShapeDtypeStruct((m, n), BF16),
                   jax.ShapeDtypeStruct((m, d), F32), jax.ShapeDtypeStruct((m, d), F32),
                   jax.ShapeDtypeStruct((n_keep, d), F32), jax.ShapeDtypeStruct((n_keep, d), F32)],
        compiler_params=_cparams(("arbitrary",)),
        name="in_proj",
    )(x2d, w_bf16)


def _half_masks(shape):
    lane = lax.broadcasted_iota(jnp.int32, shape, len(shape) - 1)
    return lane < (LANES // 2), lane >= (LANES // 2)


def _stack_halves(q):
    lo, hi = _half_masks(q.shape)
    zero = jnp.zeros_like(q)
    return jnp.concatenate([jnp.where(lo, q, zero), jnp.where(hi, q, zero)], axis=0)


def _qk(a, b):
    return lax.dot_general(a, b, (((1,), (1,)), ((), ())), preferred_element_type=F32)


def _transpose_blocks(v_ref, vt_ref, t):
    for j in range(vt_ref.shape[0]):
        vt_ref[j] = v_ref[0, j * t:(j + 1) * t, :].astype(F32).T.astype(vt_ref.dtype)


def _softmax_pv(scores, values, values_t=None):
    values_t = values_t or (False,) * len(values)
    m = functools.reduce(jnp.maximum, [jnp.max(s, axis=1, keepdims=True) for s in scores])
    l = None
    o = None
    for s, v, vt in zip(scores, values, values_t):
        p = jnp.exp2(s - m)
        ls = jnp.sum(p, axis=1, keepdims=True)
        pb = p.astype(BF16)
        ov = _qk(pb, v) if vt else jnp.dot(pb, v, preferred_element_type=F32)
        l = ls if l is None else l + ls
        o = ov if o is None else o + ov
    return o / l


def _lam(lq1, lk1, lq2, lk2, lam_init):
    e1 = jnp.exp(jnp.sum(lq1 * lk1, axis=1, keepdims=True))
    e2 = jnp.exp(jnp.sum(lq2 * lk2, axis=1, keepdims=True))
    return e1 - e2 + lam_init


def _attn_a_kernel(lq1_ref, lk1_ref, lq2_ref, lk2_ref, g_ref, q_ref, k_ref, v_ref, d_ref, o_ref,
                   vt_ref, m_ref, l_ref, acc_ref, *, t, lam_init):
    qi = pl.program_id(2)

    @pl.when(qi == 0)
    def _():
        _transpose_blocks(v_ref, vt_ref, t)

    q2 = _stack_halves(q_ref[0])

    def block(kb, bias, first):
        start = pl.multiple_of(kb * t, t)
        s = _qk(k_ref[0, pl.ds(start, t), :], q2)
        if bias is not None:
            s = s + jnp.concatenate([bias, bias], axis=1)
        m_cur = jnp.max(s, axis=0, keepdims=True)
        if first:
            m_new = m_cur
        else:
            m_prev = m_ref[...]
            m_new = jnp.maximum(m_prev, m_cur)
            alpha = jnp.exp2(m_prev - m_new)
        p = jnp.exp2(s - m_new)
        ls = jnp.sum(p, axis=0, keepdims=True)
        pv = jnp.dot(vt_ref[kb], p.astype(BF16), preferred_element_type=F32)
        if first:
            l_ref[...] = ls
            acc_ref[...] = pv
        else:
            l_ref[...] = alpha * l_ref[...] + ls
            acc_ref[...] = alpha * acc_ref[...] + pv
        m_ref[...] = m_new

    block(qi, d_ref[0, 0], True)

    @pl.when(qi >= 1)
    def _():
        block(qi - 1, d_ref[0, 1], False)

    def far(kb, carry):
        block(kb, None, False)
        return carry

    lax.fori_loop(0, jnp.maximum(qi - 1, 0), far, 0)

    lam = _lam(lq1_ref[...], lk1_ref[...], lq2_ref[...], lk2_ref[...], lam_init)
    o = acc_ref[...] * (1.0 / l_ref[...])
    o = o[:, :t] - lam * o[:, t:]
    o = o * lax.rsqrt(jnp.mean(o * o, axis=0, keepdims=True) + RMS_EPS)
    o = o * (g_ref[...] * (1.0 - lam_init))
    o_ref[0] = o.T.astype(o_ref.dtype)


def _attn_a_prompt(p3, da, lams, g_col, lam_init):
    b, s, n = p3.shape
    t = ATT_TA
    assert s % t == 0 and n == 8 * HA * LANES
    vec = pl.BlockSpec((1, DA), lambda bi, h, qi: (0, 0))
    return pl.pallas_call(
        functools.partial(_attn_a_kernel, t=t, lam_init=lam_init),
        grid=(b, HA, s // t),
        in_specs=[vec, vec, vec, vec,
                  pl.BlockSpec((LANES, 1), lambda bi, h, qi: (0, 0)),
                  pl.BlockSpec((1, t, LANES), lambda bi, h, qi: (bi, qi, h)),
                  pl.BlockSpec((1, s, LANES), lambda bi, h, qi: (bi, 0, HA + h)),
                  pl.BlockSpec((1, s, LANES), lambda bi, h, qi: (bi, 0, 2 * HA + h)),
                  pl.BlockSpec((1, 2, t, t), lambda bi, h, qi: (h, 0, 0, 0))],
        out_specs=pl.BlockSpec((1, t, LANES), lambda bi, h, qi: (bi, qi, h)),
        out_shape=jax.ShapeDtypeStruct((b, s, HA * LANES), BF16),
        scratch_shapes=[pltpu.VMEM((s // t, LANES, t), BF16),
                        pltpu.VMEM((1, 2 * t), F32), pltpu.VMEM((1, 2 * t), F32),
                        pltpu.VMEM((LANES, 2 * t), F32)],
        compiler_params=_cparams(("parallel", "parallel", "arbitrary")),
        name="attn_a_prompt",
    )(*lams, g_col, p3, p3, p3, da)


def _attn_b_kernel(q_ref, k_ref, v_ref, d_ref, o_ref, vt_ref, *, t):
    qi = pl.program_id(2)

    @pl.when(qi == 0)
    def _():
        _transpose_blocks(v_ref, vt_ref, t)

    q2 = _stack_halves(q_ref[0])

    def run(first_blk, nblk):
        start = pl.multiple_of(first_blk * t, t)
        rows = slice((3 - nblk) * t, 3 * t)
        s = _qk(k_ref[0, pl.ds(start, nblk * t), :], q2)
        s = s + jnp.concatenate([d_ref[0, rows, :], d_ref[1, rows, :]], axis=1)
        p = jnp.exp2(s - jnp.max(s, axis=0, keepdims=True))
        l = jnp.sum(p, axis=0, keepdims=True)
        p = p.astype(BF16)
        o = None
        for j in range(nblk):
            pv = jnp.dot(vt_ref[first_blk + j], p[j * t:(j + 1) * t, :], preferred_element_type=F32)
            o = pv if o is None else o + pv
        o = o * (1.0 / l)
        first_head = lax.broadcasted_iota(jnp.int32, (LANES, t), 0) < DB
        o_ref[0] = jnp.where(first_head, o[:, :t], o[:, t:]).T.astype(o_ref.dtype)

    @pl.when(qi == 0)
    def _():
        run(0, 1)

    @pl.when(qi == 1)
    def _():
        run(0, 2)

    @pl.when(qi >= 2)
    def _():
        run(qi - 2, 3)


def _attn_b_prompt(p3, db):
    b, s, n = p3.shape
    t = ATT_T
    hp = HB // 2
    assert s % t == 0 and s // t >= 2
    return pl.pallas_call(
        functools.partial(_attn_b_kernel, t=t),
        grid=(b, hp, s // t),
        in_specs=[pl.BlockSpec((1, t, LANES), lambda bi, h, qi: (bi, qi, 3 * HA + h)),
                  pl.BlockSpec((1, s, LANES), lambda bi, h, qi: (bi, 0, 3 * HA + hp + h)),
                  pl.BlockSpec((1, s, LANES), lambda bi, h, qi: (bi, 0, 3 * HA + 2 * hp + h)),
                  pl.BlockSpec((2, 3 * t, t), lambda bi, h, qi: (h, 0, 0))],
        out_specs=pl.BlockSpec((1, t, LANES), lambda bi, h, qi: (bi, qi, h)),
        out_shape=jax.ShapeDtypeStruct((b, s, hp * LANES), BF16),
        scratch_shapes=[pltpu.VMEM((s // t, LANES, t), BF16)],
        compiler_params=_cparams(("parallel", "parallel", "arbitrary")),
        name="attn_b_prompt",
    )(p3, p3, p3, db)


def _attn_a_sample_kernel(lq1_ref, lk1_ref, lq2_ref, lk2_ref, g_ref, p_ref, kc_ref, vc_ref, d_ref,
                          o_ref, *, t, lam_init):
    tq = p_ref.shape[1]
    past = kc_ref.shape[2]
    lam = _lam(lq1_ref[...], lk1_ref[...], lq2_ref[...], lk2_ref[...], lam_init)
    for h in range(HA):
        head = lambda sec: p_ref[0, :, (sec * HA + h) * LANES:(sec * HA + h + 1) * LANES]
        q2 = _stack_halves(head(0))
        kn, vn = head(1), head(2)
        kct = kc_ref[0, h * LANES:(h + 1) * LANES, :].astype(BF16)
        vc = vc_ref[0, pl.ds(h, past, stride=HA), :].astype(BF16)
        near = d_ref[h, 1].T[:tq]
        new = d_ref[h, 0, :LANES, :].T[:tq, :tq]
        scores = [jnp.dot(q2, kct[:, :past - t], preferred_element_type=F32),
                  jnp.dot(q2, kct[:, past - t:], preferred_element_type=F32)
                  + jnp.concatenate([near, near], axis=0),
                  _qk(q2, kn) + jnp.concatenate([new, new], axis=0)]
        o = _softmax_pv(scores, [vc[:past - t], vc[past - t:], vn])
        o = o[:tq] - lam * o[tq:]
        o = o * lax.rsqrt(jnp.mean(o * o, axis=1, keepdims=True) + RMS_EPS)
        o_ref[0, :, h * LANES:(h + 1) * LANES] = (o * g_ref[...] * (1.0 - lam_init)).astype(o_ref.dtype)


def _attn_a_sample(p3, cache_kt, cache_v, da, lams, g_row, lam_init):
    b, tq, n = p3.shape
    d, past = cache_kt.shape[1:]
    t = ATT_TA
    assert tq <= CHUNK and past % CHUNK == 0 and past > t and tq % 8 == 0
    vec = pl.BlockSpec((1, DA), lambda bi: (0, 0))
    return pl.pallas_call(
        functools.partial(_attn_a_sample_kernel, t=t, lam_init=lam_init),
        grid=(b,),
        in_specs=[vec, vec, vec, vec,
                  pl.BlockSpec((1, LANES), lambda bi: (0, 0)),
                  pl.BlockSpec((1, tq, n), lambda bi: (bi, 0, 0)),
                  pl.BlockSpec((1, d, past), lambda bi: (bi, 0, 0)),
                  pl.BlockSpec((1, past * HA, LANES), lambda bi: (bi, 0, 0)),
                  pl.BlockSpec((HA, 2, t, LANES), lambda bi: (0, 0, 0, 0),
                               pipeline_mode=pl.Buffered(1))],
        out_specs=pl.BlockSpec((1, tq, d), lambda bi: (bi, 0, 0)),
        out_shape=jax.ShapeDtypeStruct((b, tq, d), BF16),
        compiler_params=_cparams(("parallel",)),
        name="attn_a_sample",
    )(*lams, g_row, p3, cache_kt, cache_v, da)


def _attn_b_sample_kernel(q_ref, kn_ref, vn_ref, kc_ref, vc_ref, d_ref, o_ref, *, t):
    tq = q_ref.shape[1]
    past = kc_ref.shape[2]
    q2 = _stack_halves(q_ref[0])
    kct = kc_ref[0].astype(BF16)
    vct = vc_ref[0].astype(BF16)
    near = jnp.concatenate([d_ref[hh, t:2 * t, :].T[:tq] for hh in range(2)], axis=0)
    new = jnp.concatenate([d_ref[hh, 2 * t:2 * t + LANES, :].T[:tq, :tq] for hh in range(2)], axis=0)
    scores = [jnp.dot(q2, kct[:, past - t:], preferred_element_type=F32) + near,
              _qk(q2, kn_ref[0]) + new]
    values = [vct[:, past - t:], vn_ref[0]]
    values_t = [True, False]
    if past > t:
        scores.insert(0, jnp.dot(q2, kct[:, :past - t], preferred_element_type=F32))
        values.insert(0, vct[:, :past - t])
        values_t.insert(0, True)
    o = _softmax_pv(scores, values, values_t)
    olo, _ = _half_masks((tq, LANES))
    o_ref[0] = jnp.where(olo, o[:tq], o[tq:]).astype(o_ref.dtype)


def _attn_b_sample(p3, cache_kt, cache_vt, db, past_len):
    b, tq, n = p3.shape
    pb = cache_kt.shape[2]
    t = ATT_T
    hp = HB // 2
    assert tq <= CHUNK and past_len % CHUNK == 0 and pb == LEFT_CHUNKS * CHUNK and pb >= t
    return pl.pallas_call(
        functools.partial(_attn_b_sample_kernel, t=t),
        grid=(b, hp),
        in_specs=[pl.BlockSpec((1, tq, LANES), lambda bi, h: (bi, 0, 3 * HA + h)),
                  pl.BlockSpec((1, tq, LANES), lambda bi, h: (bi, 0, 3 * HA + hp + h)),
                  pl.BlockSpec((1, tq, LANES), lambda bi, h: (bi, 0, 3 * HA + 2 * hp + h)),
                  pl.BlockSpec((1, LANES, pb), lambda bi, h: (bi, h, 0)),
                  pl.BlockSpec((1, LANES, pb), lambda bi, h: (bi, h, 0)),
                  pl.BlockSpec((2, 3 * t, LANES), lambda bi, h: (h, 0, 0))],
        out_specs=pl.BlockSpec((1, tq, LANES), lambda bi, h: (bi, 0, h)),
        out_shape=jax.ShapeDtypeStruct((b, tq, hp * LANES), BF16),
        compiler_params=_cparams(("parallel", "parallel")),
        name="attn_b_sample",
    )(p3, p3, p3, cache_kt, cache_vt, db)


def _layer_norm(x, g, b):
    mu = jnp.mean(x, axis=1, keepdims=True)
    xc = x - mu
    var = jnp.mean(xc * xc, axis=1, keepdims=True)
    return xc * lax.rsqrt(var + LN_EPS) * g + b


def _tail_kernel(x_ref, oa_ref, ob_ref, ga_ref, gb_ref, prev_ref,
                 wpa_ref, wpb_ref, wout_ref, ln1g_ref, ln1b_ref,
                 wup_ref, wgate_ref, cw_ref, cb_ref, wdown_ref, ln2g_ref, ln2b_ref,
                 y_ref, conv_ref, u_ref, *, tm, alpha):
    ti = pl.program_id(1)
    pad = 8
    hist = CONV_W - 1

    @pl.when(ti == 0)
    def _():
        u_ref[pad - hist:pad, :] = prev_ref[0]

    ya = jnp.dot(oa_ref[0], wpa_ref[...], preferred_element_type=F32)
    yb = jnp.dot(ob_ref[0], wpb_ref[...], preferred_element_type=F32)
    merged = (jax.nn.sigmoid(ga_ref[0].astype(F32)) * ya
              + jax.nn.sigmoid(gb_ref[0].astype(F32)) * yb)
    mixed = jnp.dot(merged.astype(BF16), wout_ref[...], preferred_element_type=F32)
    h = _layer_norm(alpha * x_ref[0] + mixed, ln1g_ref[...], ln1b_ref[...])
    hb = h.astype(BF16)
    u = jnp.dot(hb, wup_ref[...], preferred_element_type=F32)
    g = jnp.dot(hb, wgate_ref[...], preferred_element_type=F32)
    u_ref[pad:pad + tm, :] = u
    uc = cb_ref[...] + u_ref[pad - 2:pad - 2 + tm, :] * cw_ref[0:1, :]
    uc = uc + u_ref[pad - 1:pad - 1 + tm, :] * cw_ref[1:2, :]
    uc = uc + u * cw_ref[2:3, :]
    last = u_ref[pad + tm - hist:pad + tm, :]
    conv_ref[0] = last
    u_ref[pad - hist:pad, :] = last
    f = jnp.dot((jax.nn.gelu(uc) * g).astype(BF16), wdown_ref[...], preferred_element_type=F32)
    y_ref[0] = _layer_norm(alpha * h + f, ln2g_ref[...], ln2b_ref[...])


def _tail(x3, oa, ob, p3, conv_prev, w, tm, alpha):
    b, s, d = x3.shape
    dff = w["w_up"].shape[1]
    assert s % tm == 0 and tm >= CONV_W - 1
    const = lambda shape: pl.BlockSpec(shape, lambda bi, ti: (0,) * len(shape),
                                       pipeline_mode=pl.Buffered(1))
    act = lambda col: pl.BlockSpec((1, tm, d), lambda bi, ti: (bi, ti, col))
    return pl.pallas_call(
        functools.partial(_tail_kernel, tm=tm, alpha=alpha),
        grid=(b, s // tm),
        in_specs=[act(0), act(0), act(0), act(6), act(7),
                  pl.BlockSpec((1, CONV_W - 1, dff), lambda bi, ti: (bi, 0, 0)),
                  const((d, d)), const((d, d)), const((d, d)), const((1, d)), const((1, d)),
                  const((d, dff)), const((d, dff)), const((CONV_W, dff)), const((1, dff)),
                  const((dff, d)), const((1, d)), const((1, d))],
        out_specs=[pl.BlockSpec((1, tm, d), lambda bi, ti: (bi, ti, 0)),
                   pl.BlockSpec((1, CONV_W - 1, dff), lambda bi, ti: (bi, 0, 0))],
        out_shape=[jax.ShapeDtypeStruct((b, s, d), F32),
                   jax.ShapeDtypeStruct((b, CONV_W - 1, dff), F32)],
        scratch_shapes=[pltpu.VMEM((tm + 8, dff), F32)],
        compiler_params=_cparams(("parallel", "arbitrary")),
        name="tail",
    )(x3, oa, ob, p3, p3, conv_prev,
      w["w_pa"], w["w_pb"], w["w_out"], w["ln1_g"], w["ln1_b"],
      w["w_up"], w["w_gate"], w["conv_w"], w["conv_b"], w["w_down"], w["ln2_g"], w["ln2_b"])


def kernel(x_prompt, x_sample, cache_a_k, cache_a_v, cache_b_k, cache_b_v, cache_conv, t5_table, w_in, lambda_q1, lambda_k1, lambda_q2, lambda_k2, subln_g, rel_table_b, w_pa, w_pb, w_out, ln1_g, ln1_b, w_up, w_gate, conv_w, conv_b, w_down, ln2_g, ln2_b):
    depth = w_in.shape[0]
    alpha = (2.0 * depth) ** 0.25
    bp, s, d = x_prompt.shape
    bs, t_new, _ = x_sample.shape
    past = cache_a_k.shape[2]
    pb = cache_b_k.shape[2]
    dff = w_up.shape[2]
    keep = min(LEFT_CHUNKS * CHUNK, s)
    assert pb == min(LEFT_CHUNKS * CHUNK, past)

    xp, xs = x_prompt, x_sample
    outs_p = [[] for _ in range(5)]
    outs_s = [[] for _ in range(5)]
    for l in range(depth):
        lam_init = 0.8 - 0.6 * math.exp(-0.3 * l)
        lams = [v[l].astype(F32).reshape(1, DA) for v in (lambda_q1, lambda_k1, lambda_q2, lambda_k2)]
        g_row = subln_g[l].astype(F32).reshape(1, 2 * DA)
        g_col = subln_g[l].astype(F32).reshape(2 * DA, 1)
        w = {
            "w_pa": w_pa[l].astype(BF16), "w_pb": w_pb[l].astype(BF16), "w_out": w_out[l].astype(BF16),
            "ln1_g": ln1_g[l].reshape(1, d), "ln1_b": ln1_b[l].reshape(1, d),
            "w_up": w_up[l].astype(BF16), "w_gate": w_gate[l].astype(BF16),
            "conv_w": conv_w[l], "conv_b": conv_b[l].reshape(1, dff),
            "w_down": w_down[l].astype(BF16),
            "ln2_g": ln2_g[l].reshape(1, d), "ln2_b": ln2_b[l].reshape(1, d),
        }
        w_in_b = w_in[l].astype(BF16)
        da, db = _bias_tiles(t5_table, rel_table_b[l])

        p, kat, va, kbt, vbt = _project_prompt(xp, w_in_b, keep, ATT_T)
        p3 = p.reshape(bp, s, 8 * d)
        oa = _attn_a_prompt(p3, da, lams, g_col, lam_init)
        ob = _attn_b_prompt(p3, db)
        conv0 = jnp.zeros((bp, CONV_W - 1, dff), F32)
        xp, conv_p = _tail(xp, oa, ob, p3, conv0, w, ATT_T, alpha)
        outs_p[0].append(kat.reshape(bp, HA, 2, DA, s).transpose(0, 4, 1, 2, 3))
        outs_p[1].append(va.reshape(bp, s, HA, 2 * DA))
        outs_p[2].append(kbt.reshape(bp, HB, DB, keep).transpose(0, 3, 1, 2))
        outs_p[3].append(vbt.reshape(bp, HB, DB, keep).transpose(0, 3, 1, 2))
        outs_p[4].append(conv_p)

        p, ka, va, kb, vb = _project(xs.reshape(bs * t_new, d), w_in_b, t_new, t_new, t_new)
        p3 = p.reshape(bs, t_new, 8 * d)
        cak_t = cache_a_k[l].transpose(0, 2, 3, 4, 1).reshape(bs, d, past)
        cbk_t = cache_b_k[l].transpose(0, 2, 3, 1).reshape(bs, d, pb)
        cbv_t = cache_b_v[l].transpose(0, 2, 3, 1).reshape(bs, d, pb)
        oa = _attn_a_sample(p3, cak_t, cache_a_v[l].reshape(bs, past * HA, 2 * DA),
                            da, lams, g_row, lam_init)
        ob = _attn_b_sample(p3, cbk_t, cbv_t, db, past)
        xs, conv_s = _tail(xs, oa, ob, p3, cache_conv[l].astype(F32), w, t_new, alpha)
        outs_s[0].append(ka.reshape(bs, t_new, HA, 2, DA))
        outs_s[1].append(va.reshape(bs, t_new, HA, 2 * DA))
        outs_s[2].append(kb.reshape(bs, t_new, HB, DB))
        outs_s[3].append(vb.reshape(bs, t_new, HB, DB))
        outs_s[4].append(conv_s)

    return (xp, xs, *[jnp.stack(o) for o in outs_p], *[jnp.stack(o) for o in outs_s])
```

```python
import functools
import math

import jax
import jax.numpy as jnp
from jax import lax
from jax.experimental import pallas as pl
from jax.experimental.pallas import tpu as pltpu

F32 = jnp.float32
BF16 = jnp.bfloat16

CHUNK = 64
HA = 8
DA = 64
HB = 16
DB = 64
LEFT_CHUNKS = 8
MAX_REL = 128
T5_BUCKETS = 32
T5_MAX_DIST = 128
CONV_W = 3
LN_EPS = 1e-5
RMS_EPS = 1e-5
NEG = -1e30
LOG2E = math.log2(math.e)

LANES = 128
ATT_T = 256
ATT_TA = 512
VMEM_LIMIT = 56 * 1024 * 1024


def _cparams(sem, vmem=VMEM_LIMIT):
    return pltpu.CompilerParams(dimension_semantics=sem, vmem_limit_bytes=vmem)


def _toeplitz(vec, t):
    x = jnp.broadcast_to(vec, (t, 2 * t))
    row = lax.broadcasted_iota(jnp.int32, (t, 2 * t), 0)
    shift = 1
    while shift < t:
        x = jnp.where((row & shift) != 0, pltpu.roll(x, shift, 1), x)
        shift *= 2
    return x[:, t:]


def _t5_bucket(rel):
    nb = T5_BUCKETS // 2
    max_exact = nb // 2
    ret = jnp.where(rel > 0, nb, 0)
    n = jnp.abs(rel)
    nf = jnp.maximum(n, 1).astype(F32)
    large = max_exact + (jnp.log(nf / max_exact) / math.log(T5_MAX_DIST / max_exact)
                         * (nb - max_exact)).astype(jnp.int32)
    large = jnp.minimum(large, nb - 1)
    return ret + jnp.where(n < max_exact, n, large)


def _chunk_ids(t):
    kc = lax.broadcasted_iota(jnp.int32, (t, t), 0) // CHUNK
    qc = lax.broadcasted_iota(jnp.int32, (t, t), 1) // CHUNK
    return kc, qc


def _bias_a_kernel(tab_ref, out_ref, *, t):
    h = pl.program_id(0)
    r = lax.broadcasted_iota(jnp.int32, (1, 2 * t), 1)
    far = tab_ref[h * T5_BUCKETS + T5_BUCKETS // 2 - 1]

    def band(rel):
        bucket = _t5_bucket(rel)
        acc = jnp.zeros(rel.shape, F32)
        for j in range(T5_BUCKETS):
            acc = jnp.where(bucket == j, tab_ref[h * T5_BUCKETS + j], acc)
        return (acc - far) * LOG2E

    kc, qc = _chunk_ids(t)
    out_ref[0, 0] = jnp.where(kc <= qc, _toeplitz(band(t - r), t), NEG)
    out_ref[0, 1] = _toeplitz(band(-r), t)


def _bias_b_kernel(tab_ref, out_ref, *, t):
    h = pl.program_id(0)
    nrel = 2 * MAX_REL + 1
    r = lax.broadcasted_iota(jnp.int32, (1, 2 * t), 1)
    far = tab_ref[h * nrel]

    def band(rel):
        idx = jnp.clip(rel, -MAX_REL, MAX_REL) + MAX_REL

        def body(j, acc):
            return jnp.where(idx == j, tab_ref[h * nrel + j], acc)

        acc = lax.fori_loop(0, nrel, body, jnp.zeros(rel.shape, F32))
        return (acc - far) * LOG2E

    kc, qc = _chunk_ids(t)
    out_ref[0, 0:t, :] = jnp.where(kc - 2 * (t // CHUNK) >= qc - LEFT_CHUNKS, 0.0, NEG)
    out_ref[0, t:2 * t, :] = _toeplitz(band(-r), t)
    out_ref[0, 2 * t:3 * t, :] = jnp.where(kc <= qc, _toeplitz(band(t - r), t), NEG)


def _bias_tiles(t5_table, rel_table):
    ta, tb = ATT_TA, ATT_T
    assert ta >= T5_MAX_DIST and ta % CHUNK == 0
    assert tb >= MAX_REL and 2 * tb == LEFT_CHUNKS * CHUNK
    smem = pl.BlockSpec(memory_space=pltpu.SMEM)
    da = pl.pallas_call(
        functools.partial(_bias_a_kernel, t=ta),
        grid=(HA,),
        in_specs=[smem],
        out_specs=pl.BlockSpec((1, 2, ta, ta), lambda h: (h, 0, 0, 0)),
        out_shape=jax.ShapeDtypeStruct((HA, 2, ta, ta), F32),
        compiler_params=_cparams(("parallel",)),
        name="bias_a",
    )(t5_table.astype(F32).T.reshape(-1))
    db = pl.pallas_call(
        functools.partial(_bias_b_kernel, t=tb),
        grid=(HB,),
        in_specs=[smem],
        out_specs=pl.BlockSpec((1, 3 * tb, tb), lambda h: (h, 0, 0)),
        out_shape=jax.ShapeDtypeStruct((HB, 3 * tb, tb), F32),
        compiler_params=_cparams(("parallel",)),
        name="bias_b",
    )(rel_table.astype(F32).T.reshape(-1))
    return da, db


def _proj_kernel(x_ref, w_ref, p_ref, ka_ref, va_ref, kb_ref, vb_ref, *, d, q_scale):
    xb = x_ref[...].astype(BF16)
    f32_outs = {1: ka_ref, 2: va_ref, 4: kb_ref, 5: vb_ref}
    for c in range(8):
        acc = jnp.dot(xb, w_ref[:, c * d:(c + 1) * d], preferred_element_type=F32)
        if c in f32_outs:
            f32_outs[c][...] = acc
        if c in (0, 3):
            acc = acc * q_scale
        p_ref[:, c * d:(c + 1) * d] = acc.astype(BF16)


def _proj_cache_kernel(x_ref, w_ref, p_ref, kat_ref, va_ref, kbt_ref, vbt_ref, *,
                       d, q_scale, tm, first_keep, tiles_per_seq):
    t = pl.program_id(0) % tiles_per_seq
    xb = x_ref[...].astype(BF16)
    for c in range(8):
        acc = jnp.dot(xb, w_ref[:, c * d:(c + 1) * d], preferred_element_type=F32)
        if c == 1:
            kat_ref[0] = acc.T
        elif c == 2:
            for h in range(HA):
                va_ref[pl.ds(h, tm, stride=HA), :] = acc[:, h * LANES:(h + 1) * LANES]
        elif c in (4, 5):
            out = kbt_ref if c == 4 else vbt_ref

            @pl.when(t >= first_keep)
            def _(acc=acc, out=out):
                out[0] = acc.T
        if c in (0, 3):
            acc = acc * q_scale
        p_ref[:, c * d:(c + 1) * d] = acc.astype(BF16)


def _project_prompt(x3, w_bf16, keep, tm):
    b, s, d = x3.shape
    n = w_bf16.shape[1]
    assert n == 8 * d and d == HA * 2 * DA == HB * DB == HA * LANES
    assert s % tm == 0 and keep % tm == 0 and tm % LANES == 0
    tiles_per_seq = s // tm
    first_keep = tiles_per_seq - keep // tm
    m = b * s
    row = lambda i: (i, 0)
    col = lambda i: (i // tiles_per_seq, 0, i % tiles_per_seq)
    keep_col = lambda i: (i // tiles_per_seq, 0, jnp.maximum(i % tiles_per_seq - first_keep, 0))
    return pl.pallas_call(
        functools.partial(_proj_cache_kernel, d=d, q_scale=DA ** -0.5 * LOG2E, tm=tm,
                          first_keep=first_keep, tiles_per_seq=tiles_per_seq),
        grid=(m // tm,),
        in_specs=[pl.BlockSpec((tm, d), row),
                  pl.BlockSpec((d, n), lambda i: (0, 0), pipeline_mode=pl.Buffered(1))],
        out_specs=[pl.BlockSpec((tm, n), row),
                   pl.BlockSpec((1, d, tm), col), pl.BlockSpec((tm * HA, LANES), row),
                   pl.BlockSpec((1, d, tm), keep_col), pl.BlockSpec((1, d, tm), keep_col)],
        out_shape=[jax.ShapeDtypeStruct((m, n), BF16),
                   jax.ShapeDtypeStruct((b, d, s), F32), jax.ShapeDtypeStruct((m * HA, LANES), F32),
                   jax.ShapeDtypeStruct((b, d, keep), F32), jax.ShapeDtypeStruct((b, d, keep), F32)],
        compiler_params=_cparams(("arbitrary",)),
        name="in_proj_prompt",
    )(x3.reshape(m, d), w_bf16)


def _project(x2d, w_bf16, rows_per_seq, keep, tm):
    m, d = x2d.shape
    n = w_bf16.shape[1]
    assert n == 8 * d and d == HA * 2 * DA == HB * DB
    assert m % tm == 0 and rows_per_seq % tm == 0 and keep % tm == 0
    tiles_per_seq = rows_per_seq // tm
    keep_tiles = keep // tm
    first_keep = tiles_per_seq - keep_tiles

    def keep_map(i):
        return ((i // tiles_per_seq) * keep_tiles
                + jnp.maximum(i % tiles_per_seq - first_keep, 0), 0)

    row = lambda i: (i, 0)
    n_keep = (m // rows_per_seq) * keep
    return pl.pallas_call(
        functools.partial(_proj_kernel, d=d, q_scale=DA ** -0.5 * LOG2E),
        grid=(m // tm,),
        in_specs=[pl.BlockSpec((tm, d), row),
                  pl.BlockSpec((d, n), lambda i: (0, 0), pipeline_mode=pl.Buffered(1))],
        out_specs=[pl.BlockSpec((tm, n), row),
                   pl.BlockSpec((tm, d), row), pl.BlockSpec((tm, d), row),
                   pl.BlockSpec((tm, d), keep_map), pl.BlockSpec((tm, d), keep_map)],
        out_shape=[jax.ShapeDtypeStruct((m, n), BF16),
                   jax.ShapeDtypeStruct((m, d), F32), jax.ShapeDtypeStruct((m, d), F32),
                   jax.ShapeDtypeStruct((n_keep, d), F32), jax.ShapeDtypeStruct((n_keep, d), F32)],
        compiler_params=_cparams(("arbitrary",)),
        name="in_proj",
    )(x2d, w_bf16)


def _half_masks(shape):
    lane = lax.broadcasted_iota(jnp.int32, shape, len(shape) - 1)
    return lane < (LANES // 2), lane >= (LANES // 2)


def _stack_halves(q):
    lo, hi = _half_masks(q.shape)
    zero = jnp.zeros_like(q)
    return jnp.concatenate([jnp.where(lo, q, zero), jnp.where(hi, q, zero)], axis=0)


def _qk(a, b):
    return lax.dot_general(a, b, (((1,), (1,)), ((), ())), preferred_element_type=F32)


def _transpose_blocks(v_ref, vt_ref, t):
    for j in range(vt_ref.shape[0]):
        vt_ref[j] = v_ref[0, j * t:(j + 1) * t, :].astype(F32).T.astype(vt_ref.dtype)


def _softmax_pv(scores, values, values_t=None):
    values_t = values_t or (False,) * len(values)
    m = functools.reduce(jnp.maximum, [jnp.max(s, axis=1, keepdims=True) for s in scores])
    l = None
    o = None
    for s, v, vt in zip(scores, values, values_t):
        p = jnp.exp2(s - m)
        ls = jnp.sum(p, axis=1, keepdims=True)
        pb = p.astype(BF16)
        ov = _qk(pb, v) if vt else jnp.dot(pb, v, preferred_element_type=F32)
        l = ls if l is None else l + ls
        o = ov if o is None else o + ov
    return o / l


def _lam(lq1, lk1, lq2, lk2, lam_init):
    e1 = jnp.exp(jnp.sum(lq1 * lk1, axis=1, keepdims=True))
    e2 = jnp.exp(jnp.sum(lq2 * lk2, axis=1, keepdims=True))
    return e1 - e2 + lam_init


def _attn_a_kernel(lq1_ref, lk1_ref, lq2_ref, lk2_ref, g_ref, q_ref, k_ref, v_ref, d_ref, o_ref,
                   vt_ref, s_ref, mc_ref, m_ref, l_ref, acc_ref, *, t, lam_init):
    qi = pl.program_id(2)

    @pl.when(qi == 0)
    def _():
        _transpose_blocks(v_ref, vt_ref, t)

    q2 = _stack_halves(q_ref[0])
    m_ref[...] = jnp.full(m_ref.shape, NEG, F32)
    l_ref[...] = jnp.zeros(l_ref.shape, F32)
    acc_ref[...] = jnp.zeros(acc_ref.shape, F32)

    def scores(kb, bias):
        start = pl.multiple_of(kb * t, t)
        s = _qk(k_ref[0, pl.ds(start, t), :], q2)
        if bias is not None:
            s = s + jnp.concatenate([bias, bias], axis=1)
        s_ref[...] = s
        mc_ref[...] = jnp.max(s, axis=0, keepdims=True)

    def accumulate(kb, kb_next=None, bias_next=None, refill=False):
        s = s_ref[...]
        m_prev = m_ref[...]
        m_new = jnp.maximum(m_prev, mc_ref[...])
        alpha = jnp.exp2(m_prev - m_new)
        p = jnp.exp2(s - m_new)
        if refill:
            scores(kb_next, bias_next)
        l_ref[...] = alpha * l_ref[...] + jnp.sum(p, axis=0, keepdims=True)
        pv = jnp.dot(vt_ref[kb], p.astype(BF16), preferred_element_type=F32)
        acc_ref[...] = alpha * acc_ref[...] + pv
        m_ref[...] = m_new

    n_far = jnp.maximum(qi - 1, 0)
    scores(qi, d_ref[0, 0])

    @pl.when(qi >= 1)
    def _():
        accumulate(qi, qi - 1, d_ref[0, 1], refill=True)

    def far(i, carry):
        accumulate(jnp.where(i == 0, qi - 1, i - 1), i, None, refill=True)
        return carry

    lax.fori_loop(0, n_far, far, 0)
    accumulate(jnp.where(qi == 0, 0, jnp.where(n_far == 0, qi - 1, n_far - 1)))

    lam = _lam(lq1_ref[...], lk1_ref[...], lq2_ref[...], lk2_ref[...], lam_init)
    o = acc_ref[...] * (1.0 / l_ref[...])
    o = o[:, :t] - lam * o[:, t:]
    o = o * lax.rsqrt(jnp.mean(o * o, axis=0, keepdims=True) + RMS_EPS)
    o = o * (g_ref[...] * (1.0 - lam_init))
    o_ref[0] = o.T.astype(o_ref.dtype)


def _attn_a_prompt(p3, da, lams, g_col, lam_init):
    b, s, n = p3.shape
    t = ATT_TA
    assert s % t == 0 and n == 8 * HA * LANES
    vec = pl.BlockSpec((1, DA), lambda bi, h, qi: (0, 0))
    return pl.pallas_call(
        functools.partial(_attn_a_kernel, t=t, lam_init=lam_init),
        grid=(b, HA, s // t),
        in_specs=[vec, vec, vec, vec,
                  pl.BlockSpec((LANES, 1), lambda bi, h, qi: (0, 0)),
                  pl.BlockSpec((1, t, LANES), lambda bi, h, qi: (bi, qi, h)),
                  pl.BlockSpec((1, s, LANES), lambda bi, h, qi: (bi, 0, HA + h)),
                  pl.BlockSpec((1, s, LANES), lambda bi, h, qi: (bi, 0, 2 * HA + h)),
                  pl.BlockSpec((1, 2, t, t), lambda bi, h, qi: (h, 0, 0, 0))],
        out_specs=pl.BlockSpec((1, t, LANES), lambda bi, h, qi: (bi, qi, h)),
        out_shape=jax.ShapeDtypeStruct((b, s, HA * LANES), BF16),
        scratch_shapes=[pltpu.VMEM((s // t, LANES, t), BF16),
                        pltpu.VMEM((t, 2 * t), F32), pltpu.VMEM((1, 2 * t), F32),
                        pltpu.VMEM((1, 2 * t), F32), pltpu.VMEM((1, 2 * t), F32),
                        pltpu.VMEM((LANES, 2 * t), F32)],
        compiler_params=_cparams(("parallel", "parallel", "arbitrary")),
        name="attn_a_prompt",
    )(*lams, g_col, p3, p3, p3, da)


def _attn_b_kernel(q_ref, k_ref, v_ref, d_ref, o_ref, vt_ref, *, t):
    qi = pl.program_id(2)

    @pl.when(qi == 0)
    def _():
        _transpose_blocks(v_ref, vt_ref, t)

    q2 = _stack_halves(q_ref[0])

    def run(first_blk, nblk):
        start = pl.multiple_of(first_blk * t, t)
        rows = slice((3 - nblk) * t, 3 * t)
        s = _qk(k_ref[0, pl.ds(start, nblk * t), :], q2)
        s = s + jnp.concatenate([d_ref[0, rows, :], d_ref[1, rows, :]], axis=1)
        p = jnp.exp2(s - jnp.max(s, axis=0, keepdims=True))
        l = jnp.sum(p, axis=0, keepdims=True)
        p = p.astype(BF16)
        o = None
        for j in range(nblk):
            pv = jnp.dot(vt_ref[first_blk + j], p[j * t:(j + 1) * t, :], preferred_element_type=F32)
            o = pv if o is None else o + pv
        o = o * (1.0 / l)
        first_head = lax.broadcasted_iota(jnp.int32, (LANES, t), 0) < DB
        o_ref[0] = jnp.where(first_head, o[:, :t], o[:, t:]).T.astype(o_ref.dtype)

    @pl.when(qi == 0)
    def _():
        run(0, 1)

    @pl.when(qi == 1)
    def _():
        run(0, 2)

    @pl.when(qi >= 2)
    def _():
        run(qi - 2, 3)


def _attn_b_prompt(p3, db):
    b, s, n = p3.shape
    t = ATT_T
    hp = HB // 2
    assert s % t == 0 and s // t >= 2
    return pl.pallas_call(
        functools.partial(_attn_b_kernel, t=t),
        grid=(b, hp, s // t),
        in_specs=[pl.BlockSpec((1, t, LANES), lambda bi, h, qi: (bi, qi, 3 * HA + h)),
                  pl.BlockSpec((1, s, LANES), lambda bi, h, qi: (bi, 0, 3 * HA + hp + h)),
                  pl.BlockSpec((1, s, LANES), lambda bi, h, qi: (bi, 0, 3 * HA + 2 * hp + h)),
                  pl.BlockSpec((2, 3 * t, t), lambda bi, h, qi: (h, 0, 0))],
        out_specs=pl.BlockSpec((1, t, LANES), lambda bi, h, qi: (bi, qi, h)),
        out_shape=jax.ShapeDtypeStruct((b, s, hp * LANES), BF16),
        scratch_shapes=[pltpu.VMEM((s // t, LANES, t), BF16)],
        compiler_params=_cparams(("parallel", "parallel", "arbitrary")),
        name="attn_b_prompt",
    )(p3, p3, p3, db)


def _attn_a_sample_kernel(lq1_ref, lk1_ref, lq2_ref, lk2_ref, g_ref, p_ref, kc_ref, vc_ref, d_ref,
                          o_ref, *, t, lam_init):
    tq = p_ref.shape[1]
    past = kc_ref.shape[2]
    lam = _lam(lq1_ref[...], lk1_ref[...], lq2_ref[...], lk2_ref[...], lam_init)
    for h in range(HA):
        head = lambda sec: p_ref[0, :, (sec * HA + h) * LANES:(sec * HA + h + 1) * LANES]
        q2 = _stack_halves(head(0))
        kn, vn = head(1), head(2)
        kct = kc_ref[0, h * LANES:(h + 1) * LANES, :].astype(BF16)
        vc = vc_ref[0, pl.ds(h, past, stride=HA), :].astype(BF16)
        near = d_ref[h, 1].T[:tq]
        new = d_ref[h, 0, :LANES, :].T[:tq, :tq]
        scores = [jnp.dot(q2, kct[:, :past - t], preferred_element_type=F32),
                  jnp.dot(q2, kct[:, past - t:], preferred_element_type=F32)
                  + jnp.concatenate([near, near], axis=0),
                  _qk(q2, kn) + jnp.concatenate([new, new], axis=0)]
        o = _softmax_pv(scores, [vc[:past - t], vc[past - t:], vn])
        o = o[:tq] - lam * o[tq:]
        o = o * lax.rsqrt(jnp.mean(o * o, axis=1, keepdims=True) + RMS_EPS)
        o_ref[0, :, h * LANES:(h + 1) * LANES] = (o * g_ref[...] * (1.0 - lam_init)).astype(o_ref.dtype)


def _attn_a_sample(p3, cache_kt, cache_v, da, lams, g_row, lam_init):
    b, tq, n = p3.shape
    d, past = cache_kt.shape[1:]
    t = ATT_TA
    assert tq <= CHUNK and past % CHUNK == 0 and past > t and tq % 8 == 0
    vec = pl.BlockSpec((1, DA), lambda bi: (0, 0))
    return pl.pallas_call(
        functools.partial(_attn_a_sample_kernel, t=t, lam_init=lam_init),
        grid=(b,),
        in_specs=[vec, vec, vec, vec,
                  pl.BlockSpec((1, LANES), lambda bi: (0, 0)),
                  pl.BlockSpec((1, tq, n), lambda bi: (bi, 0, 0)),
                  pl.BlockSpec((1, d, past), lambda bi: (bi, 0, 0)),
                  pl.BlockSpec((1, past * HA, LANES), lambda bi: (bi, 0, 0)),
                  pl.BlockSpec((HA, 2, t, LANES), lambda bi: (0, 0, 0, 0),
                               pipeline_mode=pl.Buffered(1))],
        out_specs=pl.BlockSpec((1, tq, d), lambda bi: (bi, 0, 0)),
        out_shape=jax.ShapeDtypeStruct((b, tq, d), BF16),
        compiler_params=_cparams(("parallel",)),
        name="attn_a_sample",
    )(*lams, g_row, p3, cache_kt, cache_v, da)


def _attn_b_sample_kernel(q_ref, kn_ref, vn_ref, kc_ref, vc_ref, d_ref, o_ref, *, t):
    tq = q_ref.shape[1]
    past = kc_ref.shape[2]
    q2 = _stack_halves(q_ref[0])
    kct = kc_ref[0].astype(BF16)
    vct = vc_ref[0].astype(BF16)
    near = jnp.concatenate([d_ref[hh, t:2 * t, :].T[:tq] for hh in range(2)], axis=0)
    new = jnp.concatenate([d_ref[hh, 2 * t:2 * t + LANES, :].T[:tq, :tq] for hh in range(2)], axis=0)
    scores = [jnp.dot(q2, kct[:, past - t:], preferred_element_type=F32) + near,
              _qk(q2, kn_ref[0]) + new]
    values = [vct[:, past - t:], vn_ref[0]]
    values_t = [True, False]
    if past > t:
        scores.insert(0, jnp.dot(q2, kct[:, :past - t], preferred_element_type=F32))
        values.insert(0, vct[:, :past - t])
        values_t.insert(0, True)
    o = _softmax_pv(scores, values, values_t)
    olo, _ = _half_masks((tq, LANES))
    o_ref[0] = jnp.where(olo, o[:tq], o[tq:]).astype(o_ref.dtype)


def _attn_b_sample(p3, cache_kt, cache_vt, db, past_len):
    b, tq, n = p3.shape
    pb = cache_kt.shape[2]
    t = ATT_T
    hp = HB // 2
    assert tq <= CHUNK and past_len % CHUNK == 0 and pb == LEFT_CHUNKS * CHUNK and pb >= t
    return pl.pallas_call(
        functools.partial(_attn_b_sample_kernel, t=t),
        grid=(b, hp),
        in_specs=[pl.BlockSpec((1, tq, LANES), lambda bi, h: (bi, 0, 3 * HA + h)),
                  pl.BlockSpec((1, tq, LANES), lambda bi, h: (bi, 0, 3 * HA + hp + h)),
                  pl.BlockSpec((1, tq, LANES), lambda bi, h: (bi, 0, 3 * HA + 2 * hp + h)),
                  pl.BlockSpec((1, LANES, pb), lambda bi, h: (bi, h, 0)),
                  pl.BlockSpec((1, LANES, pb), lambda bi, h: (bi, h, 0)),
                  pl.BlockSpec((2, 3 * t, LANES), lambda bi, h: (h, 0, 0))],
        out_specs=pl.BlockSpec((1, tq, LANES), lambda bi, h: (bi, 0, h)),
        out_shape=jax.ShapeDtypeStruct((b, tq, hp * LANES), BF16),
        compiler_params=_cparams(("parallel", "parallel")),
        name="attn_b_sample",
    )(p3, p3, p3, cache_kt, cache_vt, db)


def _layer_norm(x, g, b):
    mu = jnp.mean(x, axis=1, keepdims=True)
    xc = x - mu
    var = jnp.mean(xc * xc, axis=1, keepdims=True)
    return xc * lax.rsqrt(var + LN_EPS) * g + b


def _tail_kernel(x_ref, oa_ref, ob_ref, ga_ref, gb_ref, prev_ref,
                 wpa_ref, wpb_ref, wout_ref, ln1g_ref, ln1b_ref,
                 wup_ref, wgate_ref, cw_ref, cb_ref, wdown_ref, ln2g_ref, ln2b_ref,
                 y_ref, conv_ref, u_ref, *, tm, alpha):
    ti = pl.program_id(1)
    pad = 8
    hist = CONV_W - 1

    @pl.when(ti == 0)
    def _():
        u_ref[pad - hist:pad, :] = prev_ref[0]

    ya = jnp.dot(oa_ref[0], wpa_ref[...], preferred_element_type=F32)
    yb = jnp.dot(ob_ref[0], wpb_ref[...], preferred_element_type=F32)
    merged = (jax.nn.sigmoid(ga_ref[0].astype(F32)) * ya
              + jax.nn.sigmoid(gb_ref[0].astype(F32)) * yb)
    mixed = jnp.dot(merged.astype(BF16), wout_ref[...], preferred_element_type=F32)
    h = _layer_norm(alpha * x_ref[0] + mixed, ln1g_ref[...], ln1b_ref[...])
    hb = h.astype(BF16)
    u = jnp.dot(hb, wup_ref[...], preferred_element_type=F32)
    g = jnp.dot(hb, wgate_ref[...], preferred_element_type=F32)
    u_ref[pad:pad + tm, :] = u
    uc = cb_ref[...] + u_ref[pad - 2:pad - 2 + tm, :] * cw_ref[0:1, :]
    uc = uc + u_ref[pad - 1:pad - 1 + tm, :] * cw_ref[1:2, :]
    uc = uc + u * cw_ref[2:3, :]
    last = u_ref[pad + tm - hist:pad + tm, :]
    conv_ref[0] = last
    u_ref[pad - hist:pad, :] = last
    f = jnp.dot((jax.nn.gelu(uc) * g).astype(BF16), wdown_ref[...], preferred_element_type=F32)
    y_ref[0] = _layer_norm(alpha * h + f, ln2g_ref[...], ln2b_ref[...])


def _tail(x3, oa, ob, p3, conv_prev, w, tm, alpha):
    b, s, d = x3.shape
    dff = w["w_up"].shape[1]
    assert s % tm == 0 and tm >= CONV_W - 1
    const = lambda shape: pl.BlockSpec(shape, lambda bi, ti: (0,) * len(shape),
                                       pipeline_mode=pl.Buffered(1))
    act = lambda col: pl.BlockSpec((1, tm, d), lambda bi, ti: (bi, ti, col))
    return pl.pallas_call(
        functools.partial(_tail_kernel, tm=tm, alpha=alpha),
        grid=(b, s // tm),
        in_specs=[act(0), act(0), act(0), act(6), act(7),
                  pl.BlockSpec((1, CONV_W - 1, dff), lambda bi, ti: (bi, 0, 0)),
                  const((d, d)), const((d, d)), const((d, d)), const((1, d)), const((1, d)),
                  const((d, dff)), const((d, dff)), const((CONV_W, dff)), const((1, dff)),
                  const((dff, d)), const((1, d)), const((1, d))],
        out_specs=[pl.BlockSpec((1, tm, d), lambda bi, ti: (bi, ti, 0)),
                   pl.BlockSpec((1, CONV_W - 1, dff), lambda bi, ti: (bi, 0, 0))],
        out_shape=[jax.ShapeDtypeStruct((b, s, d), F32),
                   jax.ShapeDtypeStruct((b, CONV_W - 1, dff), F32)],
        scratch_shapes=[pltpu.VMEM((tm + 8, dff), F32)],
        compiler_params=_cparams(("parallel", "arbitrary")),
        name="tail",
    )(x3, oa, ob, p3, p3, conv_prev,
      w["w_pa"], w["w_pb"], w["w_out"], w["ln1_g"], w["ln1_b"],
      w["w_up"], w["w_gate"], w["conv_w"], w["conv_b"], w["w_down"], w["ln2_g"], w["ln2_b"])


def kernel(x_prompt, x_sample, cache_a_k, cache_a_v, cache_b_k, cache_b_v, cache_conv, t5_table, w_in, lambda_q1, lambda_k1, lambda_q2, lambda_k2, subln_g, rel_table_b, w_pa, w_pb, w_out, ln1_g, ln1_b, w_up, w_gate, conv_w, conv_b, w_down, ln2_g, ln2_b):
    depth = w_in.shape[0]
    alpha = (2.0 * depth) ** 0.25
    bp, s, d = x_prompt.shape
    bs, t_new, _ = x_sample.shape
    past = cache_a_k.shape[2]
    pb = cache_b_k.shape[2]
    dff = w_up.shape[2]
    keep = min(LEFT_CHUNKS * CHUNK, s)
    assert pb == min(LEFT_CHUNKS * CHUNK, past)

    xp, xs = x_prompt, x_sample
    outs_p = [[] for _ in range(5)]
    outs_s = [[] for _ in range(5)]
    for l in range(depth):
        lam_init = 0.8 - 0.6 * math.exp(-0.3 * l)
        lams = [v[l].astype(F32).reshape(1, DA) for v in (lambda_q1, lambda_k1, lambda_q2, lambda_k2)]
        g_row = subln_g[l].astype(F32).reshape(1, 2 * DA)
        g_col = subln_g[l].astype(F32).reshape(2 * DA, 1)
        w = {
            "w_pa": w_pa[l].astype(BF16), "w_pb": w_pb[l].astype(BF16), "w_out": w_out[l].astype(BF16),
            "ln1_g": ln1_g[l].reshape(1, d), "ln1_b": ln1_b[l].reshape(1, d),
            "w_up": w_up[l].astype(BF16), "w_gate": w_gate[l].astype(BF16),
            "conv_w": conv_w[l], "conv_b": conv_b[l].reshape(1, dff),
            "w_down": w_down[l].astype(BF16),
            "ln2_g": ln2_g[l].reshape(1, d), "ln2_b": ln2_b[l].reshape(1, d),
        }
        w_in_b = w_in[l].astype(BF16)
        da, db = _bias_tiles(t5_table, rel_table_b[l])

        p, kat, va, kbt, vbt = _project_prompt(xp, w_in_b, keep, ATT_T)
        p3 = p.reshape(bp, s, 8 * d)
        oa = _attn_a_prompt(p3, da, lams, g_col, lam_init)
        ob = _attn_b_prompt(p3, db)
        conv0 = jnp.zeros((bp, CONV_W - 1, dff), F32)
        xp, conv_p = _tail(xp, oa, ob, p3, conv0, w, ATT_T, alpha)
        outs_p[0].append(kat.reshape(bp, HA, 2, DA, s).transpose(0, 4, 1, 2, 3))
        outs_p[1].append(va.reshape(bp, s, HA, 2 * DA))
        outs_p[2].append(kbt.reshape(bp, HB, DB, keep).transpose(0, 3, 1, 2))
        outs_p[3].append(vbt.reshape(bp, HB, DB, keep).transpose(0, 3, 1, 2))
        outs_p[4].append(conv_p)

        p, ka, va, kb, vb = _project(xs.reshape(bs * t_new, d), w_in_b, t_new, t_new, t_new)
        p3 = p.reshape(bs, t_new, 8 * d)
        cak_t = cache_a_k[l].transpose(0, 2, 3, 4, 1).reshape(bs, d, past)
        cbk_t = cache_b_k[l].transpose(0, 2, 3, 1).reshape(bs, d, pb)
        cbv_t = cache_b_v[l].transpose(0, 2, 3, 1).reshape(bs, d, pb)
        oa = _attn_a_sample(p3, cak_t, cache_a_v[l].reshape(bs, past * HA, 2 * DA),
                            da, lams, g_row, lam_init)
        ob = _attn_b_sample(p3, cbk_t, cbv_t, db, past)
        xs, conv_s = _tail(xs, oa, ob, p3, cache_conv[l].astype(F32), w, t_new, alpha)
        outs_s[0].append(ka.reshape(bs, t_new, HA, 2, DA))
        outs_s[1].append(va.reshape(bs, t_new, HA, 2 * DA))
        outs_s[2].append(kb.reshape(bs, t_new, HB, DB))
        outs_s[3].append(vb.reshape(bs, t_new, HB, DB))
        outs_s[4].append(conv_s)

    return (xp, xs, *[jnp.stack(o) for o in outs_p], *[jnp.stack(o) for o in outs_s])
```

```python
import functools
import math

import jax
import jax.numpy as jnp
from jax import lax
from jax.experimental import pallas as pl
from jax.experimental.pallas import tpu as pltpu

F32 = jnp.float32
BF16 = jnp.bfloat16

CHUNK = 64
HA = 8
DA = 64
HB = 16
DB = 64
LEFT_CHUNKS = 8
MAX_REL = 128
T5_BUCKETS = 32
T5_MAX_DIST = 128
CONV_W = 3
LN_EPS = 1e-5
RMS_EPS = 1e-5
NEG = -1e30
LOG2E = math.log2(math.e)

LANES = 128
ATT_T = 256
ATT_TA = 512
ATT_B_HEADS = 4
VMEM_LIMIT = 56 * 1024 * 1024


def _cparams(sem, vmem=VMEM_LIMIT):
    return pltpu.CompilerParams(dimension_semantics=sem, vmem_limit_bytes=vmem)


def _toeplitz(vec, t):
    x = jnp.broadcast_to(vec, (t, 2 * t))
    row = lax.broadcasted_iota(jnp.int32, (t, 2 * t), 0)
    shift = 1
    while shift < t:
        x = jnp.where((row & shift) != 0, pltpu.roll(x, shift, 1), x)
        shift *= 2
    return x[:, t:]


def _t5_bucket(rel):
    nb = T5_BUCKETS // 2
    max_exact = nb // 2
    ret = jnp.where(rel > 0, nb, 0)
    n = jnp.abs(rel)
    nf = jnp.maximum(n, 1).astype(F32)
    large = max_exact + (jnp.log(nf / max_exact) / math.log(T5_MAX_DIST / max_exact)
                         * (nb - max_exact)).astype(jnp.int32)
    large = jnp.minimum(large, nb - 1)
    return ret + jnp.where(n < max_exact, n, large)


def _chunk_ids(t):
    kc = lax.broadcasted_iota(jnp.int32, (t, t), 0) // CHUNK
    qc = lax.broadcasted_iota(jnp.int32, (t, t), 1) // CHUNK
    return kc, qc


def _bias_a_kernel(tab_ref, out_ref, *, t):
    h = pl.program_id(0)
    r = lax.broadcasted_iota(jnp.int32, (1, 2 * t), 1)
    far = tab_ref[h * T5_BUCKETS + T5_BUCKETS // 2 - 1]

    def band(rel):
        bucket = _t5_bucket(rel)
        acc = jnp.zeros(rel.shape, F32)
        for j in range(T5_BUCKETS):
            acc = jnp.where(bucket == j, tab_ref[h * T5_BUCKETS + j], acc)
        return (acc - far) * LOG2E

    kc, qc = _chunk_ids(t)
    out_ref[0, 0] = jnp.where(kc <= qc, _toeplitz(band(t - r), t), NEG)
    out_ref[0, 1] = _toeplitz(band(-r), t)


def _bias_b_kernel(tab_ref, out_ref, *, t):
    h = pl.program_id(0)
    nrel = 2 * MAX_REL + 1
    r = lax.broadcasted_iota(jnp.int32, (1, 2 * t), 1)
    far = tab_ref[h * nrel]

    def band(rel):
        idx = jnp.clip(rel, -MAX_REL, MAX_REL) + MAX_REL

        def body(j, acc):
            return jnp.where(idx == j, tab_ref[h * nrel + j], acc)

        acc = lax.fori_loop(0, nrel, body, jnp.zeros(rel.shape, F32))
        return (acc - far) * LOG2E

    kc, qc = _chunk_ids(t)
    out_ref[0, 0:t, :] = jnp.where(kc - 2 * (t // CHUNK) >= qc - LEFT_CHUNKS, 0.0, NEG)
    out_ref[0, t:2 * t, :] = _toeplitz(band(-r), t)
    out_ref[0, 2 * t:3 * t, :] = jnp.where(kc <= qc, _toeplitz(band(t - r), t), NEG)
    out_ref[0, 3 * t:4 * t, :] = jnp.full((t, t), NEG, F32)


def _bias_tiles(t5_table, rel_table):
    ta, tb = ATT_TA, ATT_T
    assert ta >= T5_MAX_DIST and ta % CHUNK == 0
    assert tb >= MAX_REL and 2 * tb == LEFT_CHUNKS * CHUNK
    smem = pl.BlockSpec(memory_space=pltpu.SMEM)
    da = pl.pallas_call(
        functools.partial(_bias_a_kernel, t=ta),
        grid=(HA,),
        in_specs=[smem],
        out_specs=pl.BlockSpec((1, 2, ta, ta), lambda h: (h, 0, 0, 0)),
        out_shape=jax.ShapeDtypeStruct((HA, 2, ta, ta), F32),
        compiler_params=_cparams(("parallel",)),
        name="bias_a",
    )(t5_table.astype(F32).T.reshape(-1))
    db = pl.pallas_call(
        functools.partial(_bias_b_kernel, t=tb),
        grid=(HB,),
        in_specs=[smem],
        out_specs=pl.BlockSpec((1, 4 * tb, tb), lambda h: (h, 0, 0)),
        out_shape=jax.ShapeDtypeStruct((HB, 4 * tb, tb), F32),
        compiler_params=_cparams(("parallel",)),
        name="bias_b",
    )(rel_table.astype(F32).T.reshape(-1))
    return da, db


def _proj_kernel(x_ref, w_ref, p_ref, ka_ref, va_ref, kb_ref, vb_ref, *, d, q_scale):
    xb = x_ref[...].astype(BF16)
    f32_outs = {1: ka_ref, 2: va_ref, 4: kb_ref, 5: vb_ref}
    for c in range(8):
        acc = jnp.dot(xb, w_ref[:, c * d:(c + 1) * d], preferred_element_type=F32)
        if c in f32_outs:
            f32_outs[c][...] = acc
        if c in (0, 3):
            acc = acc * q_scale
        p_ref[:, c * d:(c + 1) * d] = acc.astype(BF16)


def _proj_cache_kernel(x_ref, w_ref, p_ref, kat_ref, va_ref, kbt_ref, vbt_ref, *,
                       d, q_scale, tm, first_keep, tiles_per_seq):
    t = pl.program_id(0) % tiles_per_seq
    xb = x_ref[...].astype(BF16)
    for c in range(8):
        acc = jnp.dot(xb, w_ref[:, c * d:(c + 1) * d], preferred_element_type=F32)
        if c == 1:
            kat_ref[0] = acc.T
        elif c == 2:
            for h in range(HA):
                va_ref[pl.ds(h, tm, stride=HA), :] = acc[:, h * LANES:(h + 1) * LANES]
        elif c in (4, 5):
            out = kbt_ref if c == 4 else vbt_ref

            @pl.when(t >= first_keep)
            def _(acc=acc, out=out):
                out[0] = acc.T
        if c in (0, 3):
            acc = acc * q_scale
        p_ref[:, c * d:(c + 1) * d] = acc.astype(BF16)


def _project_prompt(x3, w_bf16, keep, tm):
    b, s, d = x3.shape
    n = w_bf16.shape[1]
    assert n == 8 * d and d == HA * 2 * DA == HB * DB == HA * LANES
    assert s % tm == 0 and keep % tm == 0 and tm % LANES == 0
    tiles_per_seq = s // tm
    first_keep = tiles_per_seq - keep // tm
    m = b * s
    row = lambda i: (i, 0)
    col = lambda i: (i // tiles_per_seq, 0, i % tiles_per_seq)
    keep_col = lambda i: (i // tiles_per_seq, 0, jnp.maximum(i % tiles_per_seq - first_keep, 0))
    return pl.pallas_call(
        functools.partial(_proj_cache_kernel, d=d, q_scale=DA ** -0.5 * LOG2E, tm=tm,
                          first_keep=first_keep, tiles_per_seq=tiles_per_seq),
        grid=(m // tm,),
        in_specs=[pl.BlockSpec((tm, d), row),
                  pl.BlockSpec((d, n), lambda i: (0, 0), pipeline_mode=pl.Buffered(1))],
        out_specs=[pl.BlockSpec((tm, n), row),
                   pl.BlockSpec((1, d, tm), col), pl.BlockSpec((tm * HA, LANES), row),
                   pl.BlockSpec((1, d, tm), keep_col), pl.BlockSpec((1, d, tm), keep_col)],
        out_shape=[jax.ShapeDtypeStruct((m, n), BF16),
                   jax.ShapeDtypeStruct((b, d, s), F32), jax.ShapeDtypeStruct((m * HA, LANES), F32),
                   jax.ShapeDtypeStruct((b, d, keep), F32), jax.ShapeDtypeStruct((b, d, keep), F32)],
        compiler_params=_cparams(("arbitrary",)),
        name="in_proj_prompt",
    )(x3.reshape(m, d), w_bf16)


def _project(x2d, w_bf16, rows_per_seq, keep, tm):
    m, d = x2d.shape
    n = w_bf16.shape[1]
    assert n == 8 * d and d == HA * 2 * DA == HB * DB
    assert m % tm == 0 and rows_per_seq % tm == 0 and keep % tm == 0
    tiles_per_seq = rows_per_seq // tm
    keep_tiles = keep // tm
    first_keep = tiles_per_seq - keep_tiles

    def keep_map(i):
        return ((i // tiles_per_seq) * keep_tiles
                + jnp.maximum(i % tiles_per_seq - first_keep, 0), 0)

    row = lambda i: (i, 0)
    n_keep = (m // rows_per_seq) * keep
    return pl.pallas_call(
        functools.partial(_proj_kernel, d=d, q_scale=DA ** -0.5 * LOG2E),
        grid=(m // tm,),
        in_specs=[pl.BlockSpec((tm, d), row),
                  pl.BlockSpec((d, n), lambda i: (0, 0), pipeline_mode=pl.Buffered(1))],
        out_specs=[pl.BlockSpec((tm, n), row),
                   pl.BlockSpec((tm, d), row), pl.BlockSpec((tm, d), row),
                   pl.BlockSpec((tm, d), keep_map), pl.BlockSpec((tm, d), keep_map)],
        out_shape=[jax.ShapeDtypeStruct((m, n), BF16),
                   jax.ShapeDtypeStruct((m, d), F32), jax.ShapeDtypeStruct((m, d), F32),
                   jax.ShapeDtypeStruct((n_keep, d), F32), jax.ShapeDtypeStruct((n_keep, d), F32)],
        compiler_params=_cparams(("arbitrary",)),
        name="in_proj",
    )(x2d, w_bf16)


def _half_masks(shape):
    lane = lax.broadcasted_iota(jnp.int32, shape, len(shape) - 1)
    return lane < (LANES // 2), lane >= (LANES // 2)


def _stack_halves(q):
    lo, hi = _half_masks(q.shape)
    zero = jnp.zeros_like(q)
    return jnp.concatenate([jnp.where(lo, q, zero), jnp.where(hi, q, zero)], axis=0)


def _qk(a, b):
    return lax.dot_general(a, b, (((1,), (1,)), ((), ())), preferred_element_type=F32)


def _transpose_blocks(v_ref, vt_ref, t):
    for j in range(vt_ref.shape[0]):
        vt_ref[j] = v_ref[0, j * t:(j + 1) * t, :].astype(F32).T.astype(vt_ref.dtype)


def _softmax_pv(scores, values, values_t=None):
    values_t = values_t or (False,) * len(values)
    m = functools.reduce(jnp.maximum, [jnp.max(s, axis=1, keepdims=True) for s in scores])
    l = None
    o = None
    for s, v, vt in zip(scores, values, values_t):
        p = jnp.exp2(s - m)
        ls = jnp.sum(p, axis=1, keepdims=True)
        pb = p.astype(BF16)
        ov = _qk(pb, v) if vt else jnp.dot(pb, v, preferred_element_type=F32)
        l = ls if l is None else l + ls
        o = ov if o is None else o + ov
    return o / l


def _lam(lq1, lk1, lq2, lk2, lam_init):
    e1 = jnp.exp(jnp.sum(lq1 * lk1, axis=1, keepdims=True))
    e2 = jnp.exp(jnp.sum(lq2 * lk2, axis=1, keepdims=True))
    return e1 - e2 + lam_init


def _attn_a_kernel(lq1_ref, lk1_ref, lq2_ref, lk2_ref, g_ref, q_ref, k_ref, v_ref, d_ref, o_ref,
                   vt_ref, s_ref, mc_ref, m_ref, l_ref, acc_ref, *, t, lam_init):
    qi = pl.program_id(2)

    @pl.when(qi == 0)
    def _():
        _transpose_blocks(v_ref, vt_ref, t)

    q2 = _stack_halves(q_ref[0])
    m_ref[...] = jnp.full(m_ref.shape, NEG, F32)
    l_ref[...] = jnp.zeros(l_ref.shape, F32)
    acc_ref[...] = jnp.zeros(acc_ref.shape, F32)

    def scores(kb, bias):
        start = pl.multiple_of(kb * t, t)
        s = _qk(k_ref[0, pl.ds(start, t), :], q2)
        if bias is not None:
            s = s + jnp.concatenate([bias, bias], axis=1)
        s_ref[...] = s
        mc_ref[...] = jnp.max(s, axis=0, keepdims=True)

    def accumulate(kb, kb_next=None, bias_next=None, refill=False):
        s = s_ref[...]
        m_prev = m_ref[...]
        m_new = jnp.maximum(m_prev, mc_ref[...])
        alpha = jnp.exp2(m_prev - m_new)
        p = jnp.exp2(s - m_new)
        if refill:
            scores(kb_next, bias_next)
        l_ref[...] = alpha * l_ref[...] + jnp.sum(p, axis=0, keepdims=True)
        pv = jnp.dot(vt_ref[kb], p.astype(BF16), preferred_element_type=F32)
        acc_ref[...] = alpha * acc_ref[...] + pv
        m_ref[...] = m_new

    n_far = jnp.maximum(qi - 1, 0)
    scores(qi, d_ref[0, 0])

    @pl.when(qi >= 1)
    def _():
        accumulate(qi, qi - 1, d_ref[0, 1], refill=True)

    def far(i, carry):
        accumulate(jnp.where(i == 0, qi - 1, i - 1), i, None, refill=True)
        return carry

    lax.fori_loop(0, n_far, far, 0)
    accumulate(jnp.where(qi == 0, 0, jnp.where(n_far == 0, qi - 1, n_far - 1)))

    lam = _lam(lq1_ref[...], lk1_ref[...], lq2_ref[...], lk2_ref[...], lam_init)
    o = acc_ref[...] * (1.0 / l_ref[...])
    o = o[:, :t] - lam * o[:, t:]
    o = o * lax.rsqrt(jnp.mean(o * o, axis=0, keepdims=True) + RMS_EPS)
    o = o * (g_ref[...] * (1.0 - lam_init))
    o_ref[0] = o.T.astype(o_ref.dtype)


def _attn_a_prompt(p3, da, lams, g_col, lam_init):
    b, s, n = p3.shape
    t = ATT_TA
    assert s % t == 0 and n == 8 * HA * LANES
    vec = pl.BlockSpec((1, DA), lambda bi, h, qi: (0, 0))
    return pl.pallas_call(
        functools.partial(_attn_a_kernel, t=t, lam_init=lam_init),
        grid=(b, HA, s // t),
        in_specs=[vec, vec, vec, vec,
                  pl.BlockSpec((LANES, 1), lambda bi, h, qi: (0, 0)),
                  pl.BlockSpec((1, t, LANES), lambda bi, h, qi: (bi, qi, h)),
                  pl.BlockSpec((1, s, LANES), lambda bi, h, qi: (bi, 0, HA + h)),
                  pl.BlockSpec((1, s, LANES), lambda bi, h, qi: (bi, 0, 2 * HA + h)),
                  pl.BlockSpec((1, 2, t, t), lambda bi, h, qi: (h, 0, 0, 0))],
        out_specs=pl.BlockSpec((1, t, LANES), lambda bi, h, qi: (bi, qi, h)),
        out_shape=jax.ShapeDtypeStruct((b, s, HA * LANES), BF16),
        scratch_shapes=[pltpu.VMEM((s // t, LANES, t), BF16),
                        pltpu.VMEM((t, 2 * t), F32), pltpu.VMEM((1, 2 * t), F32),
                        pltpu.VMEM((1, 2 * t), F32), pltpu.VMEM((1, 2 * t), F32),
                        pltpu.VMEM((LANES, 2 * t), F32)],
        compiler_params=_cparams(("parallel", "parallel", "arbitrary")),
        name="attn_a_prompt",
    )(*lams, g_col, p3, p3, p3, da)


def _attn_b_kernel(q_ref, k_ref, v_ref, d_ref, o_ref, vt_ref, s_ref, mc_ref, *, t, heads):
    qi = pl.program_id(2)

    @pl.when(qi == 0)
    def _():
        _transpose_blocks(v_ref, vt_ref, t)

    blocks = [jnp.maximum(qi - 2 + j, 0) for j in range(3)]
    bias_rows = [pl.multiple_of(jnp.where(qi - 2 + j >= 0, j, 3) * t, t) for j in range(3)]

    def scores(g):
        pair = slice((g // 2) * LANES, (g // 2 + 1) * LANES)
        q = q_ref[0, :, pair]
        q = jnp.where(_half_masks(q.shape)[g % 2], q, jnp.zeros_like(q))
        k = jnp.concatenate([k_ref[0, pl.ds(pl.multiple_of(kb * t, t), t), pair] for kb in blocks],
                            axis=0)
        bias = jnp.concatenate([d_ref[g, pl.ds(r, t), :] for r in bias_rows], axis=0)
        s = _qk(k, q) + bias
        s_ref[...] = s
        mc_ref[...] = jnp.max(s, axis=0, keepdims=True)

    def finish(g, refill):
        p = jnp.exp2(s_ref[...] - mc_ref[...])
        if refill:
            scores(g + 1)
        l = jnp.sum(p, axis=0, keepdims=True)
        p = p.astype(BF16)
        o = None
        for j, kb in enumerate(blocks):
            vt = vt_ref[kb, g * DB:(g + 1) * DB, :]
            pv = jnp.dot(vt, p[j * t:(j + 1) * t, :], preferred_element_type=F32)
            o = pv if o is None else o + pv
        return o * (1.0 / l)

    scores(0)
    outs = [finish(g, g + 1 < heads) for g in range(heads)]
    o_ref[0] = jnp.concatenate(outs, axis=0).T.astype(o_ref.dtype)


def _attn_b_prompt(p3, db):
    b, s, n = p3.shape
    t = ATT_T
    g = ATT_B_HEADS
    w = g * DB
    assert s % t == 0 and HB % g == 0 and w % LANES == 0
    col0 = 3 * HA * LANES // w
    nblk = HB // g
    return pl.pallas_call(
        functools.partial(_attn_b_kernel, t=t, heads=g),
        grid=(b, nblk, s // t),
        in_specs=[pl.BlockSpec((1, t, w), lambda bi, h, qi: (bi, qi, col0 + h)),
                  pl.BlockSpec((1, s, w), lambda bi, h, qi: (bi, 0, col0 + nblk + h)),
                  pl.BlockSpec((1, s, w), lambda bi, h, qi: (bi, 0, col0 + 2 * nblk + h)),
                  pl.BlockSpec((g, 4 * t, t), lambda bi, h, qi: (h, 0, 0))],
        out_specs=pl.BlockSpec((1, t, w), lambda bi, h, qi: (bi, qi, h)),
        out_shape=jax.ShapeDtypeStruct((b, s, HB * DB), BF16),
        scratch_shapes=[pltpu.VMEM((s // t, w, t), BF16),
                        pltpu.VMEM((3 * t, t), F32), pltpu.VMEM((1, t), F32)],
        compiler_params=_cparams(("parallel", "parallel", "arbitrary")),
        name="attn_b_prompt",
    )(p3, p3, p3, db)


def _attn_a_sample_kernel(lq1_ref, lk1_ref, lq2_ref, lk2_ref, g_ref, p_ref, kc_ref, vc_ref, d_ref,
                          o_ref, *, t, lam_init):
    tq = p_ref.shape[1]
    past = kc_ref.shape[2]
    lam = _lam(lq1_ref[...], lk1_ref[...], lq2_ref[...], lk2_ref[...], lam_init)
    for h in range(HA):
        head = lambda sec: p_ref[0, :, (sec * HA + h) * LANES:(sec * HA + h + 1) * LANES]
        q2 = _stack_halves(head(0))
        kn, vn = head(1), head(2)
        kct = kc_ref[0, h * LANES:(h + 1) * LANES, :].astype(BF16)
        vc = vc_ref[0, pl.ds(h, past, stride=HA), :].astype(BF16)
        near = d_ref[h, 1].T[:tq]
        new = d_ref[h, 0, :LANES, :].T[:tq, :tq]
        scores = [jnp.dot(q2, kct[:, :past - t], preferred_element_type=F32),
                  jnp.dot(q2, kct[:, past - t:], preferred_element_type=F32)
                  + jnp.concatenate([near, near], axis=0),
                  _qk(q2, kn) + jnp.concatenate([new, new], axis=0)]
        o = _softmax_pv(scores, [vc[:past - t], vc[past - t:], vn])
        o = o[:tq] - lam * o[tq:]
        o = o * lax.rsqrt(jnp.mean(o * o, axis=1, keepdims=True) + RMS_EPS)
        o_ref[0, :, h * LANES:(h + 1) * LANES] = (o * g_ref[...] * (1.0 - lam_init)).astype(o_ref.dtype)


def _attn_a_sample(p3, cache_kt, cache_v, da, lams, g_row, lam_init):
    b, tq, n = p3.shape
    d, past = cache_kt.shape[1:]
    t = ATT_TA
    assert tq <= CHUNK and past % CHUNK == 0 and past > t and tq % 8 == 0
    vec = pl.BlockSpec((1, DA), lambda bi: (0, 0))
    return pl.pallas_call(
        functools.partial(_attn_a_sample_kernel, t=t, lam_init=lam_init),
        grid=(b,),
        in_specs=[vec, vec, vec, vec,
                  pl.BlockSpec((1, LANES), lambda bi: (0, 0)),
                  pl.BlockSpec((1, tq, n), lambda bi: (bi, 0, 0)),
                  pl.BlockSpec((1, d, past), lambda bi: (bi, 0, 0)),
                  pl.BlockSpec((1, past * HA, LANES), lambda bi: (bi, 0, 0)),
                  pl.BlockSpec((HA, 2, t, LANES), lambda bi: (0, 0, 0, 0),
                               pipeline_mode=pl.Buffered(1))],
        out_specs=pl.BlockSpec((1, tq, d), lambda bi: (bi, 0, 0)),
        out_shape=jax.ShapeDtypeStruct((b, tq, d), BF16),
        compiler_params=_cparams(("parallel",)),
        name="attn_a_sample",
    )(*lams, g_row, p3, cache_kt, cache_v, da)


def _attn_b_sample_kernel(q_ref, kn_ref, vn_ref, kc_ref, vc_ref, d_ref, o_ref, *, t):
    tq = q_ref.shape[1]
    past = kc_ref.shape[2]
    q2 = _stack_halves(q_ref[0])
    kct = kc_ref[0].astype(BF16)
    vct = vc_ref[0].astype(BF16)
    near = jnp.concatenate([d_ref[hh, t:2 * t, :].T[:tq] for hh in range(2)], axis=0)
    new = jnp.concatenate([d_ref[hh, 2 * t:2 * t + LANES, :].T[:tq, :tq] for hh in range(2)], axis=0)
    scores = [jnp.dot(q2, kct[:, past - t:], preferred_element_type=F32) + near,
              _qk(q2, kn_ref[0]) + new]
    values = [vct[:, past - t:], vn_ref[0]]
    values_t = [True, False]
    if past > t:
        scores.insert(0, jnp.dot(q2, kct[:, :past - t], preferred_element_type=F32))
        values.insert(0, vct[:, :past - t])
        values_t.insert(0, True)
    o = _softmax_pv(scores, values, values_t)
    olo, _ = _half_masks((tq, LANES))
    o_ref[0] = jnp.where(olo, o[:tq], o[tq:]).astype(o_ref.dtype)


def _attn_b_sample(p3, cache_kt, cache_vt, db, past_len):
    b, tq, n = p3.shape
    pb = cache_kt.shape[2]
    t = ATT_T
    hp = HB // 2
    assert tq <= CHUNK and past_len % CHUNK == 0 and pb == LEFT_CHUNKS * CHUNK and pb >= t
    return pl.pallas_call(
        functools.partial(_attn_b_sample_kernel, t=t),
        grid=(b, hp),
        in_specs=[pl.BlockSpec((1, tq, LANES), lambda bi, h: (bi, 0, 3 * HA + h)),
                  pl.BlockSpec((1, tq, LANES), lambda bi, h: (bi, 0, 3 * HA + hp + h)),
                  pl.BlockSpec((1, tq, LANES), lambda bi, h: (bi, 0, 3 * HA + 2 * hp + h)),
                  pl.BlockSpec((1, LANES, pb), lambda bi, h: (bi, h, 0)),
                  pl.BlockSpec((1, LANES, pb), lambda bi, h: (bi, h, 0)),
                  pl.BlockSpec((2, 4 * t, LANES), lambda bi, h: (h, 0, 0))],
        out_specs=pl.BlockSpec((1, tq, LANES), lambda bi, h: (bi, 0, h)),
        out_shape=jax.ShapeDtypeStruct((b, tq, hp * LANES), BF16),
        compiler_params=_cparams(("parallel", "parallel")),
        name="attn_b_sample",
    )(p3, p3, p3, cache_kt, cache_vt, db)


def _layer_norm(x, g, b):
    mu = jnp.mean(x, axis=1, keepdims=True)
    xc = x - mu
    var = jnp.mean(xc * xc, axis=1, keepdims=True)
    return xc * lax.rsqrt(var + LN_EPS) * g + b


def _tail_kernel(x_ref, oa_ref, ob_ref, ga_ref, gb_ref, prev_ref,
                 wpa_ref, wpb_ref, wout_ref, ln1g_ref, ln1b_ref,
                 wup_ref, wgate_ref, cw_ref, cb_ref, wdown_ref, ln2g_ref, ln2b_ref,
                 y_ref, conv_ref, u_ref, *, tm, alpha):
    ti = pl.program_id(1)
    pad = 8
    hist = CONV_W - 1

    @pl.when(ti == 0)
    def _():
        u_ref[pad - hist:pad, :] = prev_ref[0]

    ya = jnp.dot(oa_ref[0], wpa_ref[...], preferred_element_type=F32)
    yb = jnp.dot(ob_ref[0], wpb_ref[...], preferred_element_type=F32)
    merged = (jax.nn.sigmoid(ga_ref[0].astype(F32)) * ya
              + jax.nn.sigmoid(gb_ref[0].astype(F32)) * yb)
    mixed = jnp.dot(merged.astype(BF16), wout_ref[...], preferred_element_type=F32)
    h = _layer_norm(alpha * x_ref[0] + mixed, ln1g_ref[...], ln1b_ref[...])
    hb = h.astype(BF16)
    u = jnp.dot(hb, wup_ref[...], preferred_element_type=F32)
    g = jnp.dot(hb, wgate_ref[...], preferred_element_type=F32)
    u_ref[pad:pad + tm, :] = u
    uc = cb_ref[...] + u_ref[pad - 2:pad - 2 + tm, :] * cw_ref[0:1, :]
    uc = uc + u_ref[pad - 1:pad - 1 + tm, :] * cw_ref[1:2, :]
    uc = uc + u * cw_ref[2:3, :]
    last = u_ref[pad + tm - hist:pad + tm, :]
    conv_ref[0] = last
    u_ref[pad - hist:pad, :] = last
    f = jnp.dot((jax.nn.gelu(uc) * g).astype(BF16), wdown_ref[...], preferred_element_type=F32)
    y_ref[0] = _layer_norm(alpha * h + f, ln2g_ref[...], ln2b_ref[...])


def _tail(x3, oa, ob, p3, conv_prev, w, tm, alpha):
    b, s, d = x3.shape
    dff = w["w_up"].shape[1]
    assert s % tm == 0 and tm >= CONV_W - 1
    const = lambda shape: pl.BlockSpec(shape, lambda bi, ti: (0,) * len(shape),
                                       pipeline_mode=pl.Buffered(1))
    act = lambda col: pl.BlockSpec((1, tm, d), lambda bi, ti: (bi, ti, col))
    return pl.pallas_call(
        functools.partial(_tail_kernel, tm=tm, alpha=alpha),
        grid=(b, s // tm),
        in_specs=[act(0), act(0), act(0), act(6), act(7),
                  pl.BlockSpec((1, CONV_W - 1, dff), lambda bi, ti: (bi, 0, 0)),
                  const((d, d)), const((d, d)), const((d, d)), const((1, d)), const((1, d)),
                  const((d, dff)), const((d, dff)), const((CONV_W, dff)), const((1, dff)),
                  const((dff, d)), const((1, d)), const((1, d))],
        out_specs=[pl.BlockSpec((1, tm, d), lambda bi, ti: (bi, ti, 0)),
                   pl.BlockSpec((1, CONV_W - 1, dff), lambda bi, ti: (bi, 0, 0))],
        out_shape=[jax.ShapeDtypeStruct((b, s, d), F32),
                   jax.ShapeDtypeStruct((b, CONV_W - 1, dff), F32)],
        scratch_shapes=[pltpu.VMEM((tm + 8, dff), F32)],
        compiler_params=_cparams(("parallel", "arbitrary")),
        name="tail",
    )(x3, oa, ob, p3, p3, conv_prev,
      w["w_pa"], w["w_pb"], w["w_out"], w["ln1_g"], w["ln1_b"],
      w["w_up"], w["w_gate"], w["conv_w"], w["conv_b"], w["w_down"], w["ln2_g"], w["ln2_b"])


def kernel(x_prompt, x_sample, cache_a_k, cache_a_v, cache_b_k, cache_b_v, cache_conv, t5_table, w_in, lambda_q1, lambda_k1, lambda_q2, lambda_k2, subln_g, rel_table_b, w_pa, w_pb, w_out, ln1_g, ln1_b, w_up, w_gate, conv_w, conv_b, w_down, ln2_g, ln2_b):
    depth = w_in.shape[0]
    alpha = (2.0 * depth) ** 0.25
    bp, s, d = x_prompt.shape
    bs, t_new, _ = x_sample.shape
    past = cache_a_k.shape[2]
    pb = cache_b_k.shape[2]
    dff = w_up.shape[2]
    keep = min(LEFT_CHUNKS * CHUNK, s)
    assert pb == min(LEFT_CHUNKS * CHUNK, past)

    xp, xs = x_prompt, x_sample
    outs_p = [[] for _ in range(5)]
    outs_s = [[] for _ in range(5)]
    for l in range(depth):
        lam_init = 0.8 - 0.6 * math.exp(-0.3 * l)
        lams = [v[l].astype(F32).reshape(1, DA) for v in (lambda_q1, lambda_k1, lambda_q2, lambda_k2)]
        g_row = subln_g[l].astype(F32).reshape(1, 2 * DA)
        g_col = subln_g[l].astype(F32).reshape(2 * DA, 1)
        w = {
            "w_pa": w_pa[l].astype(BF16), "w_pb": w_pb[l].astype(BF16), "w_out": w_out[l].astype(BF16),
            "ln1_g": ln1_g[l].reshape(1, d), "ln1_b": ln1_b[l].reshape(1, d),
            "w_up": w_up[l].astype(BF16), "w_gate": w_gate[l].astype(BF16),
            "conv_w": conv_w[l], "conv_b": conv_b[l].reshape(1, dff),
            "w_down": w_down[l].astype(BF16),
            "ln2_g": ln2_g[l].reshape(1, d), "ln2_b": ln2_b[l].reshape(1, d),
        }
        w_in_b = w_in[l].astype(BF16)
        da, db = _bias_tiles(t5_table, rel_table_b[l])

        p, kat, va, kbt, vbt = _project_prompt(xp, w_in_b, keep, ATT_T)
        p3 = p.reshape(bp, s, 8 * d)
        oa = _attn_a_prompt(p3, da, lams, g_col, lam_init)
        ob = _attn_b_prompt(p3, db)
        conv0 = jnp.zeros((bp, CONV_W - 1, dff), F32)
        xp, conv_p = _tail(xp, oa, ob, p3, conv0, w, ATT_T, alpha)
        outs_p[0].append(kat.reshape(bp, HA, 2, DA, s).transpose(0, 4, 1, 2, 3))
        outs_p[1].append(va.reshape(bp, s, HA, 2 * DA))
        outs_p[2].append(kbt.reshape(bp, HB, DB, keep).transpose(0, 3, 1, 2))
        outs_p[3].append(vbt.reshape(bp, HB, DB, keep).transpose(0, 3, 1, 2))
        outs_p[4].append(conv_p)

        p, ka, va, kb, vb = _project(xs.reshape(bs * t_new, d), w_in_b, t_new, t_new, t_new)
        p3 = p.reshape(bs, t_new, 8 * d)
        cak_t = cache_a_k[l].transpose(0, 2, 3, 4, 1).reshape(bs, d, past)
        cbk_t = cache_b_k[l].transpose(0, 2, 3, 1).reshape(bs, d, pb)
        cbv_t = cache_b_v[l].transpose(0, 2, 3, 1).reshape(bs, d, pb)
        oa = _attn_a_sample(p3, cak_t, cache_a_v[l].reshape(bs, past * HA, 2 * DA),
                            da, lams, g_row, lam_init)
        ob = _attn_b_sample(p3, cbk_t, cbv_t, db, past)
        xs, conv_s = _tail(xs, oa, ob, p3, cache_conv[l].astype(F32), w, t_new, alpha)
        outs_s[0].append(ka.reshape(bs, t_new, HA, 2, DA))
        outs_s[1].append(va.reshape(bs, t_new, HA, 2 * DA))
        outs_s[2].append(kb.reshape(bs, t_new, HB, DB))
        outs_s[3].append(vb.reshape(bs, t_new, HB, DB))
        outs_s[4].append(conv_s)

    return (xp, xs, *[jnp.stack(o) for o in outs_p], *[jnp.stack(o) for o in outs_s])
```

```python
import functools
import math

import jax
import jax.numpy as jnp
from jax import lax
from jax.experimental import pallas as pl
from jax.experimental.pallas import tpu as pltpu

F32 = jnp.float32
BF16 = jnp.bfloat16

CHUNK = 64
HA = 8
DA = 64
HB = 16
DB = 64
LEFT_CHUNKS = 8
MAX_REL = 128
T5_BUCKETS = 32
T5_MAX_DIST = 128
CONV_W = 3
LN_EPS = 1e-5
RMS_EPS = 1e-5
NEG = -1e30
LOG2E = math.log2(math.e)

LANES = 128
ATT_T = 256
ATT_TA = 512
ATT_B_HEADS = 4
VMEM_LIMIT = 56 * 1024 * 1024


def _cparams(sem, vmem=VMEM_LIMIT):
    return pltpu.CompilerParams(dimension_semantics=sem, vmem_limit_bytes=vmem)


def _toeplitz(vec, t):
    x = jnp.broadcast_to(vec, (t, 2 * t))
    row = lax.broadcasted_iota(jnp.int32, (t, 2 * t), 0)
    shift = 1
    while shift < t:
        x = jnp.where((row & shift) != 0, pltpu.roll(x, shift, 1), x)
        shift *= 2
    return x[:, t:]


def _t5_bucket(rel):
    nb = T5_BUCKETS // 2
    max_exact = nb // 2
    ret = jnp.where(rel > 0, nb, 0)
    n = jnp.abs(rel)
    nf = jnp.maximum(n, 1).astype(F32)
    large = max_exact + (jnp.log(nf / max_exact) / math.log(T5_MAX_DIST / max_exact)
                         * (nb - max_exact)).astype(jnp.int32)
    large = jnp.minimum(large, nb - 1)
    return ret + jnp.where(n < max_exact, n, large)


def _chunk_ids(t):
    kc = lax.broadcasted_iota(jnp.int32, (t, t), 0) // CHUNK
    qc = lax.broadcasted_iota(jnp.int32, (t, t), 1) // CHUNK
    return kc, qc


def _bias_a_kernel(tab_ref, out_ref, *, t):
    h = pl.program_id(0)
    r = lax.broadcasted_iota(jnp.int32, (1, 2 * t), 1)
    far = tab_ref[h * T5_BUCKETS + T5_BUCKETS // 2 - 1]

    def band(rel):
        bucket = _t5_bucket(rel)
        acc = jnp.zeros(rel.shape, F32)
        for j in range(T5_BUCKETS):
            acc = jnp.where(bucket == j, tab_ref[h * T5_BUCKETS + j], acc)
        return (acc - far) * LOG2E

    kc, qc = _chunk_ids(t)
    out_ref[0, 0] = jnp.where(kc <= qc, _toeplitz(band(t - r), t), NEG)
    out_ref[0, 1] = _toeplitz(band(-r), t)


def _bias_b_kernel(band_ref, out_ref, *, t):
    own = band_ref[0]
    far = own[:, 2 * t - 1:]
    r = lax.broadcasted_iota(jnp.int32, (1, 2 * t), 1)
    prev = jnp.where(r < t, pltpu.roll(own, t, 1), far)
    kc, qc = _chunk_ids(t)
    out_ref[0, 0:t, :] = jnp.where(kc - 2 * (t // CHUNK) >= qc - LEFT_CHUNKS, 0.0, NEG)
    out_ref[0, t:2 * t, :] = _toeplitz((prev - far) * LOG2E, t)
    out_ref[0, 2 * t:3 * t, :] = jnp.where(kc <= qc, _toeplitz((own - far) * LOG2E, t), NEG)
    out_ref[0, 3 * t:4 * t, :] = jnp.full((t, t), NEG, F32)


def _bias_tiles(t5_table, rel_table):
    ta, tb = ATT_TA, ATT_T
    assert ta >= T5_MAX_DIST and ta % CHUNK == 0
    assert tb >= MAX_REL and 2 * tb == LEFT_CHUNKS * CHUNK
    smem = pl.BlockSpec(memory_space=pltpu.SMEM)
    da = pl.pallas_call(
        functools.partial(_bias_a_kernel, t=ta),
        grid=(HA,),
        in_specs=[smem],
        out_specs=pl.BlockSpec((1, 2, ta, ta), lambda h: (h, 0, 0, 0)),
        out_shape=jax.ShapeDtypeStruct((HA, 2, ta, ta), F32),
        compiler_params=_cparams(("parallel",)),
        name="bias_a",
    )(t5_table.astype(F32).T.reshape(-1))
    band_b = jnp.pad(rel_table.astype(F32).T[:, ::-1],
                     ((0, 0), (tb - MAX_REL, tb - MAX_REL - 1)), mode="edge").reshape(HB, 1, 2 * tb)
    db = pl.pallas_call(
        functools.partial(_bias_b_kernel, t=tb),
        grid=(HB,),
        in_specs=[pl.BlockSpec((1, 1, 2 * tb), lambda h: (h, 0, 0))],
        out_specs=pl.BlockSpec((1, 4 * tb, tb), lambda h: (h, 0, 0)),
        out_shape=jax.ShapeDtypeStruct((HB, 4 * tb, tb), F32),
        compiler_params=_cparams(("parallel",)),
        name="bias_b",
    )(band_b)
    return da, db


def _proj_kernel(x_ref, w_ref, p_ref, ka_ref, va_ref, kb_ref, vb_ref, *, d, q_scale):
    xb = x_ref[...].astype(BF16)
    f32_outs = {1: ka_ref, 2: va_ref, 4: kb_ref, 5: vb_ref}
    for c in range(8):
        acc = jnp.dot(xb, w_ref[:, c * d:(c + 1) * d], preferred_element_type=F32)
        if c in f32_outs:
            f32_outs[c][...] = acc
        if c in (0, 3):
            acc = acc * q_scale
        p_ref[:, c * d:(c + 1) * d] = acc.astype(BF16)


def _proj_cache_kernel(x_ref, w_ref, p_ref, kat_ref, va_ref, kbt_ref, vbt_ref, *,
                       d, q_scale, tm, first_keep, tiles_per_seq):
    t = pl.program_id(0) % tiles_per_seq
    xb = x_ref[...].astype(BF16)
    for c in range(8):
        acc = jnp.dot(xb, w_ref[:, c * d:(c + 1) * d], preferred_element_type=F32)
        if c == 1:
            kat_ref[0] = acc.T
        elif c == 2:
            for h in range(HA):
                va_ref[pl.ds(h, tm, stride=HA), :] = acc[:, h * LANES:(h + 1) * LANES]
        elif c in (4, 5):
            out = kbt_ref if c == 4 else vbt_ref

            @pl.when(t >= first_keep)
            def _(acc=acc, out=out):
                out[0] = acc.T
        if c in (0, 3):
            acc = acc * q_scale
        p_ref[:, c * d:(c + 1) * d] = acc.astype(BF16)


def _project_prompt(x3, w_bf16, keep, tm):
    b, s, d = x3.shape
    n = w_bf16.shape[1]
    assert n == 8 * d and d == HA * 2 * DA == HB * DB == HA * LANES
    assert s % tm == 0 and keep % tm == 0 and tm % LANES == 0
    tiles_per_seq = s // tm
    first_keep = tiles_per_seq - keep // tm
    m = b * s
    row = lambda i: (i, 0)
    col = lambda i: (i // tiles_per_seq, 0, i % tiles_per_seq)
    keep_col = lambda i: (i // tiles_per_seq, 0, jnp.maximum(i % tiles_per_seq - first_keep, 0))
    return pl.pallas_call(
        functools.partial(_proj_cache_kernel, d=d, q_scale=DA ** -0.5 * LOG2E, tm=tm,
                          first_keep=first_keep, tiles_per_seq=tiles_per_seq),
        grid=(m // tm,),
        in_specs=[pl.BlockSpec((tm, d), row),
                  pl.BlockSpec((d, n), lambda i: (0, 0), pipeline_mode=pl.Buffered(1))],
        out_specs=[pl.BlockSpec((tm, n), row),
                   pl.BlockSpec((1, d, tm), col), pl.BlockSpec((tm * HA, LANES), row),
                   pl.BlockSpec((1, d, tm), keep_col), pl.BlockSpec((1, d, tm), keep_col)],
        out_shape=[jax.ShapeDtypeStruct((m, n), BF16),
                   jax.ShapeDtypeStruct((b, d, s), F32), jax.ShapeDtypeStruct((m * HA, LANES), F32),
                   jax.ShapeDtypeStruct((b, d, keep), F32), jax.ShapeDtypeStruct((b, d, keep), F32)],
        compiler_params=_cparams(("arbitrary",)),
        name="in_proj_prompt",
    )(x3.reshape(m, d), w_bf16)


def _project(x2d, w_bf16, rows_per_seq, keep, tm):
    m, d = x2d.shape
    n = w_bf16.shape[1]
    assert n == 8 * d and d == HA * 2 * DA == HB * DB
    assert m % tm == 0
    row = lambda i: (i, 0)
    n_keep = (m // rows_per_seq) * keep
    if keep == rows_per_seq:
        keep_map = row
    else:
        assert rows_per_seq % tm == 0 and keep % tm == 0
        tiles_per_seq = rows_per_seq // tm
        keep_tiles = keep // tm
        first_keep = tiles_per_seq - keep_tiles

        def keep_map(i):
            return ((i // tiles_per_seq) * keep_tiles
                    + jnp.maximum(i % tiles_per_seq - first_keep, 0), 0)
    return pl.pallas_call(
        functools.partial(_proj_kernel, d=d, q_scale=DA ** -0.5 * LOG2E),
        grid=(m // tm,),
        in_specs=[pl.BlockSpec((tm, d), row),
                  pl.BlockSpec((d, n), lambda i: (0, 0), pipeline_mode=pl.Buffered(1))],
        out_specs=[pl.BlockSpec((tm, n), row),
                   pl.BlockSpec((tm, d), row), pl.BlockSpec((tm, d), row),
                   pl.BlockSpec((tm, d), keep_map), pl.BlockSpec((tm, d), keep_map)],
        out_shape=[jax.ShapeDtypeStruct((m, n), BF16),
                   jax.ShapeDtypeStruct((m, d), F32), jax.ShapeDtypeStruct((m, d), F32),
                   jax.ShapeDtypeStruct((n_keep, d), F32), jax.ShapeDtypeStruct((n_keep, d), F32)],
        compiler_params=_cparams(("arbitrary",)),
        name="in_proj",
    )(x2d, w_bf16)


def _half_masks(shape):
    lane = lax.broadcasted_iota(jnp.int32, shape, len(shape) - 1)
    return lane < (LANES // 2), lane >= (LANES // 2)


def _stack_halves(q):
    lo, hi = _half_masks(q.shape)
    zero = jnp.zeros_like(q)
    return jnp.concatenate([jnp.where(lo, q, zero), jnp.where(hi, q, zero)], axis=0)


def _qk(a, b):
    return lax.dot_general(a, b, (((1,), (1,)), ((), ())), preferred_element_type=F32)


def _transpose_blocks(v_ref, vt_ref, t):
    for j in range(vt_ref.shape[0]):
        vt_ref[j] = v_ref[0, j * t:(j + 1) * t, :].astype(F32).T.astype(vt_ref.dtype)


def _softmax_pv(scores, values, values_t=None):
    values_t = values_t or (False,) * len(values)
    m = functools.reduce(jnp.maximum, [jnp.max(s, axis=1, keepdims=True) for s in scores])
    l = None
    o = None
    for s, v, vt in zip(scores, values, values_t):
        p = jnp.exp2(s - m)
        ls = jnp.sum(p, axis=1, keepdims=True)
        pb = p.astype(BF16)
        ov = _qk(pb, v) if vt else jnp.dot(pb, v, preferred_element_type=F32)
        l = ls if l is None else l + ls
        o = ov if o is None else o + ov
    return o / l


def _lam(lq1, lk1, lq2, lk2, lam_init):
    e1 = jnp.exp(jnp.sum(lq1 * lk1, axis=1, keepdims=True))
    e2 = jnp.exp(jnp.sum(lq2 * lk2, axis=1, keepdims=True))
    return e1 - e2 + lam_init


def _attn_a_kernel(lq1_ref, lk1_ref, lq2_ref, lk2_ref, g_ref, q_ref, qn_ref, k_ref, v_ref, d_ref,
                   o_ref, vt_ref, s_ref, mc_ref, m_ref, l_ref, acc_ref, *, t, lam_init):
    qi = pl.program_id(2)
    last_q = pl.num_programs(2) - 1
    q2 = _stack_halves(q_ref[0])

    def scores(q, kb, bias):
        start = pl.multiple_of(kb * t, t)
        s = _qk(k_ref[0, pl.ds(start, t), :], q)
        if bias is not None:
            s = s + jnp.concatenate([bias, bias], axis=1)
        s_ref[...] = s
        mc_ref[...] = jnp.max(s, axis=0, keepdims=True)

    def accumulate(kb, q_next, kb_next, bias_next):
        s = s_ref[...]
        m_prev = m_ref[...]
        m_new = jnp.maximum(m_prev, mc_ref[...])
        alpha = jnp.exp2(m_prev - m_new)
        p = jnp.exp2(s - m_new)
        scores(q_next, kb_next, bias_next)
        l_ref[...] = alpha * l_ref[...] + jnp.sum(p, axis=0, keepdims=True)
        pv = jnp.dot(vt_ref[kb], p.astype(BF16), preferred_element_type=F32)
        acc_ref[...] = alpha * acc_ref[...] + pv
        m_ref[...] = m_new

    @pl.when(qi == 0)
    def _():
        _transpose_blocks(v_ref, vt_ref, t)
        scores(q2, qi, d_ref[0, 0])

    m_ref[...] = jnp.full(m_ref.shape, NEG, F32)
    l_ref[...] = jnp.zeros(l_ref.shape, F32)
    acc_ref[...] = jnp.zeros(acc_ref.shape, F32)

    n_far = jnp.maximum(qi - 1, 0)

    @pl.when(qi >= 1)
    def _():
        accumulate(qi, q2, qi - 1, d_ref[0, 1])

    def far(i, carry):
        accumulate(jnp.where(i == 0, qi - 1, i - 1), q2, i, None)
        return carry

    lax.fori_loop(0, n_far, far, 0)
    accumulate(jnp.where(qi == 0, 0, jnp.where(n_far == 0, qi - 1, n_far - 1)),
               _stack_halves(qn_ref[0]), jnp.minimum(qi + 1, last_q), d_ref[0, 0])

    lam = _lam(lq1_ref[...], lk1_ref[...], lq2_ref[...], lk2_ref[...], lam_init)
    o = acc_ref[...] * (1.0 / l_ref[...])
    o = o[:, :t] - lam * o[:, t:]
    o = o * lax.rsqrt(jnp.mean(o * o, axis=0, keepdims=True) + RMS_EPS)
    o = o * (g_ref[...] * (1.0 - lam_init))
    o_ref[0] = o.T.astype(o_ref.dtype)


def _attn_a_prompt(p3, da, lams, g_col, lam_init):
    b, s, n = p3.shape
    t = ATT_TA
    assert s % t == 0 and n == 8 * HA * LANES
    vec = pl.BlockSpec((1, DA), lambda bi, h, qi: (0, 0))
    return pl.pallas_call(
        functools.partial(_attn_a_kernel, t=t, lam_init=lam_init),
        grid=(b, HA, s // t),
        in_specs=[vec, vec, vec, vec,
                  pl.BlockSpec((LANES, 1), lambda bi, h, qi: (0, 0)),
                  pl.BlockSpec((1, t, LANES), lambda bi, h, qi: (bi, qi, h)),
                  pl.BlockSpec((1, t, LANES), lambda bi, h, qi: (bi, jnp.minimum(qi + 1, s // t - 1), h)),
                  pl.BlockSpec((1, s, LANES), lambda bi, h, qi: (bi, 0, HA + h)),
                  pl.BlockSpec((1, s, LANES), lambda bi, h, qi: (bi, 0, 2 * HA + h)),
                  pl.BlockSpec((1, 2, t, t), lambda bi, h, qi: (h, 0, 0, 0))],
        out_specs=pl.BlockSpec((1, t, LANES), lambda bi, h, qi: (bi, qi, h)),
        out_shape=jax.ShapeDtypeStruct((b, s, HA * LANES), BF16),
        scratch_shapes=[pltpu.VMEM((s // t, LANES, t), BF16),
                        pltpu.VMEM((t, 2 * t), F32), pltpu.VMEM((1, 2 * t), F32),
                        pltpu.VMEM((1, 2 * t), F32), pltpu.VMEM((1, 2 * t), F32),
                        pltpu.VMEM((LANES, 2 * t), F32)],
        compiler_params=_cparams(("parallel", "parallel", "arbitrary")),
        name="attn_a_prompt",
    )(*lams, g_col, p3, p3, p3, p3, da)


def _attn_b_kernel(q_ref, qn_ref, k_ref, v_ref, d_ref, o_ref, vt_ref, s_ref, mc_ref, *, t, heads):
    qi = pl.program_id(2)
    qi_next = jnp.minimum(qi + 1, pl.num_programs(2) - 1)

    def window(q_tile):
        blocks = [jnp.maximum(q_tile - 2 + j, 0) for j in range(3)]
        bias_rows = [pl.multiple_of(jnp.where(q_tile - 2 + j >= 0, j, 3) * t, t) for j in range(3)]
        return blocks, bias_rows

    def scores(g, qr, win):
        blocks, bias_rows = win
        pair = slice((g // 2) * LANES, (g // 2 + 1) * LANES)
        q = qr[0, :, pair]
        q = jnp.where(_half_masks(q.shape)[g % 2], q, jnp.zeros_like(q))
        k = jnp.concatenate([k_ref[0, pl.ds(pl.multiple_of(kb * t, t), t), pair] for kb in blocks],
                            axis=0)
        bias = jnp.concatenate([d_ref[g, pl.ds(r, t), :] for r in bias_rows], axis=0)
        s = _qk(k, q) + bias
        s_ref[...] = s
        mc_ref[...] = jnp.max(s, axis=0, keepdims=True)

    win = window(qi)

    @pl.when(qi == 0)
    def _():
        _transpose_blocks(v_ref, vt_ref, t)
        scores(0, q_ref, win)

    def finish(g):
        p = jnp.exp2(s_ref[...] - mc_ref[...])
        if g + 1 < heads:
            scores(g + 1, q_ref, win)
        else:
            scores(0, qn_ref, window(qi_next))
        l = jnp.sum(p, axis=0, keepdims=True)
        p = p.astype(BF16)
        o = None
        for j, kb in enumerate(win[0]):
            vt = vt_ref[kb, g * DB:(g + 1) * DB, :]
            pv = jnp.dot(vt, p[j * t:(j + 1) * t, :], preferred_element_type=F32)
            o = pv if o is None else o + pv
        return o * (1.0 / l)

    outs = [finish(g) for g in range(heads)]
    o_ref[0] = jnp.concatenate(outs, axis=0).T.astype(o_ref.dtype)


def _attn_b_prompt(p3, db):
    b, s, n = p3.shape
    t = ATT_T
    g = ATT_B_HEADS
    w = g * DB
    assert s % t == 0 and HB % g == 0 and w % LANES == 0
    col0 = 3 * HA * LANES // w
    nblk = HB // g
    return pl.pallas_call(
        functools.partial(_attn_b_kernel, t=t, heads=g),
        grid=(b, nblk, s // t),
        in_specs=[pl.BlockSpec((1, t, w), lambda bi, h, qi: (bi, qi, col0 + h)),
                  pl.BlockSpec((1, t, w),
                               lambda bi, h, qi: (bi, jnp.minimum(qi + 1, s // t - 1), col0 + h)),
                  pl.BlockSpec((1, s, w), lambda bi, h, qi: (bi, 0, col0 + nblk + h)),
                  pl.BlockSpec((1, s, w), lambda bi, h, qi: (bi, 0, col0 + 2 * nblk + h)),
                  pl.BlockSpec((g, 4 * t, t), lambda bi, h, qi: (h, 0, 0))],
        out_specs=pl.BlockSpec((1, t, w), lambda bi, h, qi: (bi, qi, h)),
        out_shape=jax.ShapeDtypeStruct((b, s, HB * DB), BF16),
        scratch_shapes=[pltpu.VMEM((s // t, w, t), BF16),
                        pltpu.VMEM((3 * t, t), F32), pltpu.VMEM((1, t), F32)],
        compiler_params=_cparams(("parallel", "parallel", "arbitrary")),
        name="attn_b_prompt",
    )(p3, p3, p3, p3, db)


def _attn_a_sample_kernel(lq1_ref, lk1_ref, lq2_ref, lk2_ref, g_ref, p_ref, kc_ref, vc_ref, d_ref,
                          o_ref, *, t, lam_init):
    tq = p_ref.shape[1]
    past = kc_ref.shape[2]
    lam = _lam(lq1_ref[...], lk1_ref[...], lq2_ref[...], lk2_ref[...], lam_init)
    for h in range(HA):
        head = lambda sec: p_ref[0, :, (sec * HA + h) * LANES:(sec * HA + h + 1) * LANES]
        q2 = _stack_halves(head(0))
        kn, vn = head(1), head(2)
        kct = kc_ref[0, h * LANES:(h + 1) * LANES, :].astype(BF16)
        vc = vc_ref[0, pl.ds(h, past, stride=HA), :].astype(BF16)
        near = d_ref[h, 1].T[:tq]
        new = d_ref[h, 0, :LANES, :].T[:tq, :tq]
        scores = [jnp.dot(q2, kct[:, :past - t], preferred_element_type=F32),
                  jnp.dot(q2, kct[:, past - t:], preferred_element_type=F32)
                  + jnp.concatenate([near, near], axis=0),
                  _qk(q2, kn) + jnp.concatenate([new, new], axis=0)]
        o = _softmax_pv(scores, [vc[:past - t], vc[past - t:], vn])
        o = o[:tq] - lam * o[tq:]
        o = o * lax.rsqrt(jnp.mean(o * o, axis=1, keepdims=True) + RMS_EPS)
        o_ref[0, :, h * LANES:(h + 1) * LANES] = (o * g_ref[...] * (1.0 - lam_init)).astype(o_ref.dtype)


def _attn_a_sample(p3, cache_kt, cache_v, da, lams, g_row, lam_init):
    b, tq, n = p3.shape
    d, past = cache_kt.shape[1:]
    t = ATT_TA
    assert tq <= CHUNK and past % CHUNK == 0 and past > t and tq % 8 == 0
    vec = pl.BlockSpec((1, DA), lambda bi: (0, 0))
    return pl.pallas_call(
        functools.partial(_attn_a_sample_kernel, t=t, lam_init=lam_init),
        grid=(b,),
        in_specs=[vec, vec, vec, vec,
                  pl.BlockSpec((1, LANES), lambda bi: (0, 0)),
                  pl.BlockSpec((1, tq, n), lambda bi: (bi, 0, 0)),
                  pl.BlockSpec((1, d, past), lambda bi: (bi, 0, 0)),
                  pl.BlockSpec((1, past * HA, LANES), lambda bi: (bi, 0, 0)),
                  pl.BlockSpec((HA, 2, t, LANES), lambda bi: (0, 0, 0, 0),
                               pipeline_mode=pl.Buffered(1))],
        out_specs=pl.BlockSpec((1, tq, d), lambda bi: (bi, 0, 0)),
        out_shape=jax.ShapeDtypeStruct((b, tq, d), BF16),
        compiler_params=_cparams(("parallel",)),
        name="attn_a_sample",
    )(*lams, g_row, p3, cache_kt, cache_v, da)


def _attn_b_sample_kernel(p_ref, kc_ref, vc_ref, d_ref, o_ref, *, t):
    tq = p_ref.shape[1]
    past = kc_ref.shape[2]
    pairs = HB // 2
    olo, _ = _half_masks((tq, LANES))
    for hp in range(pairs):
        rows = slice(hp * LANES, (hp + 1) * LANES)
        head = lambda sec: p_ref[0, :, (3 * HA + sec * pairs + hp) * LANES:
                                 (3 * HA + sec * pairs + hp + 1) * LANES]
        q2 = _stack_halves(head(0))
        kct = kc_ref[0, rows, :].astype(BF16)
        vct = vc_ref[0, rows, :].astype(BF16)
        tiles = [d_ref[2 * hp + hh] for hh in range(2)]
        near = jnp.concatenate([d[t:2 * t, :].T[:tq] for d in tiles], axis=0)
        new = jnp.concatenate([d[2 * t:2 * t + LANES, :].T[:tq, :tq] for d in tiles], axis=0)
        scores = [jnp.dot(q2, kct[:, past - t:], preferred_element_type=F32) + near,
                  _qk(q2, head(1)) + new]
        values = [vct[:, past - t:], head(2)]
        values_t = [True, False]
        if past > t:
            scores.insert(0, jnp.dot(q2, kct[:, :past - t], preferred_element_type=F32))
            values.insert(0, vct[:, :past - t])
            values_t.insert(0, True)
        o = _softmax_pv(scores, values, values_t)
        o_ref[0, :, rows] = jnp.where(olo, o[:tq], o[tq:]).astype(o_ref.dtype)


def _attn_b_sample(p3, cache_kt, cache_vt, db, past_len):
    b, tq, n = p3.shape
    d, pb = cache_kt.shape[1:]
    t = ATT_T
    assert tq <= CHUNK and past_len % CHUNK == 0 and pb == LEFT_CHUNKS * CHUNK and pb >= t
    return pl.pallas_call(
        functools.partial(_attn_b_sample_kernel, t=t),
        grid=(b,),
        in_specs=[pl.BlockSpec((1, tq, n), lambda bi: (bi, 0, 0)),
                  pl.BlockSpec((1, d, pb), lambda bi: (bi, 0, 0)),
                  pl.BlockSpec((1, d, pb), lambda bi: (bi, 0, 0)),
                  pl.BlockSpec((HB, 4 * t, LANES), lambda bi: (0, 0, 0),
                               pipeline_mode=pl.Buffered(1))],
        out_specs=pl.BlockSpec((1, tq, d), lambda bi: (bi, 0, 0)),
        out_shape=jax.ShapeDtypeStruct((b, tq, d), BF16),
        compiler_params=_cparams(("parallel",)),
        name="attn_b_sample",
    )(p3, cache_kt, cache_vt, db)


def _layer_norm(x, g, b):
    mu = jnp.mean(x, axis=1, keepdims=True)
    xc = x - mu
    var = jnp.mean(xc * xc, axis=1, keepdims=True)
    return xc * lax.rsqrt(var + LN_EPS) * g + b


def _tail_kernel(x_ref, oa_ref, ob_ref, ga_ref, gb_ref, prev_ref,
                 wpa_ref, wpb_ref, wout_ref, ln1g_ref, ln1b_ref,
                 wup_ref, wgate_ref, cw_ref, cb_ref, wdown_ref, ln2g_ref, ln2b_ref,
                 y_ref, conv_ref, u_ref, *, tm, alpha, seqs):
    ti = pl.program_id(1)
    pad = 8
    hist = CONV_W - 1
    dff = u_ref.shape[1]

    if seqs == 1:
        @pl.when(ti == 0)
        def _():
            u_ref[pad - hist:pad, :] = prev_ref[0]
    else:
        u_ref[pad - hist:pad, :] = jnp.zeros((hist, dff), F32)

    ya = jnp.dot(oa_ref[0], wpa_ref[...], preferred_element_type=F32)
    yb = jnp.dot(ob_ref[0], wpb_ref[...], preferred_element_type=F32)
    merged = (jax.nn.sigmoid(ga_ref[0].astype(F32)) * ya
              + jax.nn.sigmoid(gb_ref[0].astype(F32)) * yb)
    mixed = jnp.dot(merged.astype(BF16), wout_ref[...], preferred_element_type=F32)
    h = _layer_norm(alpha * x_ref[0] + mixed, ln1g_ref[...], ln1b_ref[...])
    hb = h.astype(BF16)
    u = jnp.dot(hb, wup_ref[...], preferred_element_type=F32)
    g = jnp.dot(hb, wgate_ref[...], preferred_element_type=F32)
    u_ref[pad:pad + tm, :] = u
    u2 = u_ref[pad - 2:pad - 2 + tm, :]
    u1 = u_ref[pad - 1:pad - 1 + tm, :]
    if seqs > 1:
        sl = tm // seqs
        pos = lax.broadcasted_iota(jnp.int32, (tm, 1), 0) % sl
        hist_rows = lambda j: jnp.concatenate(
            [jnp.broadcast_to(prev_ref[q, j:j + 1, :], (sl, dff)) for q in range(seqs)], axis=0)
        u1 = jnp.where(pos == 0, hist_rows(1), u1)
        u2 = jnp.where(pos == 0, hist_rows(0), jnp.where(pos == 1, hist_rows(1), u2))
    uc = cb_ref[...] + u2 * cw_ref[0:1, :]
    uc = uc + u1 * cw_ref[1:2, :]
    uc = uc + u * cw_ref[2:3, :]
    if seqs == 1:
        last = u_ref[pad + tm - hist:pad + tm, :]
        conv_ref[0] = last
        u_ref[pad - hist:pad, :] = last
    else:
        for q in range(seqs):
            conv_ref[q] = u_ref[pad + (q + 1) * sl - hist:pad + (q + 1) * sl, :]
    f = jnp.dot((jax.nn.gelu(uc) * g).astype(BF16), wdown_ref[...], preferred_element_type=F32)
    y_ref[0] = _layer_norm(alpha * h + f, ln2g_ref[...], ln2b_ref[...])


def _tail(x3, oa, ob, p3, conv_prev, w, tm, alpha):
    b, s, d = x3.shape
    dff = w["w_up"].shape[1]
    hist = CONV_W - 1
    seqs = max(tm // s, 1)
    assert (s % tm == 0 or tm % s == 0) and b % seqs == 0 and s >= hist
    if seqs > 1:
        fold = lambda a: a.reshape(b // seqs, seqs * s, a.shape[-1])
        x3, oa, ob, p3 = fold(x3), fold(oa), fold(ob), fold(p3)
    nb, rows = x3.shape[:2]
    const = lambda shape: pl.BlockSpec(shape, lambda bi, ti: (0,) * len(shape),
                                       pipeline_mode=pl.Buffered(1))
    act = lambda col: pl.BlockSpec((1, tm, d), lambda bi, ti: (bi, ti, col))
    y, conv = pl.pallas_call(
        functools.partial(_tail_kernel, tm=tm, alpha=alpha, seqs=seqs),
        grid=(nb, rows // tm),
        in_specs=[act(0), act(0), act(0), act(6), act(7),
                  pl.BlockSpec((seqs, hist, dff), lambda bi, ti: (bi, 0, 0)),
                  const((d, d)), const((d, d)), const((d, d)), const((1, d)), const((1, d)),
                  const((d, dff)), const((d, dff)), const((CONV_W, dff)), const((1, dff)),
                  const((dff, d)), const((1, d)), const((1, d))],
        out_specs=[pl.BlockSpec((1, tm, d), lambda bi, ti: (bi, ti, 0)),
                   pl.BlockSpec((seqs, hist, dff), lambda bi, ti: (bi, 0, 0))],
        out_shape=[jax.ShapeDtypeStruct((nb, rows, d), F32),
                   jax.ShapeDtypeStruct((b, hist, dff), F32)],
        scratch_shapes=[pltpu.VMEM((tm + 8, dff), F32)],
        compiler_params=_cparams(("parallel", "arbitrary")),
        name="tail",
    )(x3, oa, ob, p3, p3, conv_prev,
      w["w_pa"], w["w_pb"], w["w_out"], w["ln1_g"], w["ln1_b"],
      w["w_up"], w["w_gate"], w["conv_w"], w["conv_b"], w["w_down"], w["ln2_g"], w["ln2_b"])
    return y.reshape(b, s, d), conv


def kernel(x_prompt, x_sample, cache_a_k, cache_a_v, cache_b_k, cache_b_v, cache_conv, t5_table, w_in, lambda_q1, lambda_k1, lambda_q2, lambda_k2, subln_g, rel_table_b, w_pa, w_pb, w_out, ln1_g, ln1_b, w_up, w_gate, conv_w, conv_b, w_down, ln2_g, ln2_b):
    depth = w_in.shape[0]
    alpha = (2.0 * depth) ** 0.25
    bp, s, d = x_prompt.shape
    bs, t_new, _ = x_sample.shape
    past = cache_a_k.shape[2]
    pb = cache_b_k.shape[2]
    dff = w_up.shape[2]
    keep = min(LEFT_CHUNKS * CHUNK, s)
    assert pb == min(LEFT_CHUNKS * CHUNK, past)

    xp, xs = x_prompt, x_sample
    outs_p = [[] for _ in range(5)]
    outs_s = [[] for _ in range(5)]
    for l in range(depth):
        lam_init = 0.8 - 0.6 * math.exp(-0.3 * l)
        lams = [v[l].astype(F32).reshape(1, DA) for v in (lambda_q1, lambda_k1, lambda_q2, lambda_k2)]
        g_row = subln_g[l].astype(F32).reshape(1, 2 * DA)
        g_col = subln_g[l].astype(F32).reshape(2 * DA, 1)
        w = {
            "w_pa": w_pa[l].astype(BF16), "w_pb": w_pb[l].astype(BF16), "w_out": w_out[l].astype(BF16),
            "ln1_g": ln1_g[l].reshape(1, d), "ln1_b": ln1_b[l].reshape(1, d),
            "w_up": w_up[l].astype(BF16), "w_gate": w_gate[l].astype(BF16),
            "conv_w": conv_w[l], "conv_b": conv_b[l].reshape(1, dff),
            "w_down": w_down[l].astype(BF16),
            "ln2_g": ln2_g[l].reshape(1, d), "ln2_b": ln2_b[l].reshape(1, d),
        }
        w_in_b = w_in[l].astype(BF16)
        da, db = _bias_tiles(t5_table, rel_table_b[l])

        p, kat, va, kbt, vbt = _project_prompt(xp, w_in_b, keep, ATT_T)
        p3 = p.reshape(bp, s, 8 * d)
        oa = _attn_a_prompt(p3, da, lams, g_col, lam_init)
        ob = _attn_b_prompt(p3, db)
        conv0 = jnp.zeros((bp, CONV_W - 1, dff), F32)
        xp, conv_p = _tail(xp, oa, ob, p3, conv0, w, ATT_T, alpha)
        outs_p[0].append(kat.reshape(bp, HA, 2, DA, s).transpose(0, 4, 1, 2, 3))
        outs_p[1].append(va.reshape(bp, s, HA, 2 * DA))
        outs_p[2].append(kbt.reshape(bp, HB, DB, keep).transpose(0, 3, 1, 2))
        outs_p[3].append(vbt.reshape(bp, HB, DB, keep).transpose(0, 3, 1, 2))
        outs_p[4].append(conv_p)

        rows = bs * t_new
        p, ka, va, kb, vb = _project(xs.reshape(rows, d), w_in_b, t_new, t_new,
                                     ATT_T if rows % ATT_T == 0 else t_new)
        p3 = p.reshape(bs, t_new, 8 * d)
        cak_t = cache_a_k[l].transpose(0, 2, 3, 4, 1).reshape(bs, d, past)
        cbk_t = cache_b_k[l].transpose(0, 2, 3, 1).reshape(bs, d, pb)
        cbv_t = cache_b_v[l].transpose(0, 2, 3, 1).reshape(bs, d, pb)
        oa = _attn_a_sample(p3, cak_t, cache_a_v[l].reshape(bs, past * HA, 2 * DA),
                            da, lams, g_row, lam_init)
        ob = _attn_b_sample(p3, cbk_t, cbv_t, db, past)
        pack = ATT_T // t_new if (ATT_T % t_new == 0 and bs % (ATT_T // t_new) == 0) else 1
        xs, conv_s = _tail(xs, oa, ob, p3, cache_conv[l].astype(F32), w, pack * t_new, alpha)
        outs_s[0].append(ka.reshape(bs, t_new, HA, 2, DA))
        outs_s[1].append(va.reshape(bs, t_new, HA, 2 * DA))
        outs_s[2].append(kb.reshape(bs, t_new, HB, DB))
        outs_s[3].append(vb.reshape(bs, t_new, HB, DB))
        outs_s[4].append(conv_s)

    return (xp, xs, *[jnp.stack(o) for o in outs_p], *[jnp.stack(o) for o in outs_s])
```

```python
import functools
import math

import jax
import jax.numpy as jnp
from jax import lax
from jax.experimental import pallas as pl
from jax.experimental.pallas import tpu as pltpu

F32 = jnp.float32
BF16 = jnp.bfloat16

CHUNK = 64
HA = 8
DA = 64
HB = 16
DB = 64
LEFT_CHUNKS = 8
MAX_REL = 128
T5_BUCKETS = 32
T5_MAX_DIST = 128
CONV_W = 3
LN_EPS = 1e-5
RMS_EPS = 1e-5
NEG = -1e30
LOG2E = math.log2(math.e)

LANES = 128
ATT_T = 256
ATT_TA = 512
ATT_B_HEADS = 4
VMEM_LIMIT = 56 * 1024 * 1024


def _cparams(sem, vmem=VMEM_LIMIT):
    return pltpu.CompilerParams(dimension_semantics=sem, vmem_limit_bytes=vmem)


def _toeplitz(vec, t):
    x = jnp.broadcast_to(vec, (t, 2 * t))
    row = lax.broadcasted_iota(jnp.int32, (t, 2 * t), 0)
    shift = 1
    while shift < t:
        x = jnp.where((row & shift) != 0, pltpu.roll(x, shift, 1), x)
        shift *= 2
    return x[:, t:]


def _t5_bucket(rel):
    nb = T5_BUCKETS // 2
    max_exact = nb // 2
    ret = jnp.where(rel > 0, nb, 0)
    n = jnp.abs(rel)
    nf = jnp.maximum(n, 1).astype(F32)
    large = max_exact + (jnp.log(nf / max_exact) / math.log(T5_MAX_DIST / max_exact)
                         * (nb - max_exact)).astype(jnp.int32)
    large = jnp.minimum(large, nb - 1)
    return ret + jnp.where(n < max_exact, n, large)


def _chunk_ids(t):
    kc = lax.broadcasted_iota(jnp.int32, (t, t), 0) // CHUNK
    qc = lax.broadcasted_iota(jnp.int32, (t, t), 1) // CHUNK
    return kc, qc


def _bias_a_kernel(tab_ref, out_ref, *, t):
    h = pl.program_id(0)
    r = lax.broadcasted_iota(jnp.int32, (1, 2 * t), 1)
    far = tab_ref[h * T5_BUCKETS + T5_BUCKETS // 2 - 1]

    def band(rel):
        bucket = _t5_bucket(rel)
        acc = jnp.zeros(rel.shape, F32)
        for j in range(T5_BUCKETS):
            acc = jnp.where(bucket == j, tab_ref[h * T5_BUCKETS + j], acc)
        return (acc - far) * LOG2E

    kc, qc = _chunk_ids(t)
    out_ref[0, 0] = jnp.where(kc <= qc, _toeplitz(band(t - r), t), NEG)
    out_ref[0, 1] = _toeplitz(band(-r), t)


def _bias_b_kernel(band_ref, out_ref, *, t):
    own = band_ref[0]
    far = own[:, 2 * t - 1:]
    r = lax.broadcasted_iota(jnp.int32, (1, 2 * t), 1)
    prev = jnp.where(r < t, pltpu.roll(own, t, 1), far)
    kc, qc = _chunk_ids(t)
    out_ref[0, 0:t, :] = jnp.where(kc - 2 * (t // CHUNK) >= qc - LEFT_CHUNKS, 0.0, NEG)
    out_ref[0, t:2 * t, :] = _toeplitz((prev - far) * LOG2E, t)
    out_ref[0, 2 * t:3 * t, :] = jnp.where(kc <= qc, _toeplitz((own - far) * LOG2E, t), NEG)
    out_ref[0, 3 * t:4 * t, :] = jnp.full((t, t), NEG, F32)


def _bias_tiles(t5_table, rel_table):
    ta, tb = ATT_TA, ATT_T
    assert ta >= T5_MAX_DIST and ta % CHUNK == 0
    assert tb >= MAX_REL and 2 * tb == LEFT_CHUNKS * CHUNK
    smem = pl.BlockSpec(memory_space=pltpu.SMEM)
    da = pl.pallas_call(
        functools.partial(_bias_a_kernel, t=ta),
        grid=(HA,),
        in_specs=[smem],
        out_specs=pl.BlockSpec((1, 2, ta, ta), lambda h: (h, 0, 0, 0)),
        out_shape=jax.ShapeDtypeStruct((HA, 2, ta, ta), F32),
        compiler_params=_cparams(("parallel",)),
        name="bias_a",
    )(t5_table.astype(F32).T.reshape(-1))
    band_b = jnp.pad(rel_table.astype(F32).T[:, ::-1],
                     ((0, 0), (tb - MAX_REL, tb - MAX_REL - 1)), mode="edge").reshape(HB, 1, 2 * tb)
    db = pl.pallas_call(
        functools.partial(_bias_b_kernel, t=tb),
        grid=(HB,),
        in_specs=[pl.BlockSpec((1, 1, 2 * tb), lambda h: (h, 0, 0))],
        out_specs=pl.BlockSpec((1, 4 * tb, tb), lambda h: (h, 0, 0)),
        out_shape=jax.ShapeDtypeStruct((HB, 4 * tb, tb), F32),
        compiler_params=_cparams(("parallel",)),
        name="bias_b",
    )(band_b)
    return da, db


def _proj_kernel(x_ref, w_ref, p_ref, ka_ref, va_ref, kb_ref, vb_ref, *, d, q_scale):
    xb = x_ref[...].astype(BF16)
    f32_outs = {1: ka_ref, 2: va_ref, 4: kb_ref, 5: vb_ref}
    for c in range(8):
        acc = jnp.dot(xb, w_ref[:, c * d:(c + 1) * d], preferred_element_type=F32)
        if c in f32_outs:
            f32_outs[c][...] = acc
        if c in (0, 3):
            acc = acc * q_scale
        p_ref[:, c * d:(c + 1) * d] = acc.astype(BF16)


def _proj_cache_kernel(x_ref, w_ref, p_ref, kat_ref, va_ref, kbt_ref, vbt_ref, *,
                       d, q_scale, tm, first_keep, tiles_per_seq):
    t = pl.program_id(0) % tiles_per_seq
    xb = x_ref[...].astype(BF16)
    for c in range(8):
        acc = jnp.dot(xb, w_ref[:, c * d:(c + 1) * d], preferred_element_type=F32)
        if c == 1:
            kat_ref[0] = acc.T
        elif c == 2:
            for h in range(HA):
                va_ref[pl.ds(h, tm, stride=HA), :] = acc[:, h * LANES:(h + 1) * LANES]
        elif c in (4, 5):
            out = kbt_ref if c == 4 else vbt_ref

            @pl.when(t >= first_keep)
            def _(acc=acc, out=out):
                out[0] = acc.T
        if c in (0, 3):
            acc = acc * q_scale
        p_ref[:, c * d:(c + 1) * d] = acc.astype(BF16)


def _project_prompt(x3, w_bf16, keep, tm):
    b, s, d = x3.shape
    n = w_bf16.shape[1]
    assert n == 8 * d and d == HA * 2 * DA == HB * DB == HA * LANES
    assert s % tm == 0 and keep % tm == 0 and tm % LANES == 0
    tiles_per_seq = s // tm
    first_keep = tiles_per_seq - keep // tm
    m = b * s
    row = lambda i: (i, 0)
    col = lambda i: (i // tiles_per_seq, 0, i % tiles_per_seq)
    keep_col = lambda i: (i // tiles_per_seq, 0, jnp.maximum(i % tiles_per_seq - first_keep, 0))
    return pl.pallas_call(
        functools.partial(_proj_cache_kernel, d=d, q_scale=DA ** -0.5 * LOG2E, tm=tm,
                          first_keep=first_keep, tiles_per_seq=tiles_per_seq),
        grid=(m // tm,),
        in_specs=[pl.BlockSpec((tm, d), row),
                  pl.BlockSpec((d, n), lambda i: (0, 0), pipeline_mode=pl.Buffered(1))],
        out_specs=[pl.BlockSpec((tm, n), row),
                   pl.BlockSpec((1, d, tm), col), pl.BlockSpec((tm * HA, LANES), row),
                   pl.BlockSpec((1, d, tm), keep_col), pl.BlockSpec((1, d, tm), keep_col)],
        out_shape=[jax.ShapeDtypeStruct((m, n), BF16),
                   jax.ShapeDtypeStruct((b, d, s), F32), jax.ShapeDtypeStruct((m * HA, LANES), F32),
                   jax.ShapeDtypeStruct((b, d, keep), F32), jax.ShapeDtypeStruct((b, d, keep), F32)],
        compiler_params=_cparams(("arbitrary",)),
        name="in_proj_prompt",
    )(x3.reshape(m, d), w_bf16)


def _project(x2d, w_bf16, rows_per_seq, keep, tm):
    m, d = x2d.shape
    n = w_bf16.shape[1]
    assert n == 8 * d and d == HA * 2 * DA == HB * DB
    assert m % tm == 0
    row = lambda i: (i, 0)
    n_keep = (m // rows_per_seq) * keep
    if keep == rows_per_seq:
        keep_map = row
    else:
        assert rows_per_seq % tm == 0 and keep % tm == 0
        tiles_per_seq = rows_per_seq // tm
        keep_tiles = keep // tm
        first_keep = tiles_per_seq - keep_tiles

        def keep_map(i):
            return ((i // tiles_per_seq) * keep_tiles
                    + jnp.maximum(i % tiles_per_seq - first_keep, 0), 0)
    return pl.pallas_call(
        functools.partial(_proj_kernel, d=d, q_scale=DA ** -0.5 * LOG2E),
        grid=(m // tm,),
        in_specs=[pl.BlockSpec((tm, d), row),
                  pl.BlockSpec((d, n), lambda i: (0, 0), pipeline_mode=pl.Buffered(1))],
        out_specs=[pl.BlockSpec((tm, n), row),
                   pl.BlockSpec((tm, d), row), pl.BlockSpec((tm, d), row),
                   pl.BlockSpec((tm, d), keep_map), pl.BlockSpec((tm, d), keep_map)],
        out_shape=[jax.ShapeDtypeStruct((m, n), BF16),
                   jax.ShapeDtypeStruct((m, d), F32), jax.ShapeDtypeStruct((m, d), F32),
                   jax.ShapeDtypeStruct((n_keep, d), F32), jax.ShapeDtypeStruct((n_keep, d), F32)],
        compiler_params=_cparams(("arbitrary",)),
        name="in_proj",
    )(x2d, w_bf16)


def _half_masks(shape):
    lane = lax.broadcasted_iota(jnp.int32, shape, len(shape) - 1)
    return lane < (LANES // 2), lane >= (LANES // 2)


def _stack_halves(q):
    lo, hi = _half_masks(q.shape)
    zero = jnp.zeros_like(q)
    return jnp.concatenate([jnp.where(lo, q, zero), jnp.where(hi, q, zero)], axis=0)


def _qk(a, b):
    return lax.dot_general(a, b, (((1,), (1,)), ((), ())), preferred_element_type=F32)


def _transpose_blocks(v_ref, vt_ref, t):
    for j in range(vt_ref.shape[0]):
        vt_ref[j] = v_ref[0, j * t:(j + 1) * t, :].astype(F32).T.astype(vt_ref.dtype)


def _softmax_pv(scores, values, values_t=None):
    values_t = values_t or (False,) * len(values)
    m = functools.reduce(jnp.maximum, [jnp.max(s, axis=1, keepdims=True) for s in scores])
    l = None
    o = None
    for s, v, vt in zip(scores, values, values_t):
        p = jnp.exp2(s - m)
        ls = jnp.sum(p, axis=1, keepdims=True)
        pb = p.astype(BF16)
        ov = _qk(pb, v) if vt else jnp.dot(pb, v, preferred_element_type=F32)
        l = ls if l is None else l + ls
        o = ov if o is None else o + ov
    return o / l


def _lam(lq1, lk1, lq2, lk2, lam_init):
    e1 = jnp.exp(jnp.sum(lq1 * lk1, axis=1, keepdims=True))
    e2 = jnp.exp(jnp.sum(lq2 * lk2, axis=1, keepdims=True))
    return e1 - e2 + lam_init


def _attn_a_kernel(lq1_ref, lk1_ref, lq2_ref, lk2_ref, g_ref, q_ref, k_ref, v_ref, d_ref,
                   o_ref, vt_ref, s_ref, mc_ref, m_ref, l_ref, acc_ref, *, t, lam_init):
    n_tiles = q_ref.shape[1] // t
    q_tile = lambda qi: _stack_halves(q_ref[0, pl.ds(pl.multiple_of(qi * t, t), t), :])
    lam = _lam(lq1_ref[...], lk1_ref[...], lq2_ref[...], lk2_ref[...], lam_init)

    def scores(q, kb, bias):
        start = pl.multiple_of(kb * t, t)
        s = _qk(k_ref[0, pl.ds(start, t), :], q)
        if bias is not None:
            s = s + jnp.concatenate([bias, bias], axis=1)
        s_ref[...] = s
        mc_ref[...] = jnp.max(s, axis=0, keepdims=True)

    def accumulate(kb, q_next, kb_next, bias_next):
        s = s_ref[...]
        m_prev = m_ref[...]
        m_new = jnp.maximum(m_prev, mc_ref[...])
        alpha = jnp.exp2(m_prev - m_new)
        p = jnp.exp2(s - m_new)
        scores(q_next, kb_next, bias_next)
        l_ref[...] = alpha * l_ref[...] + jnp.sum(p, axis=0, keepdims=True)
        pv = jnp.dot(vt_ref[kb], p.astype(BF16), preferred_element_type=F32)
        acc_ref[...] = alpha * acc_ref[...] + pv
        m_ref[...] = m_new

    _transpose_blocks(v_ref, vt_ref, t)
    scores(q_tile(0), 0, d_ref[0, 0])

    def tile(qi, carry):
        q2 = q_tile(qi)
        m_ref[...] = jnp.full(m_ref.shape, NEG, F32)
        l_ref[...] = jnp.zeros(l_ref.shape, F32)
        acc_ref[...] = jnp.zeros(acc_ref.shape, F32)

        n_far = jnp.maximum(qi - 1, 0)

        @pl.when(qi >= 1)
        def _():
            accumulate(qi, q2, qi - 1, d_ref[0, 1])

        def far(i, c):
            accumulate(jnp.where(i == 0, qi - 1, i - 1), q2, i, None)
            return c

        lax.fori_loop(0, n_far, far, 0)
        nxt = jnp.minimum(qi + 1, n_tiles - 1)
        accumulate(jnp.where(qi == 0, 0, jnp.where(n_far == 0, qi - 1, n_far - 1)),
                   q_tile(nxt), nxt, d_ref[0, 0])

        o = acc_ref[...] * (1.0 / l_ref[...])
        o = o[:, :t] - lam * o[:, t:]
        o = o * lax.rsqrt(jnp.mean(o * o, axis=0, keepdims=True) + RMS_EPS)
        o = o * (g_ref[...] * (1.0 - lam_init))
        o_ref[0, pl.ds(pl.multiple_of(qi * t, t), t), :] = o.T.astype(o_ref.dtype)
        return carry

    lax.fori_loop(0, n_tiles, tile, 0)


def _attn_a_prompt(p3, da, lams, g_col, lam_init):
    b, s, n = p3.shape
    t = ATT_TA
    assert s % t == 0 and n == 8 * HA * LANES
    vec = pl.BlockSpec((1, DA), lambda bi, h: (0, 0))
    return pl.pallas_call(
        functools.partial(_attn_a_kernel, t=t, lam_init=lam_init),
        grid=(b, HA),
        in_specs=[vec, vec, vec, vec,
                  pl.BlockSpec((LANES, 1), lambda bi, h: (0, 0)),
                  pl.BlockSpec((1, s, LANES), lambda bi, h: (bi, 0, h)),
                  pl.BlockSpec((1, s, LANES), lambda bi, h: (bi, 0, HA + h)),
                  pl.BlockSpec((1, s, LANES), lambda bi, h: (bi, 0, 2 * HA + h)),
                  pl.BlockSpec((1, 2, t, t), lambda bi, h: (h, 0, 0, 0))],
        out_specs=pl.BlockSpec((1, s, LANES), lambda bi, h: (bi, 0, h)),
        out_shape=jax.ShapeDtypeStruct((b, s, HA * LANES), BF16),
        scratch_shapes=[pltpu.VMEM((s // t, LANES, t), BF16),
                        pltpu.VMEM((t, 2 * t), F32), pltpu.VMEM((1, 2 * t), F32),
                        pltpu.VMEM((1, 2 * t), F32), pltpu.VMEM((1, 2 * t), F32),
                        pltpu.VMEM((LANES, 2 * t), F32)],
        compiler_params=_cparams(("parallel", "parallel")),
        name="attn_a_prompt",
    )(*lams, g_col, p3, p3, p3, da)


def _attn_b_kernel(q_ref, k_ref, v_ref, d_ref, o_ref, vt_ref, s_ref, mc_ref, *, t, heads):
    n_tiles = q_ref.shape[1] // t

    def window(q_tile):
        blocks = [jnp.maximum(q_tile - 2 + j, 0) for j in range(3)]
        bias_rows = [pl.multiple_of(jnp.where(q_tile - 2 + j >= 0, j, 3) * t, t) for j in range(3)]
        return blocks, bias_rows

    def scores(g, q_tile):
        blocks, bias_rows = window(q_tile)
        pair = slice((g // 2) * LANES, (g // 2 + 1) * LANES)
        q = q_ref[0, pl.ds(pl.multiple_of(q_tile * t, t), t), pair]
        q = jnp.where(_half_masks(q.shape)[g % 2], q, jnp.zeros_like(q))
        k = jnp.concatenate([k_ref[0, pl.ds(pl.multiple_of(kb * t, t), t), pair] for kb in blocks],
                            axis=0)
        bias = jnp.concatenate([d_ref[g, pl.ds(r, t), :] for r in bias_rows], axis=0)
        s = _qk(k, q) + bias
        s_ref[g] = s
        mc_ref[g] = jnp.max(s, axis=0, keepdims=True)

    _transpose_blocks(v_ref, vt_ref, t)
    scores(0, 0)

    def tile(qi, carry):
        blocks, _ = window(qi)

        def finish(g):
            p = jnp.exp2(s_ref[g] - mc_ref[g])
            if g + 1 < heads:
                scores(g + 1, qi)
            else:
                scores(0, jnp.minimum(qi + 1, n_tiles - 1))
            l = jnp.sum(p, axis=0, keepdims=True)
            p = p.astype(BF16)
            o = None
            for j, kb in enumerate(blocks):
                vt = vt_ref[kb, g * DB:(g + 1) * DB, :]
                pv = jnp.dot(vt, p[j * t:(j + 1) * t, :], preferred_element_type=F32)
                o = pv if o is None else o + pv
            return o * (1.0 / l)

        outs = [finish(g) for g in range(heads)]
        o_ref[0, pl.ds(pl.multiple_of(qi * t, t), t), :] = (
            jnp.concatenate(outs, axis=0).T.astype(o_ref.dtype))
        return carry

    lax.fori_loop(0, n_tiles, tile, 0)


def _attn_b_prompt(p3, db):
    b, s, n = p3.shape
    t = ATT_T
    g = ATT_B_HEADS
    w = g * DB
    assert s % t == 0 and HB % g == 0 and w % LANES == 0
    col0 = 3 * HA * LANES // w
    nblk = HB // g
    return pl.pallas_call(
        functools.partial(_attn_b_kernel, t=t, heads=g),
        grid=(b, nblk),
        in_specs=[pl.BlockSpec((1, s, w), lambda bi, h: (bi, 0, col0 + h)),
                  pl.BlockSpec((1, s, w), lambda bi, h: (bi, 0, col0 + nblk + h)),
                  pl.BlockSpec((1, s, w), lambda bi, h: (bi, 0, col0 + 2 * nblk + h)),
                  pl.BlockSpec((g, 4 * t, t), lambda bi, h: (h, 0, 0))],
        out_specs=pl.BlockSpec((1, s, w), lambda bi, h: (bi, 0, h)),
        out_shape=jax.ShapeDtypeStruct((b, s, HB * DB), BF16),
        scratch_shapes=[pltpu.VMEM((s // t, w, t), BF16),
                        pltpu.VMEM((g, 3 * t, t), F32), pltpu.VMEM((g, 1, t), F32)],
        compiler_params=_cparams(("parallel", "parallel")),
        name="attn_b_prompt",
    )(p3, p3, p3, db)


def _attn_a_sample_kernel(lq1_ref, lk1_ref, lq2_ref, lk2_ref, g_ref, p_ref, kc_ref, vc_ref, d_ref,
                          o_ref, *, t, lam_init):
    tq = p_ref.shape[1]
    past = kc_ref.shape[2]
    lam = _lam(lq1_ref[...], lk1_ref[...], lq2_ref[...], lk2_ref[...], lam_init)
    for h in range(HA):
        head = lambda sec: p_ref[0, :, (sec * HA + h) * LANES:(sec * HA + h + 1) * LANES]
        q2 = _stack_halves(head(0))
        kn, vn = head(1), head(2)
        kct = kc_ref[0, h * LANES:(h + 1) * LANES, :].astype(BF16)
        vc = vc_ref[0, pl.ds(h, past, stride=HA), :].astype(BF16)
        near = d_ref[h, 1].T[:tq]
        new = d_ref[h, 0, :LANES, :].T[:tq, :tq]
        scores = [jnp.dot(q2, kct[:, :past - t], preferred_element_type=F32),
                  jnp.dot(q2, kct[:, past - t:], preferred_element_type=F32)
                  + jnp.concatenate([near, near], axis=0),
                  _qk(q2, kn) + jnp.concatenate([new, new], axis=0)]
        o = _softmax_pv(scores, [vc[:past - t], vc[past - t:], vn])
        o = o[:tq] - lam * o[tq:]
        o = o * lax.rsqrt(jnp.mean(o * o, axis=1, keepdims=True) + RMS_EPS)
        o_ref[0, :, h * LANES:(h + 1) * LANES] = (o * g_ref[...] * (1.0 - lam_init)).astype(o_ref.dtype)


def _attn_a_sample(p3, cache_kt, cache_v, da, lams, g_row, lam_init):
    b, tq, n = p3.shape
    d, past = cache_kt.shape[1:]
    t = ATT_TA
    assert tq <= CHUNK and past % CHUNK == 0 and past > t and tq % 8 == 0
    vec = pl.BlockSpec((1, DA), lambda bi: (0, 0))
    return pl.pallas_call(
        functools.partial(_attn_a_sample_kernel, t=t, lam_init=lam_init),
        grid=(b,),
        in_specs=[vec, vec, vec, vec,
                  pl.BlockSpec((1, LANES), lambda bi: (0, 0)),
                  pl.BlockSpec((1, tq, n), lambda bi: (bi, 0, 0)),
                  pl.BlockSpec((1, d, past), lambda bi: (bi, 0, 0)),
                  pl.BlockSpec((1, past * HA, LANES), lambda bi: (bi, 0, 0)),
                  pl.BlockSpec((HA, 2, t, LANES), lambda bi: (0, 0, 0, 0),
                               pipeline_mode=pl.Buffered(1))],
        out_specs=pl.BlockSpec((1, tq, d), lambda bi: (bi, 0, 0)),
        out_shape=jax.ShapeDtypeStruct((b, tq, d), BF16),
        compiler_params=_cparams(("parallel",)),
        name="attn_a_sample",
    )(*lams, g_row, p3, cache_kt, cache_v, da)


def _attn_b_sample_kernel(p_ref, kc_ref, vc_ref, d_ref, o_ref, *, t):
    tq = p_ref.shape[1]
    past = kc_ref.shape[2]
    pairs = HB // 2
    olo, _ = _half_masks((tq, LANES))
    for hp in range(pairs):
        rows = slice(hp * LANES, (hp + 1) * LANES)
        head = lambda sec: p_ref[0, :, (3 * HA + sec * pairs + hp) * LANES:
                                 (3 * HA + sec * pairs + hp + 1) * LANES]
        q2 = _stack_halves(head(0))
        kct = kc_ref[0, rows, :].astype(BF16)
        vct = vc_ref[0, rows, :].astype(BF16)
        tiles = [d_ref[2 * hp + hh] for hh in range(2)]
        near = jnp.concatenate([d[t:2 * t, :].T[:tq] for d in tiles], axis=0)
        new = jnp.concatenate([d[2 * t:2 * t + LANES, :].T[:tq, :tq] for d in tiles], axis=0)
        scores = [jnp.dot(q2, kct[:, past - t:], preferred_element_type=F32) + near,
                  _qk(q2, head(1)) + new]
        values = [vct[:, past - t:], head(2)]
        values_t = [True, False]
        if past > t:
            scores.insert(0, jnp.dot(q2, kct[:, :past - t], preferred_element_type=F32))
            values.insert(0, vct[:, :past - t])
            values_t.insert(0, True)
        o = _softmax_pv(scores, values, values_t)
        o_ref[0, :, rows] = jnp.where(olo, o[:tq], o[tq:]).astype(o_ref.dtype)


def _attn_b_sample(p3, cache_kt, cache_vt, db, past_len):
    b, tq, n = p3.shape
    d, pb = cache_kt.shape[1:]
    t = ATT_T
    assert tq <= CHUNK and past_len % CHUNK == 0 and pb == LEFT_CHUNKS * CHUNK and pb >= t
    return pl.pallas_call(
        functools.partial(_attn_b_sample_kernel, t=t),
        grid=(b,),
        in_specs=[pl.BlockSpec((1, tq, n), lambda bi: (bi, 0, 0)),
                  pl.BlockSpec((1, d, pb), lambda bi: (bi, 0, 0)),
                  pl.BlockSpec((1, d, pb), lambda bi: (bi, 0, 0)),
                  pl.BlockSpec((HB, 4 * t, LANES), lambda bi: (0, 0, 0),
                               pipeline_mode=pl.Buffered(1))],
        out_specs=pl.BlockSpec((1, tq, d), lambda bi: (bi, 0, 0)),
        out_shape=jax.ShapeDtypeStruct((b, tq, d), BF16),
        compiler_params=_cparams(("parallel",)),
        name="attn_b_sample",
    )(p3, cache_kt, cache_vt, db)


def _layer_norm(x, g, b):
    mu = jnp.mean(x, axis=1, keepdims=True)
    xc = x - mu
    var = jnp.mean(xc * xc, axis=1, keepdims=True)
    return xc * lax.rsqrt(var + LN_EPS) * g + b


def _tail_kernel(x_ref, oa_ref, ob_ref, ga_ref, gb_ref, prev_ref,
                 wpa_ref, wpb_ref, wout_ref, ln1g_ref, ln1b_ref,
                 wup_ref, wgate_ref, cw_ref, cb_ref, wdown_ref, ln2g_ref, ln2b_ref,
                 y_ref, conv_ref, u_ref, *, tm, alpha, seqs):
    ti = pl.program_id(1)
    pad = 8
    hist = CONV_W - 1
    dff = u_ref.shape[1]

    if seqs == 1:
        @pl.when(ti == 0)
        def _():
            u_ref[pad - hist:pad, :] = prev_ref[0]
    else:
        u_ref[pad - hist:pad, :] = jnp.zeros((hist, dff), F32)

    ya = jnp.dot(oa_ref[0], wpa_ref[...], preferred_element_type=F32)
    yb = jnp.dot(ob_ref[0], wpb_ref[...], preferred_element_type=F32)
    merged = (jax.nn.sigmoid(ga_ref[0].astype(F32)) * ya
              + jax.nn.sigmoid(gb_ref[0].astype(F32)) * yb)
    mixed = jnp.dot(merged.astype(BF16), wout_ref[...], preferred_element_type=F32)
    h = _layer_norm(alpha * x_ref[0] + mixed, ln1g_ref[...], ln1b_ref[...])
    hb = h.astype(BF16)
    u = jnp.dot(hb, wup_ref[...], preferred_element_type=F32)
    g = jnp.dot(hb, wgate_ref[...], preferred_element_type=F32)
    u_ref[pad:pad + tm, :] = u
    u2 = u_ref[pad - 2:pad - 2 + tm, :]
    u1 = u_ref[pad - 1:pad - 1 + tm, :]
    if seqs > 1:
        sl = tm // seqs
        pos = lax.broadcasted_iota(jnp.int32, (tm, 1), 0) % sl
        hist_rows = lambda j: jnp.concatenate(
            [jnp.broadcast_to(prev_ref[q, j:j + 1, :], (sl, dff)) for q in range(seqs)], axis=0)
        u1 = jnp.where(pos == 0, hist_rows(1), u1)
        u2 = jnp.where(pos == 0, hist_rows(0), jnp.where(pos == 1, hist_rows(1), u2))
    uc = cb_ref[...] + u2 * cw_ref[0:1, :]
    uc = uc + u1 * cw_ref[1:2, :]
    uc = uc + u * cw_ref[2:3, :]
    if seqs == 1:
        last = u_ref[pad + tm - hist:pad + tm, :]
        conv_ref[0] = last
        u_ref[pad - hist:pad, :] = last
    else:
        for q in range(seqs):
            conv_ref[q] = u_ref[pad + (q + 1) * sl - hist:pad + (q + 1) * sl, :]
    f = jnp.dot((jax.nn.gelu(uc) * g).astype(BF16), wdown_ref[...], preferred_element_type=F32)
    y_ref[0] = _layer_norm(alpha * h + f, ln2g_ref[...], ln2b_ref[...])


def _tail(x3, oa, ob, p3, conv_prev, w, tm, alpha):
    b, s, d = x3.shape
    dff = w["w_up"].shape[1]
    hist = CONV_W - 1
    seqs = max(tm // s, 1)
    assert (s % tm == 0 or tm % s == 0) and b % seqs == 0 and s >= hist
    if seqs > 1:
        fold = lambda a: a.reshape(b // seqs, seqs * s, a.shape[-1])
        x3, oa, ob, p3 = fold(x3), fold(oa), fold(ob), fold(p3)
    nb, rows = x3.shape[:2]
    const = lambda shape: pl.BlockSpec(shape, lambda bi, ti: (0,) * len(shape),
                                       pipeline_mode=pl.Buffered(1))
    act = lambda col: pl.BlockSpec((1, tm, d), lambda bi, ti: (bi, ti, col))
    y, conv = pl.pallas_call(
        functools.partial(_tail_kernel, tm=tm, alpha=alpha, seqs=seqs),
        grid=(nb, rows // tm),
        in_specs=[act(0), act(0), act(0), act(6), act(7),
                  pl.BlockSpec((seqs, hist, dff), lambda bi, ti: (bi, 0, 0)),
                  const((d, d)), const((d, d)), const((d, d)), const((1, d)), const((1, d)),
                  const((d, dff)), const((d, dff)), const((CONV_W, dff)), const((1, dff)),
                  const((dff, d)), const((1, d)), const((1, d))],
        out_specs=[pl.BlockSpec((1, tm, d), lambda bi, ti: (bi, ti, 0)),
                   pl.BlockSpec((seqs, hist, dff), lambda bi, ti: (bi, 0, 0))],
        out_shape=[jax.ShapeDtypeStruct((nb, rows, d), F32),
                   jax.ShapeDtypeStruct((b, hist, dff), F32)],
        scratch_shapes=[pltpu.VMEM((tm + 8, dff), F32)],
        compiler_params=_cparams(("parallel", "arbitrary")),
        name="tail",
    )(x3, oa, ob, p3, p3, conv_prev,
      w["w_pa"], w["w_pb"], w["w_out"], w["ln1_g"], w["ln1_b"],
      w["w_up"], w["w_gate"], w["conv_w"], w["conv_b"], w["w_down"], w["ln2_g"], w["ln2_b"])
    return y.reshape(b, s, d), conv


def kernel(x_prompt, x_sample, cache_a_k, cache_a_v, cache_b_k, cache_b_v, cache_conv, t5_table, w_in, lambda_q1, lambda_k1, lambda_q2, lambda_k2, subln_g, rel_table_b, w_pa, w_pb, w_out, ln1_g, ln1_b, w_up, w_gate, conv_w, conv_b, w_down, ln2_g, ln2_b):
    depth = w_in.shape[0]
    alpha = (2.0 * depth) ** 0.25
    bp, s, d = x_prompt.shape
    bs, t_new, _ = x_sample.shape
    past = cache_a_k.shape[2]
    pb = cache_b_k.shape[2]
    dff = w_up.shape[2]
    keep = min(LEFT_CHUNKS * CHUNK, s)
    assert pb == min(LEFT_CHUNKS * CHUNK, past)

    xp, xs = x_prompt, x_sample
    outs_p = [[] for _ in range(5)]
    outs_s = [[] for _ in range(5)]
    for l in range(depth):
        lam_init = 0.8 - 0.6 * math.exp(-0.3 * l)
        lams = [v[l].astype(F32).reshape(1, DA) for v in (lambda_q1, lambda_k1, lambda_q2, lambda_k2)]
        g_row = subln_g[l].astype(F32).reshape(1, 2 * DA)
        g_col = subln_g[l].astype(F32).reshape(2 * DA, 1)
        w = {
            "w_pa": w_pa[l].astype(BF16), "w_pb": w_pb[l].astype(BF16), "w_out": w_out[l].astype(BF16),
            "ln1_g": ln1_g[l].reshape(1, d), "ln1_b": ln1_b[l].reshape(1, d),
            "w_up": w_up[l].astype(BF16), "w_gate": w_gate[l].astype(BF16),
            "conv_w": conv_w[l], "conv_b": conv_b[l].reshape(1, dff),
            "w_down": w_down[l].astype(BF16),
            "ln2_g": ln2_g[l].reshape(1, d), "ln2_b": ln2_b[l].reshape(1, d),
        }
        w_in_b = w_in[l].astype(BF16)
        da, db = _bias_tiles(t5_table, rel_table_b[l])

        p, kat, va, kbt, vbt = _project_prompt(xp, w_in_b, keep, ATT_T)
        p3 = p.reshape(bp, s, 8 * d)
        oa = _attn_a_prompt(p3, da, lams, g_col, lam_init)
        ob = _attn_b_prompt(p3, db)
        conv0 = jnp.zeros((bp, CONV_W - 1, dff), F32)
        xp, conv_p = _tail(xp, oa, ob, p3, conv0, w, ATT_T, alpha)
        outs_p[0].append(kat.reshape(bp, HA, 2, DA, s).transpose(0, 4, 1, 2, 3))
        outs_p[1].append(va.reshape(bp, s, HA, 2 * DA))
        outs_p[2].append(kbt.reshape(bp, HB, DB, keep).transpose(0, 3, 1, 2))
        outs_p[3].append(vbt.reshape(bp, HB, DB, keep).transpose(0, 3, 1, 2))
        outs_p[4].append(conv_p)

        rows = bs * t_new
        p, ka, va, kb, vb = _project(xs.reshape(rows, d), w_in_b, t_new, t_new,
                                     ATT_T if rows % ATT_T == 0 else t_new)
        p3 = p.reshape(bs, t_new, 8 * d)
        cak_t = cache_a_k[l].transpose(0, 2, 3, 4, 1).reshape(bs, d, past)
        cbk_t = cache_b_k[l].transpose(0, 2, 3, 1).reshape(bs, d, pb)
        cbv_t = cache_b_v[l].transpose(0, 2, 3, 1).reshape(bs, d, pb)
        oa = _attn_a_sample(p3, cak_t, cache_a_v[l].reshape(bs, past * HA, 2 * DA),
                            da, lams, g_row, lam_init)
        ob = _attn_b_sample(p3, cbk_t, cbv_t, db, past)
        pack = ATT_T // t_new if (ATT_T % t_new == 0 and bs % (ATT_T // t_new) == 0) else 1
        xs, conv_s = _tail(xs, oa, ob, p3, cache_conv[l].astype(F32), w, pack * t_new, alpha)
        outs_s[0].append(ka.reshape(bs, t_new, HA, 2, DA))
        outs_s[1].append(va.reshape(bs, t_new, HA, 2 * DA))
        outs_s[2].append(kb.reshape(bs, t_new, HB, DB))
        outs_s[3].append(vb.reshape(bs, t_new, HB, DB))
        outs_s[4].append(conv_s)

    return (xp, xs, *[jnp.stack(o) for o in outs_p], *[jnp.stack(o) for o in outs_s])
```

```python
import functools
import math

import jax
import jax.numpy as jnp
from jax import lax
from jax.experimental import pallas as pl
from jax.experimental.pallas import tpu as pltpu

F32 = jnp.float32
BF16 = jnp.bfloat16

CHUNK = 64
HA = 8
DA = 64
HB = 16
DB = 64
LEFT_CHUNKS = 8
MAX_REL = 128
T5_BUCKETS = 32
T5_MAX_DIST = 128
CONV_W = 3
LN_EPS = 1e-5
RMS_EPS = 1e-5
NEG = -1e30
LOG2E = math.log2(math.e)

LANES = 128
ATT_T = 256
ATT_TA = 512
ATT_B_HEADS = 4
VMEM_LIMIT = 56 * 1024 * 1024


def _cparams(sem, vmem=VMEM_LIMIT):
    return pltpu.CompilerParams(dimension_semantics=sem, vmem_limit_bytes=vmem)


def _toeplitz(vec, t):
    x = jnp.broadcast_to(vec, (t, 2 * t))
    row = lax.broadcasted_iota(jnp.int32, (t, 2 * t), 0)
    shift = 1
    while shift < t:
        x = jnp.where((row & shift) != 0, pltpu.roll(x, shift, 1), x)
        shift *= 2
    return x[:, t:]


def _t5_bucket(rel):
    nb = T5_BUCKETS // 2
    max_exact = nb // 2
    ret = jnp.where(rel > 0, nb, 0)
    n = jnp.abs(rel)
    nf = jnp.maximum(n, 1).astype(F32)
    large = max_exact + (jnp.log(nf / max_exact) / math.log(T5_MAX_DIST / max_exact)
                         * (nb - max_exact)).astype(jnp.int32)
    large = jnp.minimum(large, nb - 1)
    return ret + jnp.where(n < max_exact, n, large)


def _chunk_ids(t):
    kc = lax.broadcasted_iota(jnp.int32, (t, t), 0) // CHUNK
    qc = lax.broadcasted_iota(jnp.int32, (t, t), 1) // CHUNK
    return kc, qc


def _bias_a_kernel(tab_ref, out_ref, *, t):
    h = pl.program_id(0)
    r = lax.broadcasted_iota(jnp.int32, (1, 2 * t), 1)
    far = tab_ref[h * T5_BUCKETS + T5_BUCKETS // 2 - 1]

    def band(rel):
        bucket = _t5_bucket(rel)
        acc = jnp.zeros(rel.shape, F32)
        for j in range(T5_BUCKETS):
            acc = jnp.where(bucket == j, tab_ref[h * T5_BUCKETS + j], acc)
        return (acc - far) * LOG2E

    kc, qc = _chunk_ids(t)
    out_ref[0, 0] = jnp.where(kc <= qc, _toeplitz(band(t - r), t), NEG)
    out_ref[0, 1] = _toeplitz(band(-r), t)


def _bias_b_kernel(band_ref, out_ref, *, t):
    own = band_ref[0]
    far = own[:, 2 * t - 1:]
    r = lax.broadcasted_iota(jnp.int32, (1, 2 * t), 1)
    prev = jnp.where(r < t, pltpu.roll(own, t, 1), far)
    kc, qc = _chunk_ids(t)
    out_ref[0, 0:t, :] = jnp.where(kc - 2 * (t // CHUNK) >= qc - LEFT_CHUNKS, 0.0, NEG)
    out_ref[0, t:2 * t, :] = _toeplitz((prev - far) * LOG2E, t)
    out_ref[0, 2 * t:3 * t, :] = jnp.where(kc <= qc, _toeplitz((own - far) * LOG2E, t), NEG)
    out_ref[0, 3 * t:4 * t, :] = jnp.full((t, t), NEG, F32)


def _bias_tiles(t5_table, rel_table):
    ta, tb = ATT_TA, ATT_T
    assert ta >= T5_MAX_DIST and ta % CHUNK == 0
    assert tb >= MAX_REL and 2 * tb == LEFT_CHUNKS * CHUNK
    smem = pl.BlockSpec(memory_space=pltpu.SMEM)
    da = pl.pallas_call(
        functools.partial(_bias_a_kernel, t=ta),
        grid=(HA,),
        in_specs=[smem],
        out_specs=pl.BlockSpec((1, 2, ta, ta), lambda h: (h, 0, 0, 0)),
        out_shape=jax.ShapeDtypeStruct((HA, 2, ta, ta), F32),
        compiler_params=_cparams(("parallel",)),
        name="bias_a",
    )(t5_table.astype(F32).T.reshape(-1))
    band_b = jnp.pad(rel_table.astype(F32).T[:, ::-1],
                     ((0, 0), (tb - MAX_REL, tb - MAX_REL - 1)), mode="edge").reshape(HB, 1, 2 * tb)
    db = pl.pallas_call(
        functools.partial(_bias_b_kernel, t=tb),
        grid=(HB,),
        in_specs=[pl.BlockSpec((1, 1, 2 * tb), lambda h: (h, 0, 0))],
        out_specs=pl.BlockSpec((1, 4 * tb, tb), lambda h: (h, 0, 0)),
        out_shape=jax.ShapeDtypeStruct((HB, 4 * tb, tb), F32),
        compiler_params=_cparams(("parallel",)),
        name="bias_b",
    )(band_b)
    return da, db


def _proj_kernel(x_ref, w_ref, p_ref, ka_ref, va_ref, kb_ref, vb_ref, *, d, q_scale):
    xb = x_ref[...].astype(BF16)
    f32_outs = {1: ka_ref, 2: va_ref, 4: kb_ref, 5: vb_ref}
    for c in range(8):
        acc = jnp.dot(xb, w_ref[:, c * d:(c + 1) * d], preferred_element_type=F32)
        if c in f32_outs:
            f32_outs[c][...] = acc
        if c in (0, 3):
            acc = acc * q_scale
        p_ref[:, c * d:(c + 1) * d] = acc.astype(BF16)


def _proj_cache_kernel(x_ref, w_ref, p_ref, kat_ref, va_ref, kbt_ref, vbt_ref, *,
                       d, q_scale, tm, first_keep, tiles_per_seq):
    t = pl.program_id(0) % tiles_per_seq
    xb = x_ref[...].astype(BF16)
    for c in range(8):
        acc = jnp.dot(xb, w_ref[:, c * d:(c + 1) * d], preferred_element_type=F32)
        if c == 1:
            kat_ref[0] = acc.T
        elif c == 2:
            for h in range(HA):
                va_ref[pl.ds(h, tm, stride=HA), :] = acc[:, h * LANES:(h + 1) * LANES]
        elif c in (4, 5):
            out = kbt_ref if c == 4 else vbt_ref

            @pl.when(t >= first_keep)
            def _(acc=acc, out=out):
                out[0] = acc.T
        if c in (0, 3):
            acc = acc * q_scale
        p_ref[:, c * d:(c + 1) * d] = acc.astype(BF16)


def _project_prompt(x3, w_bf16, keep, tm):
    b, s, d = x3.shape
    n = w_bf16.shape[1]
    assert n == 8 * d and d == HA * 2 * DA == HB * DB == HA * LANES
    assert s % tm == 0 and keep % tm == 0 and tm % LANES == 0
    tiles_per_seq = s // tm
    first_keep = tiles_per_seq - keep // tm
    m = b * s
    row = lambda i: (i, 0)
    col = lambda i: (i // tiles_per_seq, 0, i % tiles_per_seq)
    keep_col = lambda i: (i // tiles_per_seq, 0, jnp.maximum(i % tiles_per_seq - first_keep, 0))
    return pl.pallas_call(
        functools.partial(_proj_cache_kernel, d=d, q_scale=DA ** -0.5 * LOG2E, tm=tm,
                          first_keep=first_keep, tiles_per_seq=tiles_per_seq),
        grid=(m // tm,),
        in_specs=[pl.BlockSpec((tm, d), row),
                  pl.BlockSpec((d, n), lambda i: (0, 0), pipeline_mode=pl.Buffered(1))],
        out_specs=[pl.BlockSpec((tm, n), row),
                   pl.BlockSpec((1, d, tm), col), pl.BlockSpec((tm * HA, LANES), row),
                   pl.BlockSpec((1, d, tm), keep_col), pl.BlockSpec((1, d, tm), keep_col)],
        out_shape=[jax.ShapeDtypeStruct((m, n), BF16),
                   jax.ShapeDtypeStruct((b, d, s), F32), jax.ShapeDtypeStruct((m * HA, LANES), F32),
                   jax.ShapeDtypeStruct((b, d, keep), F32), jax.ShapeDtypeStruct((b, d, keep), F32)],
        compiler_params=_cparams(("arbitrary",)),
        name="in_proj_prompt",
    )(x3.reshape(m, d), w_bf16)


def _project(x2d, w_bf16, rows_per_seq, keep, tm):
    m, d = x2d.shape
    n = w_bf16.shape[1]
    assert n == 8 * d and d == HA * 2 * DA == HB * DB
    assert m % tm == 0
    row = lambda i: (i, 0)
    n_keep = (m // rows_per_seq) * keep
    if keep == rows_per_seq:
        keep_map = row
    else:
        assert rows_per_seq % tm == 0 and keep % tm == 0
        tiles_per_seq = rows_per_seq // tm
        keep_tiles = keep // tm
        first_keep = tiles_per_seq - keep_tiles

        def keep_map(i):
            return ((i // tiles_per_seq) * keep_tiles
                    + jnp.maximum(i % tiles_per_seq - first_keep, 0), 0)
    return pl.pallas_call(
        functools.partial(_proj_kernel, d=d, q_scale=DA ** -0.5 * LOG2E),
        grid=(m // tm,),
        in_specs=[pl.BlockSpec((tm, d), row),
                  pl.BlockSpec((d, n), lambda i: (0, 0), pipeline_mode=pl.Buffered(1))],
        out_specs=[pl.BlockSpec((tm, n), row),
                   pl.BlockSpec((tm, d), row), pl.BlockSpec((tm, d), row),
                   pl.BlockSpec((tm, d), keep_map), pl.BlockSpec((tm, d), keep_map)],
        out_shape=[jax.ShapeDtypeStruct((m, n), BF16),
                   jax.ShapeDtypeStruct((m, d), F32), jax.ShapeDtypeStruct((m, d), F32),
                   jax.ShapeDtypeStruct((n_keep, d), F32), jax.ShapeDtypeStruct((n_keep, d), F32)],
        compiler_params=_cparams(("arbitrary",)),
        name="in_proj",
    )(x2d, w_bf16)


def _half_masks(shape):
    lane = lax.broadcasted_iota(jnp.int32, shape, len(shape) - 1)
    return lane < (LANES // 2), lane >= (LANES // 2)


def _stack_halves(q):
    lo, hi = _half_masks(q.shape)
    zero = jnp.zeros_like(q)
    return jnp.concatenate([jnp.where(lo, q, zero), jnp.where(hi, q, zero)], axis=0)


def _qk(a, b):
    return lax.dot_general(a, b, (((1,), (1,)), ((), ())), preferred_element_type=F32)


def _transpose_blocks(v_ref, vt_ref, t):
    for j in range(vt_ref.shape[0]):
        vt_ref[j] = v_ref[0, j * t:(j + 1) * t, :].astype(F32).T.astype(vt_ref.dtype)


def _softmax_pv(scores, values, values_t=None):
    values_t = values_t or (False,) * len(values)
    m = functools.reduce(jnp.maximum, [jnp.max(s, axis=1, keepdims=True) for s in scores])
    l = None
    o = None
    for s, v, vt in zip(scores, values, values_t):
        p = jnp.exp2(s - m)
        ls = jnp.sum(p, axis=1, keepdims=True)
        pb = p.astype(BF16)
        ov = _qk(pb, v) if vt else jnp.dot(pb, v, preferred_element_type=F32)
        l = ls if l is None else l + ls
        o = ov if o is None else o + ov
    return o / l


def _lam(lq1, lk1, lq2, lk2, lam_init):
    e1 = jnp.exp(jnp.sum(lq1 * lk1, axis=1, keepdims=True))
    e2 = jnp.exp(jnp.sum(lq2 * lk2, axis=1, keepdims=True))
    return e1 - e2 + lam_init


def _attn_a_kernel(lq1_ref, lk1_ref, lq2_ref, lk2_ref, g_ref, q_ref, k_ref, v_ref, d_ref,
                   o_ref, vt_ref, s_ref, mc_ref, m_ref, l_ref, acc_ref, *, t, lam_init):
    n_tiles = q_ref.shape[1] // t
    q_tile = lambda qi: _stack_halves(q_ref[0, pl.ds(pl.multiple_of(qi * t, t), t), :])
    lam = _lam(lq1_ref[...], lk1_ref[...], lq2_ref[...], lk2_ref[...], lam_init)

    def scores(q, kb, bias):
        start = pl.multiple_of(kb * t, t)
        s = _qk(k_ref[0, pl.ds(start, t), :], q)
        if bias is not None:
            s = s + jnp.concatenate([bias, bias], axis=1)
        s_ref[...] = s
        mc_ref[...] = jnp.max(s, axis=0, keepdims=True)

    def accumulate(kb, q_next, kb_next, bias_next):
        s = s_ref[...]
        m_prev = m_ref[...]
        m_new = jnp.maximum(m_prev, mc_ref[...])
        alpha = jnp.exp2(m_prev - m_new)
        p = jnp.exp2(s - m_new)
        scores(q_next, kb_next, bias_next)
        l_ref[...] = alpha * l_ref[...] + jnp.sum(p, axis=0, keepdims=True)
        pv = jnp.dot(vt_ref[kb], p.astype(BF16), preferred_element_type=F32)
        acc_ref[...] = alpha * acc_ref[...] + pv
        m_ref[...] = m_new

    _transpose_blocks(v_ref, vt_ref, t)
    scores(q_tile(0), 0, d_ref[0, 0])

    def tile(qi, carry):
        q2 = q_tile(qi)
        m_ref[...] = jnp.full(m_ref.shape, NEG, F32)
        l_ref[...] = jnp.zeros(l_ref.shape, F32)
        acc_ref[...] = jnp.zeros(acc_ref.shape, F32)

        n_far = jnp.maximum(qi - 1, 0)

        @pl.when(qi >= 1)
        def _():
            accumulate(qi, q2, qi - 1, d_ref[0, 1])

        def far(i):
            accumulate(jnp.where(i == 0, qi - 1, i - 1), q2, i, None)

        def far_pair(j, c):
            far(2 * j)
            far(2 * j + 1)
            return c

        lax.fori_loop(0, n_far // 2, far_pair, 0)

        @pl.when(n_far % 2 == 1)
        def _():
            far(n_far - 1)
        nxt = jnp.minimum(qi + 1, n_tiles - 1)
        accumulate(jnp.where(qi == 0, 0, jnp.where(n_far == 0, qi - 1, n_far - 1)),
                   q_tile(nxt), nxt, d_ref[0, 0])

        o = acc_ref[...] * (1.0 / l_ref[...])
        o = o[:, :t] - lam * o[:, t:]
        o = o * lax.rsqrt(jnp.mean(o * o, axis=0, keepdims=True) + RMS_EPS)
        o = o * (g_ref[...] * (1.0 - lam_init))
        o_ref[0, pl.ds(pl.multiple_of(qi * t, t), t), :] = o.T.astype(o_ref.dtype)
        return carry

    lax.fori_loop(0, n_tiles, tile, 0)


def _attn_a_prompt(p3, da, lams, g_col, lam_init):
    b, s, n = p3.shape
    t = ATT_TA
    assert s % t == 0 and n == 8 * HA * LANES
    vec = pl.BlockSpec((1, DA), lambda bi, h: (0, 0))
    return pl.pallas_call(
        functools.partial(_attn_a_kernel, t=t, lam_init=lam_init),
        grid=(b, HA),
        in_specs=[vec, vec, vec, vec,
                  pl.BlockSpec((LANES, 1), lambda bi, h: (0, 0)),
                  pl.BlockSpec((1, s, LANES), lambda bi, h: (bi, 0, h)),
                  pl.BlockSpec((1, s, LANES), lambda bi, h: (bi, 0, HA + h)),
                  pl.BlockSpec((1, s, LANES), lambda bi, h: (bi, 0, 2 * HA + h)),
                  pl.BlockSpec((1, 2, t, t), lambda bi, h: (h, 0, 0, 0))],
        out_specs=pl.BlockSpec((1, s, LANES), lambda bi, h: (bi, 0, h)),
        out_shape=jax.ShapeDtypeStruct((b, s, HA * LANES), BF16),
        scratch_shapes=[pltpu.VMEM((s // t, LANES, t), BF16),
                        pltpu.VMEM((t, 2 * t), F32), pltpu.VMEM((1, 2 * t), F32),
                        pltpu.VMEM((1, 2 * t), F32), pltpu.VMEM((1, 2 * t), F32),
                        pltpu.VMEM((LANES, 2 * t), F32)],
        compiler_params=_cparams(("parallel", "parallel")),
        name="attn_a_prompt",
    )(*lams, g_col, p3, p3, p3, da)


def _attn_b_kernel(q_ref, k_ref, v_ref, d_ref, o_ref, vt_ref, s_ref, mc_ref, *, t, heads):
    n_tiles = q_ref.shape[1] // t

    def window(q_tile):
        blocks = [jnp.maximum(q_tile - 2 + j, 0) for j in range(3)]
        bias_rows = [pl.multiple_of(jnp.where(q_tile - 2 + j >= 0, j, 3) * t, t) for j in range(3)]
        return blocks, bias_rows

    def score_part(g, q_tile):
        blocks, bias_rows = window(q_tile)
        pair = slice((g // 2) * LANES, (g // 2 + 1) * LANES)
        q = q_ref[0, pl.ds(pl.multiple_of(q_tile * t, t), t), pair]
        q = jnp.where(_half_masks(q.shape)[g % 2], q, jnp.zeros_like(q))

        def part(j, mc):
            k = k_ref[0, pl.ds(pl.multiple_of(blocks[j] * t, t), t), pair]
            s = _qk(k, q) + d_ref[g, pl.ds(bias_rows[j], t), :]
            s_ref[g, j * t:(j + 1) * t, :] = s
            cm = jnp.max(s, axis=0, keepdims=True)
            return cm if mc is None else jnp.maximum(mc, cm)

        return part

    _transpose_blocks(v_ref, vt_ref, t)
    part0 = score_part(0, 0)
    mc_ref[0] = functools.reduce(lambda mc, j: part0(j, mc), range(3), None)

    def tile(qi, carry):
        blocks, _ = window(qi)

        def finish(g):
            g_next = (g + 1) % heads
            nxt = score_part(g_next, qi if g + 1 < heads else jnp.minimum(qi + 1, n_tiles - 1))
            m = mc_ref[g]
            mc = l = o = None
            for j, kb in enumerate(blocks):
                p = jnp.exp2(s_ref[g, j * t:(j + 1) * t, :] - m)
                mc = nxt(j, mc)
                ls = jnp.sum(p, axis=0, keepdims=True)
                vt = vt_ref[kb, g * DB:(g + 1) * DB, :]
                pv = jnp.dot(vt, p.astype(BF16), preferred_element_type=F32)
                l = ls if l is None else l + ls
                o = pv if o is None else o + pv
            mc_ref[g_next] = mc
            return o * (1.0 / l)

        outs = [finish(g) for g in range(heads)]
        o_ref[0, pl.ds(pl.multiple_of(qi * t, t), t), :] = (
            jnp.concatenate(outs, axis=0).T.astype(o_ref.dtype))
        return carry

    lax.fori_loop(0, n_tiles, tile, 0)


def _attn_b_prompt(p3, db):
    b, s, n = p3.shape
    t = ATT_T
    g = ATT_B_HEADS
    w = g * DB
    assert s % t == 0 and HB % g == 0 and w % LANES == 0
    col0 = 3 * HA * LANES // w
    nblk = HB // g
    return pl.pallas_call(
        functools.partial(_attn_b_kernel, t=t, heads=g),
        grid=(b, nblk),
        in_specs=[pl.BlockSpec((1, s, w), lambda bi, h: (bi, 0, col0 + h)),
                  pl.BlockSpec((1, s, w), lambda bi, h: (bi, 0, col0 + nblk + h)),
                  pl.BlockSpec((1, s, w), lambda bi, h: (bi, 0, col0 + 2 * nblk + h)),
                  pl.BlockSpec((g, 4 * t, t), lambda bi, h: (h, 0, 0))],
        out_specs=pl.BlockSpec((1, s, w), lambda bi, h: (bi, 0, h)),
        out_shape=jax.ShapeDtypeStruct((b, s, HB * DB), BF16),
        scratch_shapes=[pltpu.VMEM((s // t, w, t), BF16),
                        pltpu.VMEM((g, 3 * t, t), F32), pltpu.VMEM((g, 1, t), F32)],
        compiler_params=_cparams(("parallel", "parallel")),
        name="attn_b_prompt",
    )(p3, p3, p3, db)


def _attn_a_sample_kernel(lq1_ref, lk1_ref, lq2_ref, lk2_ref, g_ref, p_ref, kc_ref, vc_ref, d_ref,
                          o_ref, *, t, lam_init):
    tq = p_ref.shape[1]
    past = kc_ref.shape[2]
    lam = _lam(lq1_ref[...], lk1_ref[...], lq2_ref[...], lk2_ref[...], lam_init)
    for h in range(HA):
        head = lambda sec: p_ref[0, :, (sec * HA + h) * LANES:(sec * HA + h + 1) * LANES]
        q2 = _stack_halves(head(0))
        kn, vn = head(1), head(2)
        kct = kc_ref[0, h * LANES:(h + 1) * LANES, :].astype(BF16)
        vc = vc_ref[0, pl.ds(h, past, stride=HA), :].astype(BF16)
        near = d_ref[h, 1].T[:tq]
        new = d_ref[h, 0, :LANES, :].T[:tq, :tq]
        scores = [jnp.dot(q2, kct[:, :past - t], preferred_element_type=F32),
                  jnp.dot(q2, kct[:, past - t:], preferred_element_type=F32)
                  + jnp.concatenate([near, near], axis=0),
                  _qk(q2, kn) + jnp.concatenate([new, new], axis=0)]
        o = _softmax_pv(scores, [vc[:past - t], vc[past - t:], vn])
        o = o[:tq] - lam * o[tq:]
        o = o * lax.rsqrt(jnp.mean(o * o, axis=1, keepdims=True) + RMS_EPS)
        o_ref[0, :, h * LANES:(h + 1) * LANES] = (o * g_ref[...] * (1.0 - lam_init)).astype(o_ref.dtype)


def _attn_a_sample(p3, cache_kt, cache_v, da, lams, g_row, lam_init):
    b, tq, n = p3.shape
    d, past = cache_kt.shape[1:]
    t = ATT_TA
    assert tq <= CHUNK and past % CHUNK == 0 and past > t and tq % 8 == 0
    vec = pl.BlockSpec((1, DA), lambda bi: (0, 0))
    return pl.pallas_call(
        functools.partial(_attn_a_sample_kernel, t=t, lam_init=lam_init),
        grid=(b,),
        in_specs=[vec, vec, vec, vec,
                  pl.BlockSpec((1, LANES), lambda bi: (0, 0)),
                  pl.BlockSpec((1, tq, n), lambda bi: (bi, 0, 0)),
                  pl.BlockSpec((1, d, past), lambda bi: (bi, 0, 0)),
                  pl.BlockSpec((1, past * HA, LANES), lambda bi: (bi, 0, 0)),
                  pl.BlockSpec((HA, 2, t, LANES), lambda bi: (0, 0, 0, 0),
                               pipeline_mode=pl.Buffered(1))],
        out_specs=pl.BlockSpec((1, tq, d), lambda bi: (bi, 0, 0)),
        out_shape=jax.ShapeDtypeStruct((b, tq, d), BF16),
        compiler_params=_cparams(("parallel",)),
        name="attn_a_sample",
    )(*lams, g_row, p3, cache_kt, cache_v, da)


def _attn_b_sample_kernel(p_ref, kc_ref, vc_ref, d_ref, o_ref, *, t):
    tq = p_ref.shape[1]
    past = kc_ref.shape[2]
    pairs = HB // 2
    olo, _ = _half_masks((tq, LANES))
    for hp in range(pairs):
        rows = slice(hp * LANES, (hp + 1) * LANES)
        head = lambda sec: p_ref[0, :, (3 * HA + sec * pairs + hp) * LANES:
                                 (3 * HA + sec * pairs + hp + 1) * LANES]
        q2 = _stack_halves(head(0))
        kct = kc_ref[0, rows, :].astype(BF16)
        vct = vc_ref[0, rows, :].astype(BF16)
        tiles = [d_ref[2 * hp + hh] for hh in range(2)]
        near = jnp.concatenate([d[t:2 * t, :].T[:tq] for d in tiles], axis=0)
        new = jnp.concatenate([d[2 * t:2 * t + LANES, :].T[:tq, :tq] for d in tiles], axis=0)
        scores = [jnp.dot(q2, kct[:, past - t:], preferred_element_type=F32) + near,
                  _qk(q2, head(1)) + new]
        values = [vct[:, past - t:], head(2)]
        values_t = [True, False]
        if past > t:
            scores.insert(0, jnp.dot(q2, kct[:, :past - t], preferred_element_type=F32))
            values.insert(0, vct[:, :past - t])
            values_t.insert(0, True)
        o = _softmax_pv(scores, values, values_t)
        o_ref[0, :, rows] = jnp.where(olo, o[:tq], o[tq:]).astype(o_ref.dtype)


def _attn_b_sample(p3, cache_kt, cache_vt, db, past_len):
    b, tq, n = p3.shape
    d, pb = cache_kt.shape[1:]
    t = ATT_T
    assert tq <= CHUNK and past_len % CHUNK == 0 and pb == LEFT_CHUNKS * CHUNK and pb >= t
    return pl.pallas_call(
        functools.partial(_attn_b_sample_kernel, t=t),
        grid=(b,),
        in_specs=[pl.BlockSpec((1, tq, n), lambda bi: (bi, 0, 0)),
                  pl.BlockSpec((1, d, pb), lambda bi: (bi, 0, 0)),
                  pl.BlockSpec((1, d, pb), lambda bi: (bi, 0, 0)),
                  pl.BlockSpec((HB, 4 * t, LANES), lambda bi: (0, 0, 0),
                               pipeline_mode=pl.Buffered(1))],
        out_specs=pl.BlockSpec((1, tq, d), lambda bi: (bi, 0, 0)),
        out_shape=jax.ShapeDtypeStruct((b, tq, d), BF16),
        compiler_params=_cparams(("parallel",)),
        name="attn_b_sample",
    )(p3, cache_kt, cache_vt, db)


def _layer_norm(x, g, b):
    mu = jnp.mean(x, axis=1, keepdims=True)
    xc = x - mu
    var = jnp.mean(xc * xc, axis=1, keepdims=True)
    return xc * lax.rsqrt(var + LN_EPS) * g + b


def _tail_kernel(x_ref, oa_ref, ob_ref, ga_ref, gb_ref, prev_ref,
                 wpa_ref, wpb_ref, wout_ref, ln1g_ref, ln1b_ref,
                 wup_ref, wgate_ref, cw_ref, cb_ref, wdown_ref, ln2g_ref, ln2b_ref,
                 y_ref, conv_ref, u_ref, *, tm, alpha, seqs):
    ti = pl.program_id(1)
    pad = 8
    hist = CONV_W - 1
    dff = u_ref.shape[1]

    if seqs == 1:
        @pl.when(ti == 0)
        def _():
            u_ref[pad - hist:pad, :] = prev_ref[0]
    else:
        u_ref[pad - hist:pad, :] = jnp.zeros((hist, dff), F32)

    ya = jnp.dot(oa_ref[0], wpa_ref[...], preferred_element_type=F32)
    yb = jnp.dot(ob_ref[0], wpb_ref[...], preferred_element_type=F32)
    merged = (jax.nn.sigmoid(ga_ref[0].astype(F32)) * ya
              + jax.nn.sigmoid(gb_ref[0].astype(F32)) * yb)
    mixed = jnp.dot(merged.astype(BF16), wout_ref[...], preferred_element_type=F32)
    h = _layer_norm(alpha * x_ref[0] + mixed, ln1g_ref[...], ln1b_ref[...])
    hb = h.astype(BF16)
    u = jnp.dot(hb, wup_ref[...], preferred_element_type=F32)
    g = jnp.dot(hb, wgate_ref[...], preferred_element_type=F32)
    u_ref[pad:pad + tm, :] = u
    u2 = u_ref[pad - 2:pad - 2 + tm, :]
    u1 = u_ref[pad - 1:pad - 1 + tm, :]
    if seqs > 1:
        sl = tm // seqs
        pos = lax.broadcasted_iota(jnp.int32, (tm, 1), 0) % sl
        hist_rows = lambda j: jnp.concatenate(
            [jnp.broadcast_to(prev_ref[q, j:j + 1, :], (sl, dff)) for q in range(seqs)], axis=0)
        u1 = jnp.where(pos == 0, hist_rows(1), u1)
        u2 = jnp.where(pos == 0, hist_rows(0), jnp.where(pos == 1, hist_rows(1), u2))
    uc = cb_ref[...] + u2 * cw_ref[0:1, :]
    uc = uc + u1 * cw_ref[1:2, :]
    uc = uc + u * cw_ref[2:3, :]
    if seqs == 1:
        last = u_ref[pad + tm - hist:pad + tm, :]
        conv_ref[0] = last
        u_ref[pad - hist:pad, :] = last
    else:
        for q in range(seqs):
            conv_ref[q] = u_ref[pad + (q + 1) * sl - hist:pad + (q + 1) * sl, :]
    f = jnp.dot((jax.nn.gelu(uc) * g).astype(BF16), wdown_ref[...], preferred_element_type=F32)
    y_ref[0] = _layer_norm(alpha * h + f, ln2g_ref[...], ln2b_ref[...])


def _tail(x3, oa, ob, p3, conv_prev, w, tm, alpha):
    b, s, d = x3.shape
    dff = w["w_up"].shape[1]
    hist = CONV_W - 1
    seqs = max(tm // s, 1)
    assert (s % tm == 0 or tm % s == 0) and b % seqs == 0 and s >= hist
    if seqs > 1:
        fold = lambda a: a.reshape(b // seqs, seqs * s, a.shape[-1])
        x3, oa, ob, p3 = fold(x3), fold(oa), fold(ob), fold(p3)
    nb, rows = x3.shape[:2]
    const = lambda shape: pl.BlockSpec(shape, lambda bi, ti: (0,) * len(shape),
                                       pipeline_mode=pl.Buffered(1))
    act = lambda col: pl.BlockSpec((1, tm, d), lambda bi, ti: (bi, ti, col))
    y, conv = pl.pallas_call(
        functools.partial(_tail_kernel, tm=tm, alpha=alpha, seqs=seqs),
        grid=(nb, rows // tm),
        in_specs=[act(0), act(0), act(0), act(6), act(7),
                  pl.BlockSpec((seqs, hist, dff), lambda bi, ti: (bi, 0, 0)),
                  const((d, d)), const((d, d)), const((d, d)), const((1, d)), const((1, d)),
                  const((d, dff)), const((d, dff)), const((CONV_W, dff)), const((1, dff)),
                  const((dff, d)), const((1, d)), const((1, d))],
        out_specs=[pl.BlockSpec((1, tm, d), lambda bi, ti: (bi, ti, 0)),
                   pl.BlockSpec((seqs, hist, dff), lambda bi, ti: (bi, 0, 0))],
        out_shape=[jax.ShapeDtypeStruct((nb, rows, d), F32),
                   jax.ShapeDtypeStruct((b, hist, dff), F32)],
        scratch_shapes=[pltpu.VMEM((tm + 8, dff), F32)],
        compiler_params=_cparams(("parallel", "arbitrary")),
        name="tail",
    )(x3, oa, ob, p3, p3, conv_prev,
      w["w_pa"], w["w_pb"], w["w_out"], w["ln1_g"], w["ln1_b"],
      w["w_up"], w["w_gate"], w["conv_w"], w["conv_b"], w["w_down"], w["ln2_g"], w["ln2_b"])
    return y.reshape(b, s, d), conv


def kernel(x_prompt, x_sample, cache_a_k, cache_a_v, cache_b_k, cache_b_v, cache_conv, t5_table, w_in, lambda_q1, lambda_k1, lambda_q2, lambda_k2, subln_g, rel_table_b, w_pa, w_pb, w_out, ln1_g, ln1_b, w_up, w_gate, conv_w, conv_b, w_down, ln2_g, ln2_b):
    depth = w_in.shape[0]
    alpha = (2.0 * depth) ** 0.25
    bp, s, d = x_prompt.shape
    bs, t_new, _ = x_sample.shape
    past = cache_a_k.shape[2]
    pb = cache_b_k.shape[2]
    dff = w_up.shape[2]
    keep = min(LEFT_CHUNKS * CHUNK, s)
    assert pb == min(LEFT_CHUNKS * CHUNK, past)

    xp, xs = x_prompt, x_sample
    outs_p = [[] for _ in range(5)]
    outs_s = [[] for _ in range(5)]
    for l in range(depth):
        lam_init = 0.8 - 0.6 * math.exp(-0.3 * l)
        lams = [v[l].astype(F32).reshape(1, DA) for v in (lambda_q1, lambda_k1, lambda_q2, lambda_k2)]
        g_row = subln_g[l].astype(F32).reshape(1, 2 * DA)
        g_col = subln_g[l].astype(F32).reshape(2 * DA, 1)
        w = {
            "w_pa": w_pa[l].astype(BF16), "w_pb": w_pb[l].astype(BF16), "w_out": w_out[l].astype(BF16),
            "ln1_g": ln1_g[l].reshape(1, d), "ln1_b": ln1_b[l].reshape(1, d),
            "w_up": w_up[l].astype(BF16), "w_gate": w_gate[l].astype(BF16),
            "conv_w": conv_w[l], "conv_b": conv_b[l].reshape(1, dff),
            "w_down": w_down[l].astype(BF16),
            "ln2_g": ln2_g[l].reshape(1, d), "ln2_b": ln2_b[l].reshape(1, d),
        }
        w_in_b = w_in[l].astype(BF16)
        da, db = _bias_tiles(t5_table, rel_table_b[l])

        p, kat, va, kbt, vbt = _project_prompt(xp, w_in_b, keep, ATT_T)
        p3 = p.reshape(bp, s, 8 * d)
        oa = _attn_a_prompt(p3, da, lams, g_col, lam_init)
        ob = _attn_b_prompt(p3, db)
        conv0 = jnp.zeros((bp, CONV_W - 1, dff), F32)
        xp, conv_p = _tail(xp, oa, ob, p3, conv0, w, ATT_T, alpha)
        outs_p[0].append(kat.reshape(bp, HA, 2, DA, s).transpose(0, 4, 1, 2, 3))
        outs_p[1].append(va.reshape(bp, s, HA, 2 * DA))
        outs_p[2].append(kbt.reshape(bp, HB, DB, keep).transpose(0, 3, 1, 2))
        outs_p[3].append(vbt.reshape(bp, HB, DB, keep).transpose(0, 3, 1, 2))
        outs_p[4].append(conv_p)

        rows = bs * t_new
        p, ka, va, kb, vb = _project(xs.reshape(rows, d), w_in_b, t_new, t_new,
                                     ATT_T if rows % ATT_T == 0 else t_new)
        p3 = p.reshape(bs, t_new, 8 * d)
        cak_t = cache_a_k[l].transpose(0, 2, 3, 4, 1).reshape(bs, d, past)
        cbk_t = cache_b_k[l].transpose(0, 2, 3, 1).reshape(bs, d, pb)
        cbv_t = cache_b_v[l].transpose(0, 2, 3, 1).reshape(bs, d, pb)
        oa = _attn_a_sample(p3, cak_t, cache_a_v[l].reshape(bs, past * HA, 2 * DA),
                            da, lams, g_row, lam_init)
        ob = _attn_b_sample(p3, cbk_t, cbv_t, db, past)
        pack = ATT_T // t_new if (ATT_T % t_new == 0 and bs % (ATT_T // t_new) == 0) else 1
        xs, conv_s = _tail(xs, oa, ob, p3, cache_conv[l].astype(F32), w, pack * t_new, alpha)
        outs_s[0].append(ka.reshape(bs, t_new, HA, 2, DA))
        outs_s[1].append(va.reshape(bs, t_new, HA, 2 * DA))
        outs_s[2].append(kb.reshape(bs, t_new, HB, DB))
        outs_s[3].append(vb.reshape(bs, t_new, HB, DB))
        outs_s[4].append(conv_s)

    return (xp, xs, *[jnp.stack(o) for o in outs_p], *[jnp.stack(o) for o in outs_s])
```

```python
import functools
import math

import jax
import jax.numpy as jnp
from jax import lax
from jax.experimental import pallas as pl
from jax.experimental.pallas import tpu as pltpu

F32 = jnp.float32
BF16 = jnp.bfloat16

CHUNK = 64
HA = 8
DA = 64
HB = 16
DB = 64
LEFT_CHUNKS = 8
MAX_REL = 128
T5_BUCKETS = 32
T5_MAX_DIST = 128
CONV_W = 3
LN_EPS = 1e-5
RMS_EPS = 1e-5
NEG = -1e30
LOG2E = math.log2(math.e)

LANES = 128
ATT_T = 256
ATT_TA = 512
ATT_B_HEADS = 4
ONES_ROWS = 16
VMEM_LIMIT = 56 * 1024 * 1024


def _cparams(sem, vmem=VMEM_LIMIT):
    return pltpu.CompilerParams(dimension_semantics=sem, vmem_limit_bytes=vmem)


def _toeplitz(vec, t):
    x = jnp.broadcast_to(vec, (t, 2 * t))
    row = lax.broadcasted_iota(jnp.int32, (t, 2 * t), 0)
    shift = 1
    while shift < t:
        x = jnp.where((row & shift) != 0, pltpu.roll(x, shift, 1), x)
        shift *= 2
    return x[:, t:]


def _t5_bucket(rel):
    nb = T5_BUCKETS // 2
    max_exact = nb // 2
    ret = jnp.where(rel > 0, nb, 0)
    n = jnp.abs(rel)
    nf = jnp.maximum(n, 1).astype(F32)
    large = max_exact + (jnp.log(nf / max_exact) / math.log(T5_MAX_DIST / max_exact)
                         * (nb - max_exact)).astype(jnp.int32)
    large = jnp.minimum(large, nb - 1)
    return (ret + jnp.where(n < max_exact, n, large)) & (T5_BUCKETS - 1)


def _chunk_ids(t):
    kc = lax.broadcasted_iota(jnp.int32, (t, t), 0) // CHUNK
    qc = lax.broadcasted_iota(jnp.int32, (t, t), 1) // CHUNK
    return kc, qc


def _bias_a_kernel(tab_ref, bown_ref, bcorner_ref, own_ref, corner_ref, *, t):
    h = pl.program_id(0)
    c = T5_MAX_DIST
    far = tab_ref[h * T5_BUCKETS + T5_BUCKETS // 2 - 1]

    def band(bucket):
        acc = jnp.zeros(bucket.shape, F32)
        for j in range(T5_BUCKETS):
            acc = jnp.where(bucket == j, tab_ref[h * T5_BUCKETS + j], acc)
        return (acc - far) * LOG2E

    kc, qc = _chunk_ids(t)
    own_ref[0] = jnp.where(kc <= qc, _toeplitz(band(bown_ref[...]), t), NEG)
    corner_ref[0] = _toeplitz(band(bcorner_ref[...]), c)


def _bias_b_kernel(band_ref, out_ref, *, t):
    own = band_ref[0]
    far = own[:, 2 * t - 1:]
    r = lax.broadcasted_iota(jnp.int32, (1, 2 * t), 1)
    prev = jnp.where(r < t, pltpu.roll(own, t, 1), far)
    kc, qc = _chunk_ids(t)
    out_ref[0, 0:t, :] = jnp.where(kc - 2 * (t // CHUNK) >= qc - LEFT_CHUNKS, 0.0, NEG)
    out_ref[0, t:2 * t, :] = _toeplitz((prev - far) * LOG2E, t)
    out_ref[0, 2 * t:3 * t, :] = jnp.where(kc <= qc, _toeplitz((own - far) * LOG2E, t), NEG)
    out_ref[0, 3 * t:4 * t, :] = jnp.full((t, t), NEG, F32)


def _bias_tiles(t5_table, rel_table):
    ta, tb = ATT_TA, ATT_T
    assert ta >= T5_MAX_DIST and ta % CHUNK == 0
    assert tb >= MAX_REL and 2 * tb == LEFT_CHUNKS * CHUNK
    smem = pl.BlockSpec(memory_space=pltpu.SMEM)
    c = T5_MAX_DIST
    bucket_own = _t5_bucket(ta - jnp.arange(2 * ta, dtype=jnp.int32)).reshape(1, 2 * ta)
    bucket_corner = _t5_bucket(-jnp.arange(2 * c, dtype=jnp.int32)).reshape(1, 2 * c)
    da = pl.pallas_call(
        functools.partial(_bias_a_kernel, t=ta),
        grid=(HA,),
        in_specs=[smem, pl.BlockSpec((1, 2 * ta), lambda h: (0, 0)),
                  pl.BlockSpec((1, 2 * c), lambda h: (0, 0))],
        out_specs=[pl.BlockSpec((1, ta, ta), lambda h: (h, 0, 0)),
                   pl.BlockSpec((1, c, c), lambda h: (h, 0, 0))],
        out_shape=[jax.ShapeDtypeStruct((HA, ta, ta), F32),
                   jax.ShapeDtypeStruct((HA, c, c), F32)],
        compiler_params=_cparams(("parallel",)),
        name="bias_a",
    )(t5_table.astype(F32).T.reshape(-1), bucket_own, bucket_corner)
    band_b = jnp.pad(rel_table.astype(F32).T[:, ::-1],
                     ((0, 0), (tb - MAX_REL, tb - MAX_REL - 1)), mode="edge").reshape(HB, 1, 2 * tb)
    db = pl.pallas_call(
        functools.partial(_bias_b_kernel, t=tb),
        grid=(HB,),
        in_specs=[pl.BlockSpec((1, 1, 2 * tb), lambda h: (h, 0, 0))],
        out_specs=pl.BlockSpec((1, 4 * tb, tb), lambda h: (h, 0, 0)),
        out_shape=jax.ShapeDtypeStruct((HB, 4 * tb, tb), F32),
        compiler_params=_cparams(("parallel",)),
        name="bias_b",
    )(band_b)
    return da, db


def _proj_kernel(x_ref, w_ref, p_ref, ka_ref, va_ref, kb_ref, vb_ref, *, d, q_scale):
    xb = x_ref[...].astype(BF16)
    f32_outs = {1: ka_ref, 2: va_ref, 4: kb_ref, 5: vb_ref}
    for c in range(8):
        acc = jnp.dot(xb, w_ref[:, c * d:(c + 1) * d], preferred_element_type=F32)
        if c in f32_outs:
            f32_outs[c][...] = acc
        if c in (0, 3):
            acc = acc * q_scale
        p_ref[:, c * d:(c + 1) * d] = acc.astype(BF16)


def _proj_cache_kernel(x_ref, w_ref, p_ref, kat_ref, va_ref, kbt_ref, vbt_ref, *,
                       d, q_scale, tm, first_keep, tiles_per_seq):
    t = pl.program_id(0) % tiles_per_seq
    xb = x_ref[...].astype(BF16)
    kept = {}
    for c in range(8):
        acc = jnp.dot(xb, w_ref[:, c * d:(c + 1) * d], preferred_element_type=F32)
        if c == 1:
            kat_ref[0] = acc.T
        elif c == 2:
            for h in range(HA):
                va_ref[pl.ds(h, tm, stride=HA), :] = acc[:, h * LANES:(h + 1) * LANES]
        elif c in (4, 5):
            kept[c] = acc
        if c in (0, 3):
            acc = acc * q_scale
        p_ref[:, c * d:(c + 1) * d] = acc.astype(BF16)

    @pl.when(t >= first_keep)
    def _():
        kbt_ref[0] = kept[4].T
        vbt_ref[0] = kept[5].T


def _project_prompt(x3, w_bf16, keep, tm):
    b, s, d = x3.shape
    n = w_bf16.shape[1]
    assert n == 8 * d and d == HA * 2 * DA == HB * DB == HA * LANES
    assert s % tm == 0 and keep % tm == 0 and tm % LANES == 0
    tiles_per_seq = s // tm
    first_keep = tiles_per_seq - keep // tm
    m = b * s
    row = lambda i: (i, 0)
    col = lambda i: (i // tiles_per_seq, 0, i % tiles_per_seq)
    keep_col = lambda i: (i // tiles_per_seq, 0, jnp.maximum(i % tiles_per_seq - first_keep, 0))
    return pl.pallas_call(
        functools.partial(_proj_cache_kernel, d=d, q_scale=DA ** -0.5 * LOG2E, tm=tm,
                          first_keep=first_keep, tiles_per_seq=tiles_per_seq),
        grid=(m // tm,),
        in_specs=[pl.BlockSpec((tm, d), row),
                  pl.BlockSpec((d, n), lambda i: (0, 0), pipeline_mode=pl.Buffered(1))],
        out_specs=[pl.BlockSpec((tm, n), row),
                   pl.BlockSpec((1, d, tm), col), pl.BlockSpec((tm * HA, LANES), row),
                   pl.BlockSpec((1, d, tm), keep_col), pl.BlockSpec((1, d, tm), keep_col)],
        out_shape=[jax.ShapeDtypeStruct((m, n), BF16),
                   jax.ShapeDtypeStruct((b, d, s), F32), jax.ShapeDtypeStruct((m * HA, LANES), F32),
                   jax.ShapeDtypeStruct((b, d, keep), F32), jax.ShapeDtypeStruct((b, d, keep), F32)],
        compiler_params=_cparams(("arbitrary",)),
        name="in_proj_prompt",
    )(x3.reshape(m, d), w_bf16)


def _project(x2d, w_bf16, rows_per_seq, keep, tm):
    m, d = x2d.shape
    n = w_bf16.shape[1]
    assert n == 8 * d and d == HA * 2 * DA == HB * DB
    assert m % tm == 0
    row = lambda i: (i, 0)
    n_keep = (m // rows_per_seq) * keep
    if keep == rows_per_seq:
        keep_map = row
    else:
        assert rows_per_seq % tm == 0 and keep % tm == 0
        tiles_per_seq = rows_per_seq // tm
        keep_tiles = keep // tm
        first_keep = tiles_per_seq - keep_tiles

        def keep_map(i):
            return ((i // tiles_per_seq) * keep_tiles
                    + jnp.maximum(i % tiles_per_seq - first_keep, 0), 0)
    return pl.pallas_call(
        functools.partial(_proj_kernel, d=d, q_scale=DA ** -0.5 * LOG2E),
        grid=(m // tm,),
        in_specs=[pl.BlockSpec((tm, d), row),
                  pl.BlockSpec((d, n), lambda i: (0, 0), pipeline_mode=pl.Buffered(1))],
        out_specs=[pl.BlockSpec((tm, n), row),
                   pl.BlockSpec((tm, d), row), pl.BlockSpec((tm, d), row),
                   pl.BlockSpec((tm, d), keep_map), pl.BlockSpec((tm, d), keep_map)],
        out_shape=[jax.ShapeDtypeStruct((m, n), BF16),
                   jax.ShapeDtypeStruct((m, d), F32), jax.ShapeDtypeStruct((m, d), F32),
                   jax.ShapeDtypeStruct((n_keep, d), F32), jax.ShapeDtypeStruct((n_keep, d), F32)],
        compiler_params=_cparams(("arbitrary",)),
        name="in_proj",
    )(x2d, w_bf16)


def _half_masks(shape):
    lane = lax.broadcasted_iota(jnp.int32, shape, len(shape) - 1)
    return lane < (LANES // 2), lane >= (LANES // 2)


def _stack_halves(q):
    lo, hi = _half_masks(q.shape)
    zero = jnp.zeros_like(q)
    return jnp.concatenate([jnp.where(lo, q, zero), jnp.where(hi, q, zero)], axis=0)


def _qk(a, b):
    return lax.dot_general(a, b, (((1,), (1,)), ((), ())), preferred_element_type=F32)


def _transpose_blocks(v_ref, vt_ref, t):
    for j in range(vt_ref.shape[0]):
        vt_ref[j] = v_ref[0, j * t:(j + 1) * t, :].astype(F32).T.astype(vt_ref.dtype)


def _softmax_pv(scores, values, values_t=None):
    values_t = values_t or (False,) * len(values)
    m = functools.reduce(jnp.maximum, [jnp.max(s, axis=1, keepdims=True) for s in scores])
    l = None
    o = None
    for s, v, vt in zip(scores, values, values_t):
        p = jnp.exp2(s - m)
        ls = jnp.sum(p, axis=1, keepdims=True)
        pb = p.astype(BF16)
        ov = _qk(pb, v) if vt else jnp.dot(pb, v, preferred_element_type=F32)
        l = ls if l is None else l + ls
        o = ov if o is None else o + ov
    return o / l


def _lam(lq1, lk1, lq2, lk2, lam_init):
    e1 = jnp.exp(jnp.sum(lq1 * lk1, axis=1, keepdims=True))
    e2 = jnp.exp(jnp.sum(lq2 * lk2, axis=1, keepdims=True))
    return e1 - e2 + lam_init


def _attn_a_kernel(lq1_ref, lk1_ref, lq2_ref, lk2_ref, g_ref, q_ref, k_ref, v_ref, d_ref, dc_ref,
                   o_ref, vt_ref, s_ref, mc_ref, m_ref, acc_ref, *, t, lam_init):
    n_tiles = q_ref.shape[1] // t
    c = dc_ref.shape[1]
    q_tile = lambda qi: _stack_halves(q_ref[0, pl.ds(pl.multiple_of(qi * t, t), t), :])
    lam = _lam(lq1_ref[...], lk1_ref[...], lq2_ref[...], lk2_ref[...], lam_init)

    def scores(q, kb, bias=None):
        start = pl.multiple_of(kb * t, t)
        s = _qk(k_ref[0, pl.ds(start, t), :], q)
        if bias is not None:
            s = s + jnp.concatenate([bias, bias], axis=1)
        s_ref[...] = s
        mc_ref[...] = jnp.max(s, axis=0, keepdims=True)

    def scores_prev(q, kb):
        start = pl.multiple_of(kb * t, t)
        top = _qk(k_ref[0, pl.ds(start, t - c), :], q)
        bot = _qk(k_ref[0, pl.ds(pl.multiple_of(start + (t - c), c), c), :], q)
        corner = dc_ref[0]
        bot = jnp.concatenate([bot[:, :c] + corner, bot[:, c:t],
                               bot[:, t:t + c] + corner, bot[:, t + c:]], axis=1)
        s_ref[:t - c, :] = top
        s_ref[t - c:, :] = bot
        mc_ref[...] = jnp.maximum(jnp.max(top, axis=0, keepdims=True),
                                  jnp.max(bot, axis=0, keepdims=True))

    def accumulate(kb, refill):
        s = s_ref[...]
        m_prev = m_ref[...]
        m_new = jnp.maximum(m_prev, mc_ref[...])
        alpha = jnp.exp2(m_prev - m_new)
        p = jnp.exp2(s - m_new)
        refill()
        pv = jnp.dot(vt_ref[kb], p.astype(BF16), preferred_element_type=F32)
        acc_ref[...] = alpha * acc_ref[...] + pv
        m_ref[...] = m_new

    for j in range(n_tiles):
        vt_ref[j, :LANES, :] = v_ref[0, j * t:(j + 1) * t, :].astype(F32).T.astype(BF16)
        vt_ref[j, LANES:, :] = jnp.ones((ONES_ROWS, t), BF16)
    scores(q_tile(0), 0, d_ref[0])

    def tile(qi, carry):
        q2 = q_tile(qi)
        m_ref[...] = jnp.full(m_ref.shape, NEG, F32)
        acc_ref[...] = jnp.zeros(acc_ref.shape, F32)

        n_far = jnp.maximum(qi - 1, 0)

        @pl.when(qi >= 1)
        def _():
            accumulate(qi, lambda: scores_prev(q2, qi - 1))

        def far(i):
            accumulate(jnp.where(i == 0, qi - 1, i - 1), lambda: scores(q2, i))

        def far_pair(j, c):
            far(2 * j)
            far(2 * j + 1)
            return c

        lax.fori_loop(0, n_far // 2, far_pair, 0)

        @pl.when(n_far % 2 == 1)
        def _():
            far(n_far - 1)
        nxt = jnp.minimum(qi + 1, n_tiles - 1)
        accumulate(jnp.where(qi == 0, 0, jnp.where(n_far == 0, qi - 1, n_far - 1)),
                   lambda: scores(q_tile(nxt), nxt, d_ref[0]))

        o = acc_ref[:LANES, :] * (1.0 / acc_ref[LANES:LANES + 1, :])
        o = o[:, :t] - lam * o[:, t:]
        o = o * lax.rsqrt(jnp.mean(o * o, axis=0, keepdims=True) + RMS_EPS)
        o = o * (g_ref[...] * (1.0 - lam_init))
        o_ref[0, pl.ds(pl.multiple_of(qi * t, t), t), :] = o.T.astype(o_ref.dtype)
        return carry

    lax.fori_loop(0, n_tiles, tile, 0)


def _attn_a_prompt(p3, da, lams, g_col, lam_init):
    b, s, n = p3.shape
    t = ATT_TA
    d_own, d_corner = da
    c = d_corner.shape[1]
    assert s % t == 0 and n == 8 * HA * LANES and t > c
    vec = pl.BlockSpec((1, DA), lambda bi, h: (0, 0))
    return pl.pallas_call(
        functools.partial(_attn_a_kernel, t=t, lam_init=lam_init),
        grid=(b, HA),
        in_specs=[vec, vec, vec, vec,
                  pl.BlockSpec((LANES, 1), lambda bi, h: (0, 0)),
                  pl.BlockSpec((1, s, LANES), lambda bi, h: (bi, 0, h)),
                  pl.BlockSpec((1, s, LANES), lambda bi, h: (bi, 0, HA + h)),
                  pl.BlockSpec((1, s, LANES), lambda bi, h: (bi, 0, 2 * HA + h)),
                  pl.BlockSpec((1, t, t), lambda bi, h: (h, 0, 0)),
                  pl.BlockSpec((1, c, c), lambda bi, h: (h, 0, 0))],
        out_specs=pl.BlockSpec((1, s, LANES), lambda bi, h: (bi, 0, h)),
        out_shape=jax.ShapeDtypeStruct((b, s, HA * LANES), BF16),
        scratch_shapes=[pltpu.VMEM((s // t, LANES + ONES_ROWS, t), BF16),
                        pltpu.VMEM((t, 2 * t), F32), pltpu.VMEM((1, 2 * t), F32),
                        pltpu.VMEM((1, 2 * t), F32),
                        pltpu.VMEM((LANES + ONES_ROWS, 2 * t), F32)],
        compiler_params=_cparams(("parallel", "parallel")),
        name="attn_a_prompt",
    )(*lams, g_col, p3, p3, p3, d_own, d_corner)


def _attn_b_kernel(q_ref, k_ref, v_ref, d_ref, o_ref, vt_ref, s_ref, mc_ref, *, t, heads):
    n_tiles = q_ref.shape[1] // t

    def window(q_tile):
        blocks = [jnp.maximum(q_tile - 2 + j, 0) for j in range(3)]
        bias_rows = [pl.multiple_of(jnp.where(q_tile - 2 + j >= 0, j, 3) * t, t) for j in range(3)]
        return blocks, bias_rows

    def score_part(g, q_tile):
        blocks, bias_rows = window(q_tile)
        pair = slice((g // 2) * LANES, (g // 2 + 1) * LANES)
        q = q_ref[0, pl.ds(pl.multiple_of(q_tile * t, t), t), pair]
        q = jnp.where(_half_masks(q.shape)[g % 2], q, jnp.zeros_like(q))

        def part(j, mc):
            k = k_ref[0, pl.ds(pl.multiple_of(blocks[j] * t, t), t), pair]
            s = _qk(k, q) + d_ref[g, pl.ds(bias_rows[j], t), :]
            s_ref[g, j * t:(j + 1) * t, :] = s
            cm = jnp.max(s, axis=0, keepdims=True)
            return cm if mc is None else jnp.maximum(mc, cm)

        return part

    _transpose_blocks(v_ref, vt_ref, t)
    part0 = score_part(0, 0)
    mc_ref[0] = functools.reduce(lambda mc, j: part0(j, mc), range(3), None)

    def tile(qi, carry):
        blocks, _ = window(qi)

        def finish(g):
            g_next = (g + 1) % heads
            nxt = score_part(g_next, qi if g + 1 < heads else jnp.minimum(qi + 1, n_tiles - 1))
            m = mc_ref[g]
            mc = l = o = None
            for j, kb in enumerate(blocks):
                p = jnp.exp2(s_ref[g, j * t:(j + 1) * t, :] - m)
                mc = nxt(j, mc)
                ls = jnp.sum(p, axis=0, keepdims=True)
                vt = vt_ref[kb, g * DB:(g + 1) * DB, :]
                pv = jnp.dot(vt, p.astype(BF16), preferred_element_type=F32)
                l = ls if l is None else l + ls
                o = pv if o is None else o + pv
            mc_ref[g_next] = mc
            return o * (1.0 / l)

        outs = [finish(g) for g in range(heads)]
        o_ref[0, pl.ds(pl.multiple_of(qi * t, t), t), :] = (
            jnp.concatenate(outs, axis=0).T.astype(o_ref.dtype))
        return carry

    lax.fori_loop(0, n_tiles, tile, 0)


def _attn_b_prompt(p3, db):
    b, s, n = p3.shape
    t = ATT_T
    g = ATT_B_HEADS
    w = g * DB
    assert s % t == 0 and HB % g == 0 and w % LANES == 0
    col0 = 3 * HA * LANES // w
    nblk = HB // g
    return pl.pallas_call(
        functools.partial(_attn_b_kernel, t=t, heads=g),
        grid=(b, nblk),
        in_specs=[pl.BlockSpec((1, s, w), lambda bi, h: (bi, 0, col0 + h)),
                  pl.BlockSpec((1, s, w), lambda bi, h: (bi, 0, col0 + nblk + h)),
                  pl.BlockSpec((1, s, w), lambda bi, h: (bi, 0, col0 + 2 * nblk + h)),
                  pl.BlockSpec((g, 4 * t, t), lambda bi, h: (h, 0, 0))],
        out_specs=pl.BlockSpec((1, s, w), lambda bi, h: (bi, 0, h)),
        out_shape=jax.ShapeDtypeStruct((b, s, HB * DB), BF16),
        scratch_shapes=[pltpu.VMEM((s // t, w, t), BF16),
                        pltpu.VMEM((g, 3 * t, t), F32), pltpu.VMEM((g, 1, t), F32)],
        compiler_params=_cparams(("parallel", "parallel")),
        name="attn_b_prompt",
    )(p3, p3, p3, db)


def _attn_a_sample_kernel(lq1_ref, lk1_ref, lq2_ref, lk2_ref, g_ref, p_ref, kc_ref, vc_ref, d_ref,
                          dc_ref, o_ref, *, lam_init):
    tq = p_ref.shape[1]
    past = kc_ref.shape[2]
    c = dc_ref.shape[1]
    lam = _lam(lq1_ref[...], lk1_ref[...], lq2_ref[...], lk2_ref[...], lam_init)
    for h in range(HA):
        head = lambda sec: p_ref[0, :, (sec * HA + h) * LANES:(sec * HA + h + 1) * LANES]
        q2 = _stack_halves(head(0))
        kn, vn = head(1), head(2)
        kct = kc_ref[0, h * LANES:(h + 1) * LANES, :].astype(BF16)
        vc = vc_ref[0, pl.ds(h, past, stride=HA), :].astype(BF16)
        near = dc_ref[h].T[:tq]
        new = d_ref[h].T[:tq, :tq]
        scores = [jnp.dot(q2, kct[:, :past - c], preferred_element_type=F32),
                  jnp.dot(q2, kct[:, past - c:], preferred_element_type=F32)
                  + jnp.concatenate([near, near], axis=0),
                  _qk(q2, kn) + jnp.concatenate([new, new], axis=0)]
        o = _softmax_pv(scores, [vc[:past - c], vc[past - c:], vn])
        o = o[:tq] - lam * o[tq:]
        o = o * lax.rsqrt(jnp.mean(o * o, axis=1, keepdims=True) + RMS_EPS)
        o_ref[0, :, h * LANES:(h + 1) * LANES] = (o * g_ref[...] * (1.0 - lam_init)).astype(o_ref.dtype)


def _attn_a_sample(p3, cache_kt, cache_v, da, lams, g_row, lam_init):
    b, tq, n = p3.shape
    d, past = cache_kt.shape[1:]
    d_own, d_corner = da
    c = d_corner.shape[1]
    assert tq <= CHUNK and past % CHUNK == 0 and past > c and tq % 8 == 0 and c == LANES
    vec = pl.BlockSpec((1, DA), lambda bi: (0, 0))
    whole = lambda bi: (0, 0, 0)
    return pl.pallas_call(
        functools.partial(_attn_a_sample_kernel, lam_init=lam_init),
        grid=(b,),
        in_specs=[vec, vec, vec, vec,
                  pl.BlockSpec((1, LANES), lambda bi: (0, 0)),
                  pl.BlockSpec((1, tq, n), lambda bi: (bi, 0, 0)),
                  pl.BlockSpec((1, d, past), lambda bi: (bi, 0, 0)),
                  pl.BlockSpec((1, past * HA, LANES), lambda bi: (bi, 0, 0)),
                  pl.BlockSpec((HA, LANES, LANES), whole, pipeline_mode=pl.Buffered(1)),
                  pl.BlockSpec((HA, c, c), whole, pipeline_mode=pl.Buffered(1))],
        out_specs=pl.BlockSpec((1, tq, d), lambda bi: (bi, 0, 0)),
        out_shape=jax.ShapeDtypeStruct((b, tq, d), BF16),
        compiler_params=_cparams(("parallel",)),
        name="attn_a_sample",
    )(*lams, g_row, p3, cache_kt, cache_v, d_own, d_corner)


def _attn_b_sample_kernel(p_ref, kc_ref, vc_ref, d_ref, o_ref, *, t):
    tq = p_ref.shape[1]
    past = kc_ref.shape[2]
    pairs = HB // 2
    olo, _ = _half_masks((tq, LANES))
    for hp in range(pairs):
        rows = slice(hp * LANES, (hp + 1) * LANES)
        head = lambda sec: p_ref[0, :, (3 * HA + sec * pairs + hp) * LANES:
                                 (3 * HA + sec * pairs + hp + 1) * LANES]
        q2 = _stack_halves(head(0))
        kct = kc_ref[0, rows, :].astype(BF16)
        vct = vc_ref[0, rows, :].astype(BF16)
        tiles = [d_ref[2 * hp + hh] for hh in range(2)]
        near = jnp.concatenate([d[t:2 * t, :].T[:tq] for d in tiles], axis=0)
        new = jnp.concatenate([d[2 * t:2 * t + LANES, :].T[:tq, :tq] for d in tiles], axis=0)
        scores = [jnp.dot(q2, kct[:, past - t:], preferred_element_type=F32) + near,
                  _qk(q2, head(1)) + new]
        values = [vct[:, past - t:], head(2)]
        values_t = [True, False]
        if past > t:
            scores.insert(0, jnp.dot(q2, kct[:, :past - t], preferred_element_type=F32))
            values.insert(0, vct[:, :past - t])
            values_t.insert(0, True)
        o = _softmax_pv(scores, values, values_t)
        o_ref[0, :, rows] = jnp.where(olo, o[:tq], o[tq:]).astype(o_ref.dtype)


def _attn_b_sample(p3, cache_kt, cache_vt, db, past_len):
    b, tq, n = p3.shape
    d, pb = cache_kt.shape[1:]
    t = ATT_T
    assert tq <= CHUNK and past_len % CHUNK == 0 and pb == LEFT_CHUNKS * CHUNK and pb >= t
    return pl.pallas_call(
        functools.partial(_attn_b_sample_kernel, t=t),
        grid=(b,),
        in_specs=[pl.BlockSpec((1, tq, n), lambda bi: (bi, 0, 0)),
                  pl.BlockSpec((1, d, pb), lambda bi: (bi, 0, 0)),
                  pl.BlockSpec((1, d, pb), lambda bi: (bi, 0, 0)),
                  pl.BlockSpec((HB, 4 * t, LANES), lambda bi: (0, 0, 0),
                               pipeline_mode=pl.Buffered(1))],
        out_specs=pl.BlockSpec((1, tq, d), lambda bi: (bi, 0, 0)),
        out_shape=jax.ShapeDtypeStruct((b, tq, d), BF16),
        compiler_params=_cparams(("parallel",)),
        name="attn_b_sample",
    )(p3, cache_kt, cache_vt, db)


def _layer_norm(x, g, b):
    mu = jnp.mean(x, axis=1, keepdims=True)
    xc = x - mu
    var = jnp.mean(xc * xc, axis=1, keepdims=True)
    return xc * lax.rsqrt(var + LN_EPS) * g + b


def _tail_kernel(x_ref, oa_ref, ob_ref, ga_ref, gb_ref, prev_ref,
                 wpa_ref, wpb_ref, wout_ref, ln1g_ref, ln1b_ref,
                 wup_ref, wgate_ref, cw_ref, cb_ref, wdown_ref, ln2g_ref, ln2b_ref,
                 y_ref, conv_ref, u_ref, *, tm, alpha, seqs):
    ti = pl.program_id(1)
    pad = 8
    hist = CONV_W - 1
    dff = u_ref.shape[1]

    if seqs == 1:
        @pl.when(ti == 0)
        def _():
            u_ref[pad - hist:pad, :] = prev_ref[0]
    else:
        u_ref[pad - hist:pad, :] = jnp.zeros((hist, dff), F32)

    ya = jnp.dot(oa_ref[0], wpa_ref[...], preferred_element_type=F32)
    yb = jnp.dot(ob_ref[0], wpb_ref[...], preferred_element_type=F32)
    merged = (jax.nn.sigmoid(ga_ref[0].astype(F32)) * ya
              + jax.nn.sigmoid(gb_ref[0].astype(F32)) * yb)
    mixed = jnp.dot(merged.astype(BF16), wout_ref[...], preferred_element_type=F32)
    h = _layer_norm(alpha * x_ref[0] + mixed, ln1g_ref[...], ln1b_ref[...])
    hb = h.astype(BF16)
    u = jnp.dot(hb, wup_ref[...], preferred_element_type=F32)
    g = jnp.dot(hb, wgate_ref[...], preferred_element_type=F32)
    u_ref[pad:pad + tm, :] = u
    u2 = u_ref[pad - 2:pad - 2 + tm, :]
    u1 = u_ref[pad - 1:pad - 1 + tm, :]
    if seqs > 1:
        sl = tm // seqs
        pos = lax.broadcasted_iota(jnp.int32, (tm, 1), 0) % sl
        hist_rows = lambda j: jnp.concatenate(
            [jnp.broadcast_to(prev_ref[q, j:j + 1, :], (sl, dff)) for q in range(seqs)], axis=0)
        u1 = jnp.where(pos == 0, hist_rows(1), u1)
        u2 = jnp.where(pos == 0, hist_rows(0), jnp.where(pos == 1, hist_rows(1), u2))
    uc = cb_ref[...] + u2 * cw_ref[0:1, :]
    uc = uc + u1 * cw_ref[1:2, :]
    uc = uc + u * cw_ref[2:3, :]
    if seqs == 1:
        last = u_ref[pad + tm - hist:pad + tm, :]
        conv_ref[0] = last
        u_ref[pad - hist:pad, :] = last
    else:
        for q in range(seqs):
            conv_ref[q] = u_ref[pad + (q + 1) * sl - hist:pad + (q + 1) * sl, :]
    f = jnp.dot((jax.nn.gelu(uc) * g).astype(BF16), wdown_ref[...], preferred_element_type=F32)
    y_ref[0] = _layer_norm(alpha * h + f, ln2g_ref[...], ln2b_ref[...])


def _tail(x3, oa, ob, p3, conv_prev, w, tm, alpha):
    b, s, d = x3.shape
    dff = w["w_up"].shape[1]
    hist = CONV_W - 1
    seqs = max(tm // s, 1)
    assert (s % tm == 0 or tm % s == 0) and b % seqs == 0 and s >= hist
    if seqs > 1:
        fold = lambda a: a.reshape(b // seqs, seqs * s, a.shape[-1])
        x3, oa, ob, p3 = fold(x3), fold(oa), fold(ob), fold(p3)
    nb, rows = x3.shape[:2]
    const = lambda shape: pl.BlockSpec(shape, lambda bi, ti: (0,) * len(shape),
                                       pipeline_mode=pl.Buffered(1))
    act = lambda col: pl.BlockSpec((1, tm, d), lambda bi, ti: (bi, ti, col))
    y, conv = pl.pallas_call(
        functools.partial(_tail_kernel, tm=tm, alpha=alpha, seqs=seqs),
        grid=(nb, rows // tm),
        in_specs=[act(0), act(0), act(0), act(6), act(7),
                  pl.BlockSpec((seqs, hist, dff), lambda bi, ti: (bi, 0, 0)),
                  const((d, d)), const((d, d)), const((d, d)), const((1, d)), const((1, d)),
                  const((d, dff)), const((d, dff)), const((CONV_W, dff)), const((1, dff)),
                  const((dff, d)), const((1, d)), const((1, d))],
        out_specs=[pl.BlockSpec((1, tm, d), lambda bi, ti: (bi, ti, 0)),
                   pl.BlockSpec((seqs, hist, dff), lambda bi, ti: (bi, 0, 0))],
        out_shape=[jax.ShapeDtypeStruct((nb, rows, d), F32),
                   jax.ShapeDtypeStruct((b, hist, dff), F32)],
        scratch_shapes=[pltpu.VMEM((tm + 8, dff), F32)],
        compiler_params=_cparams(("parallel", "arbitrary")),
        name="tail",
    )(x3, oa, ob, p3, p3, conv_prev,
      w["w_pa"], w["w_pb"], w["w_out"], w["ln1_g"], w["ln1_b"],
      w["w_up"], w["w_gate"], w["conv_w"], w["conv_b"], w["w_down"], w["ln2_g"], w["ln2_b"])
    return y.reshape(b, s, d), conv


def kernel(x_prompt, x_sample, cache_a_k, cache_a_v, cache_b_k, cache_b_v, cache_conv, t5_table, w_in, lambda_q1, lambda_k1, lambda_q2, lambda_k2, subln_g, rel_table_b, w_pa, w_pb, w_out, ln1_g, ln1_b, w_up, w_gate, conv_w, conv_b, w_down, ln2_g, ln2_b):
    depth = w_in.shape[0]
    alpha = (2.0 * depth) ** 0.25
    bp, s, d = x_prompt.shape
    bs, t_new, _ = x_sample.shape
    past = cache_a_k.shape[2]
    pb = cache_b_k.shape[2]
    dff = w_up.shape[2]
    keep = min(LEFT_CHUNKS * CHUNK, s)
    assert pb == min(LEFT_CHUNKS * CHUNK, past)

    xp, xs = x_prompt, x_sample
    outs_p = [[] for _ in range(5)]
    outs_s = [[] for _ in range(5)]
    for l in range(depth):
        lam_init = 0.8 - 0.6 * math.exp(-0.3 * l)
        lams = [v[l].astype(F32).reshape(1, DA) for v in (lambda_q1, lambda_k1, lambda_q2, lambda_k2)]
        g_row = subln_g[l].astype(F32).reshape(1, 2 * DA)
        g_col = subln_g[l].astype(F32).reshape(2 * DA, 1)
        w = {
            "w_pa": w_pa[l].astype(BF16), "w_pb": w_pb[l].astype(BF16), "w_out": w_out[l].astype(BF16),
            "ln1_g": ln1_g[l].reshape(1, d), "ln1_b": ln1_b[l].reshape(1, d),
            "w_up": w_up[l].astype(BF16), "w_gate": w_gate[l].astype(BF16),
            "conv_w": conv_w[l], "conv_b": conv_b[l].reshape(1, dff),
            "w_down": w_down[l].astype(BF16),
            "ln2_g": ln2_g[l].reshape(1, d), "ln2_b": ln2_b[l].reshape(1, d),
        }
        w_in_b = w_in[l].astype(BF16)
        da, db = _bias_tiles(t5_table, rel_table_b[l])

        p, kat, va, kbt, vbt = _project_prompt(xp, w_in_b, keep, ATT_T)
        p3 = p.reshape(bp, s, 8 * d)
        oa = _attn_a_prompt(p3, da, lams, g_col, lam_init)
        ob = _attn_b_prompt(p3, db)
        conv0 = jnp.zeros((bp, CONV_W - 1, dff), F32)
        xp, conv_p = _tail(xp, oa, ob, p3, conv0, w, ATT_T, alpha)
        outs_p[0].append(kat.reshape(bp, HA, 2, DA, s).transpose(0, 4, 1, 2, 3))
        outs_p[1].append(va.reshape(bp, s, HA, 2 * DA))
        outs_p[2].append(kbt.reshape(bp, HB, DB, keep).transpose(0, 3, 1, 2))
        outs_p[3].append(vbt.reshape(bp, HB, DB, keep).transpose(0, 3, 1, 2))
        outs_p[4].append(conv_p)

        rows = bs * t_new
        p, ka, va, kb, vb = _project(xs.reshape(rows, d), w_in_b, t_new, t_new,
                                     ATT_T if rows % ATT_T == 0 else t_new)
        p3 = p.reshape(bs, t_new, 8 * d)
        cak_t = cache_a_k[l].transpose(0, 2, 3, 4, 1).reshape(bs, d, past)
        cbk_t = cache_b_k[l].transpose(0, 2, 3, 1).reshape(bs, d, pb)
        cbv_t = cache_b_v[l].transpose(0, 2, 3, 1).reshape(bs, d, pb)
        oa = _attn_a_sample(p3, cak_t, cache_a_v[l].reshape(bs, past * HA, 2 * DA),
                            da, lams, g_row, lam_init)
        ob = _attn_b_sample(p3, cbk_t, cbv_t, db, past)
        pack = ATT_T // t_new if (ATT_T % t_new == 0 and bs % (ATT_T // t_new) == 0) else 1
        xs, conv_s = _tail(xs, oa, ob, p3, cache_conv[l].astype(F32), w, pack * t_new, alpha)
        outs_s[0].append(ka.reshape(bs, t_new, HA, 2, DA))
        outs_s[1].append(va.reshape(bs, t_new, HA, 2 * DA))
        outs_s[2].append(kb.reshape(bs, t_new, HB, DB))
        outs_s[3].append(vb.reshape(bs, t_new, HB, DB))
        outs_s[4].append(conv_s)

    return (xp, xs, *[jnp.stack(o) for o in outs_p], *[jnp.stack(o) for o in outs_s])
```

```python
import functools
import math

import jax
import jax.numpy as jnp
from jax import lax
from jax.experimental import pallas as pl
from jax.experimental.pallas import tpu as pltpu

F32 = jnp.float32
BF16 = jnp.bfloat16

CHUNK = 64
HA = 8
DA = 64
HB = 16
DB = 64
LEFT_CHUNKS = 8
MAX_REL = 128
T5_BUCKETS = 32
T5_MAX_DIST = 128
CONV_W = 3
LN_EPS = 1e-5
RMS_EPS = 1e-5
NEG = -1e30
LOG2E = math.log2(math.e)

LANES = 128
ATT_T = 256
ATT_TA = 512
ATT_B_HEADS = 4
ONES_ROWS = 16
VMEM_LIMIT = 56 * 1024 * 1024


def _cparams(sem, vmem=VMEM_LIMIT):
    return pltpu.CompilerParams(dimension_semantics=sem, vmem_limit_bytes=vmem)


def _toeplitz(vec, t):
    x = jnp.broadcast_to(vec, (t, 2 * t))
    row = lax.broadcasted_iota(jnp.int32, (t, 2 * t), 0)
    shift = 1
    while shift < t:
        x = jnp.where((row & shift) != 0, pltpu.roll(x, shift, 1), x)
        shift *= 2
    return x[:, t:]


def _t5_bucket(rel):
    nb = T5_BUCKETS // 2
    max_exact = nb // 2
    ret = jnp.where(rel > 0, nb, 0)
    n = jnp.abs(rel)
    nf = jnp.maximum(n, 1).astype(F32)
    large = max_exact + (jnp.log(nf / max_exact) / math.log(T5_MAX_DIST / max_exact)
                         * (nb - max_exact)).astype(jnp.int32)
    large = jnp.minimum(large, nb - 1)
    return (ret + jnp.where(n < max_exact, n, large)) & (T5_BUCKETS - 1)


def _chunk_ids(t):
    kc = lax.broadcasted_iota(jnp.int32, (t, t), 0) // CHUNK
    qc = lax.broadcasted_iota(jnp.int32, (t, t), 1) // CHUNK
    return kc, qc


def _bias_a_kernel(tab_ref, bown_ref, bcorner_ref, own_ref, corner_ref, *, t):
    h = pl.program_id(0)
    c = T5_MAX_DIST
    far = tab_ref[h * T5_BUCKETS + T5_BUCKETS // 2 - 1]

    def band(bucket):
        acc = jnp.zeros(bucket.shape, F32)
        for j in range(T5_BUCKETS):
            acc = jnp.where(bucket == j, tab_ref[h * T5_BUCKETS + j], acc)
        return (acc - far) * LOG2E

    kc, qc = _chunk_ids(t)
    own_ref[0] = jnp.where(kc <= qc, _toeplitz(band(bown_ref[...]), t), NEG)
    corner_ref[0] = _toeplitz(band(bcorner_ref[...]), c)


def _bias_b_kernel(band_ref, out_ref, *, t):
    own = band_ref[0]
    far = own[:, 2 * t - 1:]
    r = lax.broadcasted_iota(jnp.int32, (1, 2 * t), 1)
    prev = jnp.where(r < t, pltpu.roll(own, t, 1), far)
    kc, qc = _chunk_ids(t)
    out_ref[0, 0:t, :] = jnp.where(kc - 2 * (t // CHUNK) >= qc - LEFT_CHUNKS, 0.0, NEG)
    out_ref[0, t:2 * t, :] = _toeplitz((prev - far) * LOG2E, t)
    out_ref[0, 2 * t:3 * t, :] = jnp.where(kc <= qc, _toeplitz((own - far) * LOG2E, t), NEG)
    out_ref[0, 3 * t:4 * t, :] = jnp.full((t, t), NEG, F32)


def _bias_tiles(t5_table, rel_table):
    ta, tb = ATT_TA, ATT_T
    assert ta >= T5_MAX_DIST and ta % CHUNK == 0
    assert tb >= MAX_REL and 2 * tb == LEFT_CHUNKS * CHUNK
    smem = pl.BlockSpec(memory_space=pltpu.SMEM)
    c = T5_MAX_DIST
    bucket_own = _t5_bucket(ta - jnp.arange(2 * ta, dtype=jnp.int32)).reshape(1, 2 * ta)
    bucket_corner = _t5_bucket(-jnp.arange(2 * c, dtype=jnp.int32)).reshape(1, 2 * c)
    da = pl.pallas_call(
        functools.partial(_bias_a_kernel, t=ta),
        grid=(HA,),
        in_specs=[smem, pl.BlockSpec((1, 2 * ta), lambda h: (0, 0)),
                  pl.BlockSpec((1, 2 * c), lambda h: (0, 0))],
        out_specs=[pl.BlockSpec((1, ta, ta), lambda h: (h, 0, 0)),
                   pl.BlockSpec((1, c, c), lambda h: (h, 0, 0))],
        out_shape=[jax.ShapeDtypeStruct((HA, ta, ta), F32),
                   jax.ShapeDtypeStruct((HA, c, c), F32)],
        compiler_params=_cparams(("parallel",)),
        name="bias_a",
    )(t5_table.astype(F32).T.reshape(-1), bucket_own, bucket_corner)
    band_b = jnp.pad(rel_table.astype(F32).T[:, ::-1],
                     ((0, 0), (tb - MAX_REL, tb - MAX_REL - 1)), mode="edge").reshape(HB, 1, 2 * tb)
    db = pl.pallas_call(
        functools.partial(_bias_b_kernel, t=tb),
        grid=(HB,),
        in_specs=[pl.BlockSpec((1, 1, 2 * tb), lambda h: (h, 0, 0))],
        out_specs=pl.BlockSpec((1, 4 * tb, tb), lambda h: (h, 0, 0)),
        out_shape=jax.ShapeDtypeStruct((HB, 4 * tb, tb), F32),
        compiler_params=_cparams(("parallel",)),
        name="bias_b",
    )(band_b)
    return da, db


def _proj_kernel(x_ref, w_ref, p_ref, ka_ref, va_ref, kb_ref, vb_ref, *, d, q_scale):
    xb = x_ref[...].astype(BF16)
    f32_outs = {1: ka_ref, 2: va_ref, 4: kb_ref, 5: vb_ref}
    for c in range(8):
        acc = jnp.dot(xb, w_ref[:, c * d:(c + 1) * d], preferred_element_type=F32)
        if c in f32_outs:
            f32_outs[c][...] = acc
        if c in (0, 3):
            acc = acc * q_scale
        p_ref[:, c * d:(c + 1) * d] = acc.astype(BF16)


def _proj_cache_kernel(x_ref, w_ref, p_ref, kat_ref, va_ref, kbt_ref, vbt_ref, *,
                       d, q_scale, tm, first_keep, tiles_per_seq):
    t = pl.program_id(0) % tiles_per_seq
    xb = x_ref[...].astype(BF16)
    kept = {}
    for c in range(8):
        acc = jnp.dot(xb, w_ref[:, c * d:(c + 1) * d], preferred_element_type=F32)
        if c == 1:
            kat_ref[0] = acc.T
        elif c == 2:
            for h in range(HA):
                va_ref[pl.ds(h, tm, stride=HA), :] = acc[:, h * LANES:(h + 1) * LANES]
        elif c in (4, 5):
            kept[c] = acc
        if c in (0, 3):
            acc = acc * q_scale
        p_ref[:, c * d:(c + 1) * d] = acc.astype(BF16)

    @pl.when(t >= first_keep)
    def _():
        kbt_ref[0] = kept[4].T
        vbt_ref[0] = kept[5].T


def _project_prompt(x3, w_bf16, keep, tm):
    b, s, d = x3.shape
    n = w_bf16.shape[1]
    assert n == 8 * d and d == HA * 2 * DA == HB * DB == HA * LANES
    assert s % tm == 0 and keep % tm == 0 and tm % LANES == 0
    tiles_per_seq = s // tm
    first_keep = tiles_per_seq - keep // tm
    m = b * s
    row = lambda i: (i, 0)
    col = lambda i: (i // tiles_per_seq, 0, i % tiles_per_seq)
    keep_col = lambda i: (i // tiles_per_seq, 0, jnp.maximum(i % tiles_per_seq - first_keep, 0))
    return pl.pallas_call(
        functools.partial(_proj_cache_kernel, d=d, q_scale=DA ** -0.5 * LOG2E, tm=tm,
                          first_keep=first_keep, tiles_per_seq=tiles_per_seq),
        grid=(m // tm,),
        in_specs=[pl.BlockSpec((tm, d), row),
                  pl.BlockSpec((d, n), lambda i: (0, 0), pipeline_mode=pl.Buffered(1))],
        out_specs=[pl.BlockSpec((tm, n), row),
                   pl.BlockSpec((1, d, tm), col), pl.BlockSpec((tm * HA, LANES), row),
                   pl.BlockSpec((1, d, tm), keep_col), pl.BlockSpec((1, d, tm), keep_col)],
        out_shape=[jax.ShapeDtypeStruct((m, n), BF16),
                   jax.ShapeDtypeStruct((b, d, s), F32), jax.ShapeDtypeStruct((m * HA, LANES), F32),
                   jax.ShapeDtypeStruct((b, d, keep), F32), jax.ShapeDtypeStruct((b, d, keep), F32)],
        compiler_params=_cparams(("arbitrary",)),
        name="in_proj_prompt",
    )(x3.reshape(m, d), w_bf16)


def _project(x2d, w_bf16, rows_per_seq, keep, tm):
    m, d = x2d.shape
    n = w_bf16.shape[1]
    assert n == 8 * d and d == HA * 2 * DA == HB * DB
    assert m % tm == 0
    row = lambda i: (i, 0)
    n_keep = (m // rows_per_seq) * keep
    if keep == rows_per_seq:
        keep_map = row
    else:
        assert rows_per_seq % tm == 0 and keep % tm == 0
        tiles_per_seq = rows_per_seq // tm
        keep_tiles = keep // tm
        first_keep = tiles_per_seq - keep_tiles

        def keep_map(i):
            return ((i // tiles_per_seq) * keep_tiles
                    + jnp.maximum(i % tiles_per_seq - first_keep, 0), 0)
    return pl.pallas_call(
        functools.partial(_proj_kernel, d=d, q_scale=DA ** -0.5 * LOG2E),
        grid=(m // tm,),
        in_specs=[pl.BlockSpec((tm, d), row),
                  pl.BlockSpec((d, n), lambda i: (0, 0), pipeline_mode=pl.Buffered(1))],
        out_specs=[pl.BlockSpec((tm, n), row),
                   pl.BlockSpec((tm, d), row), pl.BlockSpec((tm, d), row),
                   pl.BlockSpec((tm, d), keep_map), pl.BlockSpec((tm, d), keep_map)],
        out_shape=[jax.ShapeDtypeStruct((m, n), BF16),
                   jax.ShapeDtypeStruct((m, d), F32), jax.ShapeDtypeStruct((m, d), F32),
                   jax.ShapeDtypeStruct((n_keep, d), F32), jax.ShapeDtypeStruct((n_keep, d), F32)],
        compiler_params=_cparams(("arbitrary",)),
        name="in_proj",
    )(x2d, w_bf16)


def _half_masks(shape):
    lane = lax.broadcasted_iota(jnp.int32, shape, len(shape) - 1)
    return lane < (LANES // 2), lane >= (LANES // 2)


def _stack_halves(q):
    lo, hi = _half_masks(q.shape)
    zero = jnp.zeros_like(q)
    return jnp.concatenate([jnp.where(lo, q, zero), jnp.where(hi, q, zero)], axis=0)


def _qk(a, b):
    return lax.dot_general(a, b, (((1,), (1,)), ((), ())), preferred_element_type=F32)


def _transpose_blocks(v_ref, vt_ref, t):
    for j in range(vt_ref.shape[0]):
        vt_ref[j] = v_ref[0, j * t:(j + 1) * t, :].astype(F32).T.astype(vt_ref.dtype)


def _softmax_pv(scores, values, values_t=None):
    values_t = values_t or (False,) * len(values)
    m = functools.reduce(jnp.maximum, [jnp.max(s, axis=1, keepdims=True) for s in scores])
    l = None
    o = None
    for s, v, vt in zip(scores, values, values_t):
        p = jnp.exp2(s - m)
        ls = jnp.sum(p, axis=1, keepdims=True)
        pb = p.astype(BF16)
        ov = _qk(pb, v) if vt else jnp.dot(pb, v, preferred_element_type=F32)
        l = ls if l is None else l + ls
        o = ov if o is None else o + ov
    return o / l


def _lam(lq1, lk1, lq2, lk2, lam_init):
    e1 = jnp.exp(jnp.sum(lq1 * lk1, axis=1, keepdims=True))
    e2 = jnp.exp(jnp.sum(lq2 * lk2, axis=1, keepdims=True))
    return e1 - e2 + lam_init


def _attn_a_kernel(lq1_ref, lk1_ref, lq2_ref, lk2_ref, g_ref, q_ref, k_ref, v_ref, d_ref, dc_ref,
                   o_ref, vt_ref, s_ref, mc_ref, m_ref, acc_ref, done_ref, *, t, lam_init):
    n_tiles = q_ref.shape[1] // t
    c = dc_ref.shape[1]
    q_tile = lambda qi: _stack_halves(q_ref[0, pl.ds(pl.multiple_of(qi * t, t), t), :])
    lam = _lam(lq1_ref[...], lk1_ref[...], lq2_ref[...], lk2_ref[...], lam_init)

    def scores(q, kb, bias=None):
        start = pl.multiple_of(kb * t, t)
        s = _qk(k_ref[0, pl.ds(start, t), :], q)
        if bias is not None:
            s = s + jnp.concatenate([bias, bias], axis=1)
        s_ref[...] = s
        mc_ref[...] = jnp.max(s, axis=0, keepdims=True)

    def scores_prev(q, kb):
        start = pl.multiple_of(kb * t, t)
        top = _qk(k_ref[0, pl.ds(start, t - c), :], q)
        bot = _qk(k_ref[0, pl.ds(pl.multiple_of(start + (t - c), c), c), :], q)
        corner = dc_ref[0]
        bot = jnp.concatenate([bot[:, :c] + corner, bot[:, c:t],
                               bot[:, t:t + c] + corner, bot[:, t + c:]], axis=1)
        s_ref[:t - c, :] = top
        s_ref[t - c:, :] = bot
        mc_ref[...] = jnp.maximum(jnp.max(top, axis=0, keepdims=True),
                                  jnp.max(bot, axis=0, keepdims=True))

    def accumulate(kb, refill):
        s = s_ref[...]
        m_prev = m_ref[...]
        m_new = jnp.maximum(m_prev, mc_ref[...])
        alpha = jnp.exp2(m_prev - m_new)
        p = jnp.exp2(s - m_new)
        refill()
        pv = jnp.dot(vt_ref[kb], p.astype(BF16), preferred_element_type=F32)
        acc_ref[...] = alpha * acc_ref[...] + pv
        m_ref[...] = m_new

    for j in range(n_tiles):
        vt_ref[j, :LANES, :] = v_ref[0, j * t:(j + 1) * t, :].astype(F32).T.astype(BF16)
        vt_ref[j, LANES:, :] = jnp.ones((ONES_ROWS, t), BF16)
    scores(q_tile(0), 0, d_ref[0])

    def finish(qi):
        o = done_ref[:LANES, :] * (1.0 / done_ref[LANES:LANES + 1, :])
        o = o[:, :t] - lam * o[:, t:]
        o = o * lax.rsqrt(jnp.mean(o * o, axis=0, keepdims=True) + RMS_EPS)
        o = o * (g_ref[...] * (1.0 - lam_init))
        o_ref[0, pl.ds(pl.multiple_of(qi * t, t), t), :] = o.T.astype(o_ref.dtype)

    def tile(qi, carry):
        q2 = q_tile(qi)
        m_ref[...] = jnp.full(m_ref.shape, NEG, F32)
        acc_ref[...] = jnp.zeros(acc_ref.shape, F32)

        n_far = jnp.maximum(qi - 1, 0)

        @pl.when(qi >= 1)
        def _():
            finish(qi - 1)
            accumulate(qi, lambda: scores_prev(q2, qi - 1))

        def far(i):
            accumulate(jnp.where(i == 0, qi - 1, i - 1), lambda: scores(q2, i))

        def far_pair(j, c):
            far(2 * j)
            far(2 * j + 1)
            return c

        lax.fori_loop(0, n_far // 2, far_pair, 0)

        @pl.when(n_far % 2 == 1)
        def _():
            far(n_far - 1)
        nxt = jnp.minimum(qi + 1, n_tiles - 1)
        accumulate(jnp.where(qi == 0, 0, jnp.where(n_far == 0, qi - 1, n_far - 1)),
                   lambda: scores(q_tile(nxt), nxt, d_ref[0]))
        done_ref[...] = acc_ref[...]
        return carry

    lax.fori_loop(0, n_tiles, tile, 0)
    finish(n_tiles - 1)


def _attn_a_prompt(p3, da, lams, g_col, lam_init):
    b, s, n = p3.shape
    t = ATT_TA
    d_own, d_corner = da
    c = d_corner.shape[1]
    assert s % t == 0 and n == 8 * HA * LANES and t > c
    vec = pl.BlockSpec((1, DA), lambda bi, h: (0, 0))
    return pl.pallas_call(
        functools.partial(_attn_a_kernel, t=t, lam_init=lam_init),
        grid=(b, HA),
        in_specs=[vec, vec, vec, vec,
                  pl.BlockSpec((LANES, 1), lambda bi, h: (0, 0)),
                  pl.BlockSpec((1, s, LANES), lambda bi, h: (bi, 0, h)),
                  pl.BlockSpec((1, s, LANES), lambda bi, h: (bi, 0, HA + h)),
                  pl.BlockSpec((1, s, LANES), lambda bi, h: (bi, 0, 2 * HA + h)),
                  pl.BlockSpec((1, t, t), lambda bi, h: (h, 0, 0)),
                  pl.BlockSpec((1, c, c), lambda bi, h: (h, 0, 0))],
        out_specs=pl.BlockSpec((1, s, LANES), lambda bi, h: (bi, 0, h)),
        out_shape=jax.ShapeDtypeStruct((b, s, HA * LANES), BF16),
        scratch_shapes=[pltpu.VMEM((s // t, LANES + ONES_ROWS, t), BF16),
                        pltpu.VMEM((t, 2 * t), F32), pltpu.VMEM((1, 2 * t), F32),
                        pltpu.VMEM((1, 2 * t), F32),
                        pltpu.VMEM((LANES + ONES_ROWS, 2 * t), F32),
                        pltpu.VMEM((LANES + ONES_ROWS, 2 * t), F32)],
        compiler_params=_cparams(("parallel", "parallel")),
        name="attn_a_prompt",
    )(*lams, g_col, p3, p3, p3, d_own, d_corner)


def _attn_b_kernel(q_ref, k_ref, v_ref, d_ref, o_ref, vt_ref, s_ref, mc_ref, *, t, heads):
    n_tiles = q_ref.shape[1] // t

    def window(q_tile):
        blocks = [jnp.maximum(q_tile - 2 + j, 0) for j in range(3)]
        bias_rows = [pl.multiple_of(jnp.where(q_tile - 2 + j >= 0, j, 3) * t, t) for j in range(3)]
        return blocks, bias_rows

    def score_part(g, q_tile):
        blocks, bias_rows = window(q_tile)
        pair = slice((g // 2) * LANES, (g // 2 + 1) * LANES)
        q = q_ref[0, pl.ds(pl.multiple_of(q_tile * t, t), t), pair]
        q = jnp.where(_half_masks(q.shape)[g % 2], q, jnp.zeros_like(q))

        def part(j, mc):
            k = k_ref[0, pl.ds(pl.multiple_of(blocks[j] * t, t), t), pair]
            s = _qk(k, q) + d_ref[g, pl.ds(bias_rows[j], t), :]
            s_ref[g, j * t:(j + 1) * t, :] = s
            cm = jnp.max(s, axis=0, keepdims=True)
            return cm if mc is None else jnp.maximum(mc, cm)

        return part

    _transpose_blocks(v_ref, vt_ref, t)
    part0 = score_part(0, 0)
    mc_ref[0] = functools.reduce(lambda mc, j: part0(j, mc), range(3), None)

    def tile(qi, carry):
        blocks, _ = window(qi)

        def finish(g):
            g_next = (g + 1) % heads
            nxt = score_part(g_next, qi if g + 1 < heads else jnp.minimum(qi + 1, n_tiles - 1))
            m = mc_ref[g]
            mc = l = o = None
            for j, kb in enumerate(blocks):
                p = jnp.exp2(s_ref[g, j * t:(j + 1) * t, :] - m)
                mc = nxt(j, mc)
                ls = jnp.sum(p, axis=0, keepdims=True)
                vt = vt_ref[kb, g * DB:(g + 1) * DB, :]
                pv = jnp.dot(vt, p.astype(BF16), preferred_element_type=F32)
                l = ls if l is None else l + ls
                o = pv if o is None else o + pv
            mc_ref[g_next] = mc
            return o * (1.0 / l)

        outs = [finish(g) for g in range(heads)]
        o_ref[0, pl.ds(pl.multiple_of(qi * t, t), t), :] = (
            jnp.concatenate(outs, axis=0).T.astype(o_ref.dtype))
        return carry

    lax.fori_loop(0, n_tiles, tile, 0)


def _attn_b_prompt(p3, db):
    b, s, n = p3.shape
    t = ATT_T
    g = ATT_B_HEADS
    w = g * DB
    assert s % t == 0 and HB % g == 0 and w % LANES == 0
    col0 = 3 * HA * LANES // w
    nblk = HB // g
    return pl.pallas_call(
        functools.partial(_attn_b_kernel, t=t, heads=g),
        grid=(b, nblk),
        in_specs=[pl.BlockSpec((1, s, w), lambda bi, h: (bi, 0, col0 + h)),
                  pl.BlockSpec((1, s, w), lambda bi, h: (bi, 0, col0 + nblk + h)),
                  pl.BlockSpec((1, s, w), lambda bi, h: (bi, 0, col0 + 2 * nblk + h)),
                  pl.BlockSpec((g, 4 * t, t), lambda bi, h: (h, 0, 0))],
        out_specs=pl.BlockSpec((1, s, w), lambda bi, h: (bi, 0, h)),
        out_shape=jax.ShapeDtypeStruct((b, s, HB * DB), BF16),
        scratch_shapes=[pltpu.VMEM((s // t, w, t), BF16),
                        pltpu.VMEM((g, 3 * t, t), F32), pltpu.VMEM((g, 1, t), F32)],
        compiler_params=_cparams(("parallel", "parallel")),
        name="attn_b_prompt",
    )(p3, p3, p3, db)


def _attn_a_sample_kernel(lq1_ref, lk1_ref, lq2_ref, lk2_ref, g_ref, p_ref, kc_ref, vc_ref, d_ref,
                          dc_ref, o_ref, *, lam_init):
    tq = p_ref.shape[1]
    past = kc_ref.shape[2]
    c = dc_ref.shape[1]
    lam = _lam(lq1_ref[...], lk1_ref[...], lq2_ref[...], lk2_ref[...], lam_init)
    for h in range(HA):
        head = lambda sec: p_ref[0, :, (sec * HA + h) * LANES:(sec * HA + h + 1) * LANES]
        q2 = _stack_halves(head(0))
        kn, vn = head(1), head(2)
        kct = kc_ref[0, h * LANES:(h + 1) * LANES, :].astype(BF16)
        vc = vc_ref[0, pl.ds(h, past, stride=HA), :].astype(BF16)
        near = dc_ref[h].T[:tq]
        new = d_ref[h].T[:tq, :tq]
        scores = [jnp.dot(q2, kct[:, :past - c], preferred_element_type=F32),
                  jnp.dot(q2, kct[:, past - c:], preferred_element_type=F32)
                  + jnp.concatenate([near, near], axis=0),
                  _qk(q2, kn) + jnp.concatenate([new, new], axis=0)]
        o = _softmax_pv(scores, [vc[:past - c], vc[past - c:], vn])
        o = o[:tq] - lam * o[tq:]
        o = o * lax.rsqrt(jnp.mean(o * o, axis=1, keepdims=True) + RMS_EPS)
        o_ref[0, :, h * LANES:(h + 1) * LANES] = (o * g_ref[...] * (1.0 - lam_init)).astype(o_ref.dtype)


def _attn_a_sample(p3, cache_kt, cache_v, da, lams, g_row, lam_init):
    b, tq, n = p3.shape
    d, past = cache_kt.shape[1:]
    d_own, d_corner = da
    c = d_corner.shape[1]
    assert tq <= CHUNK and past % CHUNK == 0 and past > c and tq % 8 == 0 and c == LANES
    vec = pl.BlockSpec((1, DA), lambda bi: (0, 0))
    whole = lambda bi: (0, 0, 0)
    return pl.pallas_call(
        functools.partial(_attn_a_sample_kernel, lam_init=lam_init),
        grid=(b,),
        in_specs=[vec, vec, vec, vec,
                  pl.BlockSpec((1, LANES), lambda bi: (0, 0)),
                  pl.BlockSpec((1, tq, n), lambda bi: (bi, 0, 0)),
                  pl.BlockSpec((1, d, past), lambda bi: (bi, 0, 0)),
                  pl.BlockSpec((1, past * HA, LANES), lambda bi: (bi, 0, 0)),
                  pl.BlockSpec((HA, LANES, LANES), whole, pipeline_mode=pl.Buffered(1)),
                  pl.BlockSpec((HA, c, c), whole, pipeline_mode=pl.Buffered(1))],
        out_specs=pl.BlockSpec((1, tq, d), lambda bi: (bi, 0, 0)),
        out_shape=jax.ShapeDtypeStruct((b, tq, d), BF16),
        compiler_params=_cparams(("parallel",)),
        name="attn_a_sample",
    )(*lams, g_row, p3, cache_kt, cache_v, d_own, d_corner)


def _attn_b_sample_kernel(p_ref, kc_ref, vc_ref, d_ref, o_ref, *, t):
    tq = p_ref.shape[1]
    past = kc_ref.shape[2]
    pairs = HB // 2
    olo, _ = _half_masks((tq, LANES))
    for hp in range(pairs):
        rows = slice(hp * LANES, (hp + 1) * LANES)
        head = lambda sec: p_ref[0, :, (3 * HA + sec * pairs + hp) * LANES:
                                 (3 * HA + sec * pairs + hp + 1) * LANES]
        q2 = _stack_halves(head(0))
        kct = kc_ref[0, rows, :].astype(BF16)
        vct = vc_ref[0, rows, :].astype(BF16)
        tiles = [d_ref[2 * hp + hh] for hh in range(2)]
        near = jnp.concatenate([d[t:2 * t, :].T[:tq] for d in tiles], axis=0)
        new = jnp.concatenate([d[2 * t:2 * t + LANES, :].T[:tq, :tq] for d in tiles], axis=0)
        scores = [jnp.dot(q2, kct[:, past - t:], preferred_element_type=F32) + near,
                  _qk(q2, head(1)) + new]
        values = [vct[:, past - t:], head(2)]
        values_t = [True, False]
        if past > t:
            scores.insert(0, jnp.dot(q2, kct[:, :past - t], preferred_element_type=F32))
            values.insert(0, vct[:, :past - t])
            values_t.insert(0, True)
        o = _softmax_pv(scores, values, values_t)
        o_ref[0, :, rows] = jnp.where(olo, o[:tq], o[tq:]).astype(o_ref.dtype)


def _attn_b_sample(p3, cache_kt, cache_vt, db, past_len):
    b, tq, n = p3.shape
    d, pb = cache_kt.shape[1:]
    t = ATT_T
    assert tq <= CHUNK and past_len % CHUNK == 0 and pb == LEFT_CHUNKS * CHUNK and pb >= t
    return pl.pallas_call(
        functools.partial(_attn_b_sample_kernel, t=t),
        grid=(b,),
        in_specs=[pl.BlockSpec((1, tq, n), lambda bi: (bi, 0, 0)),
                  pl.BlockSpec((1, d, pb), lambda bi: (bi, 0, 0)),
                  pl.BlockSpec((1, d, pb), lambda bi: (bi, 0, 0)),
                  pl.BlockSpec((HB, 4 * t, LANES), lambda bi: (0, 0, 0),
                               pipeline_mode=pl.Buffered(1))],
        out_specs=pl.BlockSpec((1, tq, d), lambda bi: (bi, 0, 0)),
        out_shape=jax.ShapeDtypeStruct((b, tq, d), BF16),
        compiler_params=_cparams(("parallel",)),
        name="attn_b_sample",
    )(p3, cache_kt, cache_vt, db)


def _layer_norm(x, g, b):
    mu = jnp.mean(x, axis=1, keepdims=True)
    xc = x - mu
    var = jnp.mean(xc * xc, axis=1, keepdims=True)
    return xc * lax.rsqrt(var + LN_EPS) * g + b


def _tail_kernel(x_ref, oa_ref, ob_ref, ga_ref, gb_ref, prev_ref,
                 wpa_ref, wpb_ref, wout_ref, ln1g_ref, ln1b_ref,
                 wup_ref, wgate_ref, cw_ref, cb_ref, wdown_ref, ln2g_ref, ln2b_ref,
                 y_ref, conv_ref, u_ref, *, tm, alpha, seqs):
    ti = pl.program_id(1)
    pad = 8
    hist = CONV_W - 1
    dff = u_ref.shape[1]

    if seqs == 1:
        @pl.when(ti == 0)
        def _():
            u_ref[pad - hist:pad, :] = prev_ref[0]
    else:
        u_ref[pad - hist:pad, :] = jnp.zeros((hist, dff), F32)

    ya = jnp.dot(oa_ref[0], wpa_ref[...], preferred_element_type=F32)
    yb = jnp.dot(ob_ref[0], wpb_ref[...], preferred_element_type=F32)
    merged = (jax.nn.sigmoid(ga_ref[0].astype(F32)) * ya
              + jax.nn.sigmoid(gb_ref[0].astype(F32)) * yb)
    mixed = jnp.dot(merged.astype(BF16), wout_ref[...], preferred_element_type=F32)
    h = _layer_norm(alpha * x_ref[0] + mixed, ln1g_ref[...], ln1b_ref[...])
    hb = h.astype(BF16)
    u = jnp.dot(hb, wup_ref[...], preferred_element_type=F32)
    g = jnp.dot(hb, wgate_ref[...], preferred_element_type=F32)
    u_ref[pad:pad + tm, :] = u
    u2 = u_ref[pad - 2:pad - 2 + tm, :]
    u1 = u_ref[pad - 1:pad - 1 + tm, :]
    if seqs > 1:
        sl = tm // seqs
        pos = lax.broadcasted_iota(jnp.int32, (tm, 1), 0) % sl
        hist_rows = lambda j: jnp.concatenate(
            [jnp.broadcast_to(prev_ref[q, j:j + 1, :], (sl, dff)) for q in range(seqs)], axis=0)
        u1 = jnp.where(pos == 0, hist_rows(1), u1)
        u2 = jnp.where(pos == 0, hist_rows(0), jnp.where(pos == 1, hist_rows(1), u2))
    uc = cb_ref[...] + u2 * cw_ref[0:1, :]
    uc = uc + u1 * cw_ref[1:2, :]
    uc = uc + u * cw_ref[2:3, :]
    if seqs == 1:
        last = u_ref[pad + tm - hist:pad + tm, :]
        conv_ref[0] = last
        u_ref[pad - hist:pad, :] = last
    else:
        for q in range(seqs):
            conv_ref[q] = u_ref[pad + (q + 1) * sl - hist:pad + (q + 1) * sl, :]
    f = jnp.dot((jax.nn.gelu(uc) * g).astype(BF16), wdown_ref[...], preferred_element_type=F32)
    y_ref[0] = _layer_norm(alpha * h + f, ln2g_ref[...], ln2b_ref[...])


def _tail(x3, oa, ob, p3, conv_prev, w, tm, alpha):
    b, s, d = x3.shape
    dff = w["w_up"].shape[1]
    hist = CONV_W - 1
    seqs = max(tm // s, 1)
    assert (s % tm == 0 or tm % s == 0) and b % seqs == 0 and s >= hist
    if seqs > 1:
        fold = lambda a: a.reshape(b // seqs, seqs * s, a.shape[-1])
        x3, oa, ob, p3 = fold(x3), fold(oa), fold(ob), fold(p3)
    nb, rows = x3.shape[:2]
    const = lambda shape: pl.BlockSpec(shape, lambda bi, ti: (0,) * len(shape),
                                       pipeline_mode=pl.Buffered(1))
    act = lambda col: pl.BlockSpec((1, tm, d), lambda bi, ti: (bi, ti, col))
    y, conv = pl.pallas_call(
        functools.partial(_tail_kernel, tm=tm, alpha=alpha, seqs=seqs),
        grid=(nb, rows // tm),
        in_specs=[act(0), act(0), act(0), act(6), act(7),
                  pl.BlockSpec((seqs, hist, dff), lambda bi, ti: (bi, 0, 0)),
                  const((d, d)), const((d, d)), const((d, d)), const((1, d)), const((1, d)),
                  const((d, dff)), const((d, dff)), const((CONV_W, dff)), const((1, dff)),
                  const((dff, d)), const((1, d)), const((1, d))],
        out_specs=[pl.BlockSpec((1, tm, d), lambda bi, ti: (bi, ti, 0)),
                   pl.BlockSpec((seqs, hist, dff), lambda bi, ti: (bi, 0, 0))],
        out_shape=[jax.ShapeDtypeStruct((nb, rows, d), F32),
                   jax.ShapeDtypeStruct((b, hist, dff), F32)],
        scratch_shapes=[pltpu.VMEM((tm + 8, dff), F32)],
        compiler_params=_cparams(("parallel", "arbitrary")),
        name="tail",
    )(x3, oa, ob, p3, p3, conv_prev,
      w["w_pa"], w["w_pb"], w["w_out"], w["ln1_g"], w["ln1_b"],
      w["w_up"], w["w_gate"], w["conv_w"], w["conv_b"], w["w_down"], w["ln2_g"], w["ln2_b"])
    return y.reshape(b, s, d), conv


def kernel(x_prompt, x_sample, cache_a_k, cache_a_v, cache_b_k, cache_b_v, cache_conv, t5_table, w_in, lambda_q1, lambda_k1, lambda_q2, lambda_k2, subln_g, rel_table_b, w_pa, w_pb, w_out, ln1_g, ln1_b, w_up, w_gate, conv_w, conv_b, w_down, ln2_g, ln2_b):
    depth = w_in.shape[0]
    alpha = (2.0 * depth) ** 0.25
    bp, s, d = x_prompt.shape
    bs, t_new, _ = x_sample.shape
    past = cache_a_k.shape[2]
    pb = cache_b_k.shape[2]
    dff = w_up.shape[2]
    keep = min(LEFT_CHUNKS * CHUNK, s)
    assert pb == min(LEFT_CHUNKS * CHUNK, past)

    xp, xs = x_prompt, x_sample
    outs_p = [[] for _ in range(5)]
    outs_s = [[] for _ in range(5)]
    for l in range(depth):
        lam_init = 0.8 - 0.6 * math.exp(-0.3 * l)
        lams = [v[l].astype(F32).reshape(1, DA) for v in (lambda_q1, lambda_k1, lambda_q2, lambda_k2)]
        g_row = subln_g[l].astype(F32).reshape(1, 2 * DA)
        g_col = subln_g[l].astype(F32).reshape(2 * DA, 1)
        w = {
            "w_pa": w_pa[l].astype(BF16), "w_pb": w_pb[l].astype(BF16), "w_out": w_out[l].astype(BF16),
            "ln1_g": ln1_g[l].reshape(1, d), "ln1_b": ln1_b[l].reshape(1, d),
            "w_up": w_up[l].astype(BF16), "w_gate": w_gate[l].astype(BF16),
            "conv_w": conv_w[l], "conv_b": conv_b[l].reshape(1, dff),
            "w_down": w_down[l].astype(BF16),
            "ln2_g": ln2_g[l].reshape(1, d), "ln2_b": ln2_b[l].reshape(1, d),
        }
        w_in_b = w_in[l].astype(BF16)
        da, db = _bias_tiles(t5_table, rel_table_b[l])

        p, kat, va, kbt, vbt = _project_prompt(xp, w_in_b, keep, ATT_T)
        p3 = p.reshape(bp, s, 8 * d)
        oa = _attn_a_prompt(p3, da, lams, g_col, lam_init)
        ob = _attn_b_prompt(p3, db)
        conv0 = jnp.zeros((bp, CONV_W - 1, dff), F32)
        xp, conv_p = _tail(xp, oa, ob, p3, conv0, w, ATT_T, alpha)
        outs_p[0].append(kat.reshape(bp, HA, 2, DA, s).transpose(0, 4, 1, 2, 3))
        outs_p[1].append(va.reshape(bp, s, HA, 2 * DA))
        outs_p[2].append(kbt.reshape(bp, HB, DB, keep).transpose(0, 3, 1, 2))
        outs_p[3].append(vbt.reshape(bp, HB, DB, keep).transpose(0, 3, 1, 2))
        outs_p[4].append(conv_p)

        rows = bs * t_new
        p, ka, va, kb, vb = _project(xs.reshape(rows, d), w_in_b, t_new, t_new,
                                     ATT_T if rows % ATT_T == 0 else t_new)
        p3 = p.reshape(bs, t_new, 8 * d)
        cak_t = cache_a_k[l].transpose(0, 2, 3, 4, 1).reshape(bs, d, past)
        cbk_t = cache_b_k[l].transpose(0, 2, 3, 1).reshape(bs, d, pb)
        cbv_t = cache_b_v[l].transpose(0, 2, 3, 1).reshape(bs, d, pb)
        oa = _attn_a_sample(p3, cak_t, cache_a_v[l].reshape(bs, past * HA, 2 * DA),
                            da, lams, g_row, lam_init)
        ob = _attn_b_sample(p3, cbk_t, cbv_t, db, past)
        pack = ATT_T // t_new if (ATT_T % t_new == 0 and bs % (ATT_T // t_new) == 0) else 1
        xs, conv_s = _tail(xs, oa, ob, p3, cache_conv[l].astype(F32), w, pack * t_new, alpha)
        outs_s[0].append(ka.reshape(bs, t_new, HA, 2, DA))
        outs_s[1].append(va.reshape(bs, t_new, HA, 2 * DA))
        outs_s[2].append(kb.reshape(bs, t_new, HB, DB))
        outs_s[3].append(vb.reshape(bs, t_new, HB, DB))
        outs_s[4].append(conv_s)

    return (xp, xs, *[jnp.stack(o) for o in outs_p], *[jnp.stack(o) for o in outs_s])
```

```python
import functools
import math

import jax
import jax.numpy as jnp
from jax import lax
from jax.experimental import pallas as pl
from jax.experimental.pallas import tpu as pltpu

F32 = jnp.float32
BF16 = jnp.bfloat16

CHUNK = 64
HA = 8
DA = 64
HB = 16
DB = 64
LEFT_CHUNKS = 8
MAX_REL = 128
T5_BUCKETS = 32
T5_MAX_DIST = 128
CONV_W = 3
LN_EPS = 1e-5
RMS_EPS = 1e-5
NEG = -1e30
LOG2E = math.log2(math.e)

LANES = 128
ATT_T = 256
ATT_TA = 512
ATT_B_HEADS = 4
ONES_ROWS = 16
VMEM_LIMIT = 56 * 1024 * 1024


def _cparams(sem, vmem=VMEM_LIMIT):
    return pltpu.CompilerParams(dimension_semantics=sem, vmem_limit_bytes=vmem)


def _toeplitz(vec, t):
    x = jnp.broadcast_to(vec, (t, 2 * t))
    row = lax.broadcasted_iota(jnp.int32, (t, 2 * t), 0)
    shift = 1
    while shift < t:
        x = jnp.where((row & shift) != 0, pltpu.roll(x, shift, 1), x)
        shift *= 2
    return x[:, t:]


def _t5_bucket(rel):
    nb = T5_BUCKETS // 2
    max_exact = nb // 2
    ret = jnp.where(rel > 0, nb, 0)
    n = jnp.abs(rel)
    nf = jnp.maximum(n, 1).astype(F32)
    large = max_exact + (jnp.log(nf / max_exact) / math.log(T5_MAX_DIST / max_exact)
                         * (nb - max_exact)).astype(jnp.int32)
    large = jnp.minimum(large, nb - 1)
    return (ret + jnp.where(n < max_exact, n, large)) & (T5_BUCKETS - 1)


def _chunk_ids(t):
    kc = lax.broadcasted_iota(jnp.int32, (t, t), 0) // CHUNK
    qc = lax.broadcasted_iota(jnp.int32, (t, t), 1) // CHUNK
    return kc, qc


def _bias_a_kernel(tab_ref, bown_ref, bcorner_ref, own_ref, corner_ref, *, t):
    h = pl.program_id(0)
    c = T5_MAX_DIST
    far = tab_ref[h * T5_BUCKETS + T5_BUCKETS // 2 - 1]

    def band(bucket):
        acc = jnp.zeros(bucket.shape, F32)
        for j in range(T5_BUCKETS):
            acc = jnp.where(bucket == j, tab_ref[h * T5_BUCKETS + j], acc)
        return (acc - far) * LOG2E

    kc, qc = _chunk_ids(t)
    own_ref[0] = jnp.where(kc <= qc, _toeplitz(band(bown_ref[...]), t), NEG)
    corner_ref[0] = _toeplitz(band(bcorner_ref[...]), c)


def _bias_b_kernel(band_ref, out_ref, *, t):
    own = band_ref[0]
    far = own[:, 2 * t - 1:]
    r = lax.broadcasted_iota(jnp.int32, (1, 2 * t), 1)
    prev = jnp.where(r < t, pltpu.roll(own, t, 1), far)
    kc, qc = _chunk_ids(t)
    out_ref[0, 0:t, :] = jnp.where(kc - 2 * (t // CHUNK) >= qc - LEFT_CHUNKS, 0.0, NEG)
    out_ref[0, t:2 * t, :] = _toeplitz((prev - far) * LOG2E, t)
    out_ref[0, 2 * t:3 * t, :] = jnp.where(kc <= qc, _toeplitz((own - far) * LOG2E, t), NEG)
    out_ref[0, 3 * t:4 * t, :] = jnp.full((t, t), NEG, F32)


def _bias_tiles(t5_table, rel_table):
    ta, tb = ATT_TA, ATT_T
    assert ta >= T5_MAX_DIST and ta % CHUNK == 0
    assert tb >= MAX_REL and 2 * tb == LEFT_CHUNKS * CHUNK
    smem = pl.BlockSpec(memory_space=pltpu.SMEM)
    c = T5_MAX_DIST
    bucket_own = _t5_bucket(ta - jnp.arange(2 * ta, dtype=jnp.int32)).reshape(1, 2 * ta)
    bucket_corner = _t5_bucket(-jnp.arange(2 * c, dtype=jnp.int32)).reshape(1, 2 * c)
    da = pl.pallas_call(
        functools.partial(_bias_a_kernel, t=ta),
        grid=(HA,),
        in_specs=[smem, pl.BlockSpec((1, 2 * ta), lambda h: (0, 0)),
                  pl.BlockSpec((1, 2 * c), lambda h: (0, 0))],
        out_specs=[pl.BlockSpec((1, ta, ta), lambda h: (h, 0, 0)),
                   pl.BlockSpec((1, c, c), lambda h: (h, 0, 0))],
        out_shape=[jax.ShapeDtypeStruct((HA, ta, ta), F32),
                   jax.ShapeDtypeStruct((HA, c, c), F32)],
        compiler_params=_cparams(("parallel",)),
        name="bias_a",
    )(t5_table.astype(F32).T.reshape(-1), bucket_own, bucket_corner)
    band_b = jnp.pad(rel_table.astype(F32).T[:, ::-1],
                     ((0, 0), (tb - MAX_REL, tb - MAX_REL - 1)), mode="edge").reshape(HB, 1, 2 * tb)
    db = pl.pallas_call(
        functools.partial(_bias_b_kernel, t=tb),
        grid=(HB,),
        in_specs=[pl.BlockSpec((1, 1, 2 * tb), lambda h: (h, 0, 0))],
        out_specs=pl.BlockSpec((1, 4 * tb, tb), lambda h: (h, 0, 0)),
        out_shape=jax.ShapeDtypeStruct((HB, 4 * tb, tb), F32),
        compiler_params=_cparams(("parallel",)),
        name="bias_b",
    )(band_b)
    return da, db


def _proj_kernel(x_ref, w_ref, p_ref, ka_ref, va_ref, kb_ref, vb_ref, *, d, q_scale):
    xb = x_ref[...].astype(BF16)
    f32_outs = {1: ka_ref, 2: va_ref, 4: kb_ref, 5: vb_ref}
    for c in range(8):
        acc = jnp.dot(xb, w_ref[:, c * d:(c + 1) * d], preferred_element_type=F32)
        if c in f32_outs:
            f32_outs[c][...] = acc
        if c in (0, 3):
            acc = acc * q_scale
        p_ref[:, c * d:(c + 1) * d] = acc.astype(BF16)


def _proj_cache_kernel(x_ref, w_ref, p_ref, kat_ref, va_ref, kbt_ref, vbt_ref, *,
                       d, q_scale, tm, first_keep, tiles_per_seq):
    t = pl.program_id(0) % tiles_per_seq
    xb = x_ref[...].astype(BF16)
    kept = {}
    for c in range(8):
        acc = jnp.dot(xb, w_ref[:, c * d:(c + 1) * d], preferred_element_type=F32)
        if c == 1:
            kat_ref[0] = acc.T
        elif c == 2:
            for h in range(HA):
                va_ref[pl.ds(h, tm, stride=HA), :] = acc[:, h * LANES:(h + 1) * LANES]
        elif c in (4, 5):
            kept[c] = acc
        if c in (0, 3):
            acc = acc * q_scale
        p_ref[:, c * d:(c + 1) * d] = acc.astype(BF16)

    @pl.when(t >= first_keep)
    def _():
        kbt_ref[0] = kept[4].T
        vbt_ref[0] = kept[5].T


def _project_prompt(x3, w_bf16, keep, tm):
    b, s, d = x3.shape
    n = w_bf16.shape[1]
    assert n == 8 * d and d == HA * 2 * DA == HB * DB == HA * LANES
    assert s % tm == 0 and keep % tm == 0 and tm % LANES == 0
    tiles_per_seq = s // tm
    first_keep = tiles_per_seq - keep // tm
    m = b * s
    row = lambda i: (i, 0)
    col = lambda i: (i // tiles_per_seq, 0, i % tiles_per_seq)
    keep_col = lambda i: (i // tiles_per_seq, 0, jnp.maximum(i % tiles_per_seq - first_keep, 0))
    return pl.pallas_call(
        functools.partial(_proj_cache_kernel, d=d, q_scale=DA ** -0.5 * LOG2E, tm=tm,
                          first_keep=first_keep, tiles_per_seq=tiles_per_seq),
        grid=(m // tm,),
        in_specs=[pl.BlockSpec((tm, d), row),
                  pl.BlockSpec((d, n), lambda i: (0, 0), pipeline_mode=pl.Buffered(1))],
        out_specs=[pl.BlockSpec((tm, n), row),
                   pl.BlockSpec((1, d, tm), col), pl.BlockSpec((tm * HA, LANES), row),
                   pl.BlockSpec((1, d, tm), keep_col), pl.BlockSpec((1, d, tm), keep_col)],
        out_shape=[jax.ShapeDtypeStruct((m, n), BF16),
                   jax.ShapeDtypeStruct((b, d, s), F32), jax.ShapeDtypeStruct((m * HA, LANES), F32),
                   jax.ShapeDtypeStruct((b, d, keep), F32), jax.ShapeDtypeStruct((b, d, keep), F32)],
        compiler_params=_cparams(("arbitrary",)),
        name="in_proj_prompt",
    )(x3.reshape(m, d), w_bf16)


def _project(x2d, w_bf16, rows_per_seq, keep, tm):
    m, d = x2d.shape
    n = w_bf16.shape[1]
    assert n == 8 * d and d == HA * 2 * DA == HB * DB
    assert m % tm == 0
    row = lambda i: (i, 0)
    n_keep = (m // rows_per_seq) * keep
    if keep == rows_per_seq:
        keep_map = row
    else:
        assert rows_per_seq % tm == 0 and keep % tm == 0
        tiles_per_seq = rows_per_seq // tm
        keep_tiles = keep // tm
        first_keep = tiles_per_seq - keep_tiles

        def keep_map(i):
            return ((i // tiles_per_seq) * keep_tiles
                    + jnp.maximum(i % tiles_per_seq - first_keep, 0), 0)
    return pl.pallas_call(
        functools.partial(_proj_kernel, d=d, q_scale=DA ** -0.5 * LOG2E),
        grid=(m // tm,),
        in_specs=[pl.BlockSpec((tm, d), row),
                  pl.BlockSpec((d, n), lambda i: (0, 0), pipeline_mode=pl.Buffered(1))],
        out_specs=[pl.BlockSpec((tm, n), row),
                   pl.BlockSpec((tm, d), row), pl.BlockSpec((tm, d), row),
                   pl.BlockSpec((tm, d), keep_map), pl.BlockSpec((tm, d), keep_map)],
        out_shape=[jax.ShapeDtypeStruct((m, n), BF16),
                   jax.ShapeDtypeStruct((m, d), F32), jax.ShapeDtypeStruct((m, d), F32),
                   jax.ShapeDtypeStruct((n_keep, d), F32), jax.ShapeDtypeStruct((n_keep, d), F32)],
        compiler_params=_cparams(("arbitrary",)),
        name="in_proj",
    )(x2d, w_bf16)


def _half_masks(shape):
    lane = lax.broadcasted_iota(jnp.int32, shape, len(shape) - 1)
    return lane < (LANES // 2), lane >= (LANES // 2)


def _stack_halves(q):
    lo, hi = _half_masks(q.shape)
    zero = jnp.zeros_like(q)
    return jnp.concatenate([jnp.where(lo, q, zero), jnp.where(hi, q, zero)], axis=0)


def _qk(a, b):
    return lax.dot_general(a, b, (((1,), (1,)), ((), ())), preferred_element_type=F32)


def _transpose_blocks(v_ref, vt_ref, t):
    for j in range(vt_ref.shape[0]):
        vt_ref[j] = v_ref[0, j * t:(j + 1) * t, :].astype(F32).T.astype(vt_ref.dtype)


def _softmax_pv(scores, values, values_t=None):
    values_t = values_t or (False,) * len(values)
    m = functools.reduce(jnp.maximum, [jnp.max(s, axis=1, keepdims=True) for s in scores])
    l = None
    o = None
    for s, v, vt in zip(scores, values, values_t):
        p = jnp.exp2(s - m)
        ls = jnp.sum(p, axis=1, keepdims=True)
        pb = p.astype(BF16)
        ov = _qk(pb, v) if vt else jnp.dot(pb, v, preferred_element_type=F32)
        l = ls if l is None else l + ls
        o = ov if o is None else o + ov
    return o / l


def _lam(lq1, lk1, lq2, lk2, lam_init):
    e1 = jnp.exp(jnp.sum(lq1 * lk1, axis=1, keepdims=True))
    e2 = jnp.exp(jnp.sum(lq2 * lk2, axis=1, keepdims=True))
    return e1 - e2 + lam_init


def _attn_a_kernel(lq1_ref, lk1_ref, lq2_ref, lk2_ref, g_ref, q_ref, k_ref, v_ref, d_ref, dc_ref,
                   o_ref, vt_ref, s_ref, mc_ref, m_ref, acc_ref, done_ref, *, t, lam_init):
    n_tiles = q_ref.shape[1] // t
    c = dc_ref.shape[1]
    q_tile = lambda qi: _stack_halves(q_ref[0, pl.ds(pl.multiple_of(qi * t, t), t), :])
    lam = _lam(lq1_ref[...], lk1_ref[...], lq2_ref[...], lk2_ref[...], lam_init)

    def scores(q, kb, bias=None):
        start = pl.multiple_of(kb * t, t)
        s = _qk(k_ref[0, pl.ds(start, t), :], q)
        if bias is not None:
            s = s + jnp.concatenate([bias, bias], axis=1)
        s_ref[...] = s
        mc_ref[...] = jnp.max(s, axis=0, keepdims=True)

    def scores_prev(q, kb):
        start = pl.multiple_of(kb * t, t)
        top = _qk(k_ref[0, pl.ds(start, t - c), :], q)
        bot = _qk(k_ref[0, pl.ds(pl.multiple_of(start + (t - c), c), c), :], q)
        corner = dc_ref[0]
        bot = jnp.concatenate([bot[:, :c] + corner, bot[:, c:t],
                               bot[:, t:t + c] + corner, bot[:, t + c:]], axis=1)
        s_ref[:t - c, :] = top
        s_ref[t - c:, :] = bot
        mc_ref[...] = jnp.maximum(jnp.max(top, axis=0, keepdims=True),
                                  jnp.max(bot, axis=0, keepdims=True))

    def accumulate(kb, refill):
        s = s_ref[...]
        m_prev = m_ref[...]
        m_new = jnp.maximum(m_prev, mc_ref[...])
        alpha = jnp.exp2(m_prev - m_new)
        p = jnp.exp2(s - m_new)
        refill()
        pv = jnp.dot(vt_ref[kb], p.astype(BF16), preferred_element_type=F32)
        acc_ref[...] = alpha * acc_ref[...] + pv
        m_ref[...] = m_new

    for j in range(n_tiles):
        vt_ref[j, :LANES, :] = v_ref[0, j * t:(j + 1) * t, :].astype(F32).T.astype(BF16)
        vt_ref[j, LANES:, :] = jnp.ones((ONES_ROWS, t), BF16)
    scores(q_tile(0), 0, d_ref[0])

    def finish(qi):
        o = done_ref[:LANES, :] * (1.0 / done_ref[LANES:LANES + 1, :])
        o = o[:, :t] - lam * o[:, t:]
        o = o * lax.rsqrt(jnp.mean(o * o, axis=0, keepdims=True) + RMS_EPS)
        o = o * (g_ref[...] * (1.0 - lam_init))
        o_ref[0, pl.ds(pl.multiple_of(qi * t, t), t), :] = o.T.astype(o_ref.dtype)

    def tile(qi, carry):
        q2 = q_tile(qi)
        m_ref[...] = jnp.full(m_ref.shape, NEG, F32)
        acc_ref[...] = jnp.zeros(acc_ref.shape, F32)

        n_far = jnp.maximum(qi - 1, 0)

        @pl.when(qi >= 1)
        def _():
            finish(qi - 1)
            accumulate(qi, lambda: scores_prev(q2, qi - 1))

        def far(i):
            accumulate(jnp.where(i == 0, qi - 1, i - 1), lambda: scores(q2, i))

        def far_pair(j, c):
            far(2 * j)
            far(2 * j + 1)
            return c

        lax.fori_loop(0, n_far // 2, far_pair, 0)

        @pl.when(n_far % 2 == 1)
        def _():
            far(n_far - 1)
        nxt = jnp.minimum(qi + 1, n_tiles - 1)
        accumulate(jnp.where(qi == 0, 0, jnp.where(n_far == 0, qi - 1, n_far - 1)),
                   lambda: scores(q_tile(nxt), nxt, d_ref[0]))
        done_ref[...] = acc_ref[...]
        return carry

    lax.fori_loop(0, n_tiles, tile, 0)
    finish(n_tiles - 1)


def _attn_a_prompt(p3, da, lams, g_col, lam_init):
    b, s, n = p3.shape
    t = ATT_TA
    d_own, d_corner = da
    c = d_corner.shape[1]
    assert s % t == 0 and n == 8 * HA * LANES and t > c
    vec = pl.BlockSpec((1, DA), lambda bi, h: (0, 0))
    return pl.pallas_call(
        functools.partial(_attn_a_kernel, t=t, lam_init=lam_init),
        grid=(b, HA),
        in_specs=[vec, vec, vec, vec,
                  pl.BlockSpec((LANES, 1), lambda bi, h: (0, 0)),
                  pl.BlockSpec((1, s, LANES), lambda bi, h: (bi, 0, h)),
                  pl.BlockSpec((1, s, LANES), lambda bi, h: (bi, 0, HA + h)),
                  pl.BlockSpec((1, s, LANES), lambda bi, h: (bi, 0, 2 * HA + h)),
                  pl.BlockSpec((1, t, t), lambda bi, h: (h, 0, 0)),
                  pl.BlockSpec((1, c, c), lambda bi, h: (h, 0, 0))],
        out_specs=pl.BlockSpec((1, s, LANES), lambda bi, h: (bi, 0, h)),
        out_shape=jax.ShapeDtypeStruct((b, s, HA * LANES), BF16),
        scratch_shapes=[pltpu.VMEM((s // t, LANES + ONES_ROWS, t), BF16),
                        pltpu.VMEM((t, 2 * t), F32), pltpu.VMEM((1, 2 * t), F32),
                        pltpu.VMEM((1, 2 * t), F32),
                        pltpu.VMEM((LANES + ONES_ROWS, 2 * t), F32),
                        pltpu.VMEM((LANES + ONES_ROWS, 2 * t), F32)],
        compiler_params=_cparams(("parallel", "parallel")),
        name="attn_a_prompt",
    )(*lams, g_col, p3, p3, p3, d_own, d_corner)


def _attn_b_kernel(q_ref, k_ref, v_ref, d_ref, o_ref, vt_ref, s_ref, mc_ref, done_ref, *, t, heads):
    n_tiles = q_ref.shape[1] // t

    def window(q_tile):
        blocks = [jnp.maximum(q_tile - 2 + j, 0) for j in range(3)]
        bias_rows = [pl.multiple_of(jnp.where(q_tile - 2 + j >= 0, j, 3) * t, t) for j in range(3)]
        return blocks, bias_rows

    def score_part(g, q_tile):
        blocks, bias_rows = window(q_tile)
        pair = slice((g // 2) * LANES, (g // 2 + 1) * LANES)
        q = q_ref[0, pl.ds(pl.multiple_of(q_tile * t, t), t), pair]
        q = jnp.where(_half_masks(q.shape)[g % 2], q, jnp.zeros_like(q))

        def part(j, mc):
            k = k_ref[0, pl.ds(pl.multiple_of(blocks[j] * t, t), t), pair]
            s = _qk(k, q) + d_ref[g, pl.ds(bias_rows[j], t), :]
            s_ref[g, j * t:(j + 1) * t, :] = s
            cm = jnp.max(s, axis=0, keepdims=True)
            return cm if mc is None else jnp.maximum(mc, cm)

        return part

    _transpose_blocks(v_ref, vt_ref, t)
    part0 = score_part(0, 0)
    mc_ref[0] = functools.reduce(lambda mc, j: part0(j, mc), range(3), None)

    def emit(q_tile):
        o_ref[0, pl.ds(pl.multiple_of(q_tile * t, t), t), :] = done_ref[...].T.astype(o_ref.dtype)

    done_ref[...] = jnp.zeros(done_ref.shape, F32)

    def tile(qi, carry):
        blocks, _ = window(qi)
        emit(jnp.maximum(qi - 1, 0))

        def finish(g):
            g_next = (g + 1) % heads
            nxt = score_part(g_next, qi if g + 1 < heads else jnp.minimum(qi + 1, n_tiles - 1))
            m = mc_ref[g]
            mc = l = o = None
            for j, kb in enumerate(blocks):
                p = jnp.exp2(s_ref[g, j * t:(j + 1) * t, :] - m)
                mc = nxt(j, mc)
                ls = jnp.sum(p, axis=0, keepdims=True)
                vt = vt_ref[kb, g * DB:(g + 1) * DB, :]
                pv = jnp.dot(vt, p.astype(BF16), preferred_element_type=F32)
                l = ls if l is None else l + ls
                o = pv if o is None else o + pv
            mc_ref[g_next] = mc
            return o * (1.0 / l)

        outs = [finish(g) for g in range(heads)]
        done_ref[...] = jnp.concatenate(outs, axis=0)
        return carry

    lax.fori_loop(0, n_tiles, tile, 0)
    emit(n_tiles - 1)


def _attn_b_prompt(p3, db):
    b, s, n = p3.shape
    t = ATT_T
    g = ATT_B_HEADS
    w = g * DB
    assert s % t == 0 and HB % g == 0 and w % LANES == 0
    col0 = 3 * HA * LANES // w
    nblk = HB // g
    return pl.pallas_call(
        functools.partial(_attn_b_kernel, t=t, heads=g),
        grid=(b, nblk),
        in_specs=[pl.BlockSpec((1, s, w), lambda bi, h: (bi, 0, col0 + h)),
                  pl.BlockSpec((1, s, w), lambda bi, h: (bi, 0, col0 + nblk + h)),
                  pl.BlockSpec((1, s, w), lambda bi, h: (bi, 0, col0 + 2 * nblk + h)),
                  pl.BlockSpec((g, 4 * t, t), lambda bi, h: (h, 0, 0))],
        out_specs=pl.BlockSpec((1, s, w), lambda bi, h: (bi, 0, h)),
        out_shape=jax.ShapeDtypeStruct((b, s, HB * DB), BF16),
        scratch_shapes=[pltpu.VMEM((s // t, w, t), BF16),
                        pltpu.VMEM((g, 3 * t, t), F32), pltpu.VMEM((g, 1, t), F32),
                        pltpu.VMEM((w, t), F32)],
        compiler_params=_cparams(("parallel", "parallel")),
        name="attn_b_prompt",
    )(p3, p3, p3, db)


def _attn_a_sample_kernel(lq1_ref, lk1_ref, lq2_ref, lk2_ref, g_ref, p_ref, kc_ref, vc_ref, d_ref,
                          dc_ref, o_ref, *, lam_init):
    tq = p_ref.shape[1]
    past = kc_ref.shape[2]
    c = dc_ref.shape[1]
    lam = _lam(lq1_ref[...], lk1_ref[...], lq2_ref[...], lk2_ref[...], lam_init)
    for h in range(HA):
        head = lambda sec: p_ref[0, :, (sec * HA + h) * LANES:(sec * HA + h + 1) * LANES]
        q2 = _stack_halves(head(0))
        kn, vn = head(1), head(2)
        kct = kc_ref[0, h * LANES:(h + 1) * LANES, :].astype(BF16)
        vc = vc_ref[0, pl.ds(h, past, stride=HA), :].astype(BF16)
        near = dc_ref[h].T[:tq]
        new = d_ref[h].T[:tq, :tq]
        scores = [jnp.dot(q2, kct[:, :past - c], preferred_element_type=F32),
                  jnp.dot(q2, kct[:, past - c:], preferred_element_type=F32)
                  + jnp.concatenate([near, near], axis=0),
                  _qk(q2, kn) + jnp.concatenate([new, new], axis=0)]
        o = _softmax_pv(scores, [vc[:past - c], vc[past - c:], vn])
        o = o[:tq] - lam * o[tq:]
        o = o * lax.rsqrt(jnp.mean(o * o, axis=1, keepdims=True) + RMS_EPS)
        o_ref[0, :, h * LANES:(h + 1) * LANES] = (o * g_ref[...] * (1.0 - lam_init)).astype(o_ref.dtype)


def _attn_a_sample(p3, cache_kt, cache_v, da, lams, g_row, lam_init):
    b, tq, n = p3.shape
    d, past = cache_kt.shape[1:]
    d_own, d_corner = da
    c = d_corner.shape[1]
    assert tq <= CHUNK and past % CHUNK == 0 and past > c and tq % 8 == 0 and c == LANES
    vec = pl.BlockSpec((1, DA), lambda bi: (0, 0))
    whole = lambda bi: (0, 0, 0)
    return pl.pallas_call(
        functools.partial(_attn_a_sample_kernel, lam_init=lam_init),
        grid=(b,),
        in_specs=[vec, vec, vec, vec,
                  pl.BlockSpec((1, LANES), lambda bi: (0, 0)),
                  pl.BlockSpec((1, tq, n), lambda bi: (bi, 0, 0)),
                  pl.BlockSpec((1, d, past), lambda bi: (bi, 0, 0)),
                  pl.BlockSpec((1, past * HA, LANES), lambda bi: (bi, 0, 0)),
                  pl.BlockSpec((HA, LANES, LANES), whole, pipeline_mode=pl.Buffered(1)),
                  pl.BlockSpec((HA, c, c), whole, pipeline_mode=pl.Buffered(1))],
        out_specs=pl.BlockSpec((1, tq, d), lambda bi: (bi, 0, 0)),
        out_shape=jax.ShapeDtypeStruct((b, tq, d), BF16),
        compiler_params=_cparams(("parallel",)),
        name="attn_a_sample",
    )(*lams, g_row, p3, cache_kt, cache_v, d_own, d_corner)


def _attn_b_sample_kernel(p_ref, kc_ref, vc_ref, d_ref, o_ref, *, t):
    tq = p_ref.shape[1]
    past = kc_ref.shape[2]
    pairs = HB // 2
    olo, _ = _half_masks((tq, LANES))
    for hp in range(pairs):
        rows = slice(hp * LANES, (hp + 1) * LANES)
        head = lambda sec: p_ref[0, :, (3 * HA + sec * pairs + hp) * LANES:
                                 (3 * HA + sec * pairs + hp + 1) * LANES]
        q2 = _stack_halves(head(0))
        kct = kc_ref[0, rows, :].astype(BF16)
        vct = vc_ref[0, rows, :].astype(BF16)
        tiles = [d_ref[2 * hp + hh] for hh in range(2)]
        near = jnp.concatenate([d[t:2 * t, :].T[:tq] for d in tiles], axis=0)
        new = jnp.concatenate([d[2 * t:2 * t + LANES, :].T[:tq, :tq] for d in tiles], axis=0)
        scores = [jnp.dot(q2, kct[:, past - t:], preferred_element_type=F32) + near,
                  _qk(q2, head(1)) + new]
        values = [vct[:, past - t:], head(2)]
        values_t = [True, False]
        if past > t:
            scores.insert(0, jnp.dot(q2, kct[:, :past - t], preferred_element_type=F32))
            values.insert(0, vct[:, :past - t])
            values_t.insert(0, True)
        o = _softmax_pv(scores, values, values_t)
        o_ref[0, :, rows] = jnp.where(olo, o[:tq], o[tq:]).astype(o_ref.dtype)


def _attn_b_sample(p3, cache_kt, cache_vt, db, past_len):
    b, tq, n = p3.shape
    d, pb = cache_kt.shape[1:]
    t = ATT_T
    assert tq <= CHUNK and past_len % CHUNK == 0 and pb == LEFT_CHUNKS * CHUNK and pb >= t
    return pl.pallas_call(
        functools.partial(_attn_b_sample_kernel, t=t),
        grid=(b,),
        in_specs=[pl.BlockSpec((1, tq, n), lambda bi: (bi, 0, 0)),
                  pl.BlockSpec((1, d, pb), lambda bi: (bi, 0, 0)),
                  pl.BlockSpec((1, d, pb), lambda bi: (bi, 0, 0)),
                  pl.BlockSpec((HB, 4 * t, LANES), lambda bi: (0, 0, 0),
                               pipeline_mode=pl.Buffered(1))],
        out_specs=pl.BlockSpec((1, tq, d), lambda bi: (bi, 0, 0)),
        out_shape=jax.ShapeDtypeStruct((b, tq, d), BF16),
        compiler_params=_cparams(("parallel",)),
        name="attn_b_sample",
    )(p3, cache_kt, cache_vt, db)


def _layer_norm(x, g, b):
    mu = jnp.mean(x, axis=1, keepdims=True)
    xc = x - mu
    var = jnp.mean(xc * xc, axis=1, keepdims=True)
    return xc * lax.rsqrt(var + LN_EPS) * g + b


def _tail_kernel(x_ref, oa_ref, ob_ref, ga_ref, gb_ref, prev_ref,
                 wpa_ref, wpb_ref, wout_ref, ln1g_ref, ln1b_ref,
                 wup_ref, wgate_ref, cw_ref, cb_ref, wdown_ref, ln2g_ref, ln2b_ref,
                 y_ref, conv_ref, u_ref, *, tm, alpha, seqs):
    ti = pl.program_id(1)
    pad = 8
    hist = CONV_W - 1
    dff = u_ref.shape[1]

    if seqs == 1:
        @pl.when(ti == 0)
        def _():
            u_ref[pad - hist:pad, :] = prev_ref[0]
    else:
        u_ref[pad - hist:pad, :] = jnp.zeros((hist, dff), F32)

    ya = jnp.dot(oa_ref[0], wpa_ref[...], preferred_element_type=F32)
    yb = jnp.dot(ob_ref[0], wpb_ref[...], preferred_element_type=F32)
    merged = (jax.nn.sigmoid(ga_ref[0].astype(F32)) * ya
              + jax.nn.sigmoid(gb_ref[0].astype(F32)) * yb)
    mixed = jnp.dot(merged.astype(BF16), wout_ref[...], preferred_element_type=F32)
    h = _layer_norm(alpha * x_ref[0] + mixed, ln1g_ref[...], ln1b_ref[...])
    hb = h.astype(BF16)
    u = jnp.dot(hb, wup_ref[...], preferred_element_type=F32)
    g = jnp.dot(hb, wgate_ref[...], preferred_element_type=F32)
    u_ref[pad:pad + tm, :] = u
    u2 = u_ref[pad - 2:pad - 2 + tm, :]
    u1 = u_ref[pad - 1:pad - 1 + tm, :]
    if seqs > 1:
        sl = tm // seqs
        pos = lax.broadcasted_iota(jnp.int32, (tm, 1), 0) % sl
        hist_rows = lambda j: jnp.concatenate(
            [jnp.broadcast_to(prev_ref[q, j:j + 1, :], (sl, dff)) for q in range(seqs)], axis=0)
        u1 = jnp.where(pos == 0, hist_rows(1), u1)
        u2 = jnp.where(pos == 0, hist_rows(0), jnp.where(pos == 1, hist_rows(1), u2))
    uc = cb_ref[...] + u2 * cw_ref[0:1, :]
    uc = uc + u1 * cw_ref[1:2, :]
    uc = uc + u * cw_ref[2:3, :]
    if seqs == 1:
        last = u_ref[pad + tm - hist:pad + tm, :]
        conv_ref[0] = last
        u_ref[pad - hist:pad, :] = last
    else:
        for q in range(seqs):
            conv_ref[q] = u_ref[pad + (q + 1) * sl - hist:pad + (q + 1) * sl, :]
    f = jnp.dot((jax.nn.gelu(uc) * g).astype(BF16), wdown_ref[...], preferred_element_type=F32)
    y_ref[0] = _layer_norm(alpha * h + f, ln2g_ref[...], ln2b_ref[...])


def _tail(x3, oa, ob, p3, conv_prev, w, tm, alpha):
    b, s, d = x3.shape
    dff = w["w_up"].shape[1]
    hist = CONV_W - 1
    seqs = max(tm // s, 1)
    assert (s % tm == 0 or tm % s == 0) and b % seqs == 0 and s >= hist
    if seqs > 1:
        fold = lambda a: a.reshape(b // seqs, seqs * s, a.shape[-1])
        x3, oa, ob, p3 = fold(x3), fold(oa), fold(ob), fold(p3)
    nb, rows = x3.shape[:2]
    const = lambda shape: pl.BlockSpec(shape, lambda bi, ti: (0,) * len(shape),
                                       pipeline_mode=pl.Buffered(1))
    act = lambda col: pl.BlockSpec((1, tm, d), lambda bi, ti: (bi, ti, col))
    y, conv = pl.pallas_call(
        functools.partial(_tail_kernel, tm=tm, alpha=alpha, seqs=seqs),
        grid=(nb, rows // tm),
        in_specs=[act(0), act(0), act(0), act(6), act(7),
                  pl.BlockSpec((seqs, hist, dff), lambda bi, ti: (bi, 0, 0)),
                  const((d, d)), const((d, d)), const((d, d)), const((1, d)), const((1, d)),
                  const((d, dff)), const((d, dff)), const((CONV_W, dff)), const((1, dff)),
                  const((dff, d)), const((1, d)), const((1, d))],
        out_specs=[pl.BlockSpec((1, tm, d), lambda bi, ti: (bi, ti, 0)),
                   pl.BlockSpec((seqs, hist, dff), lambda bi, ti: (bi, 0, 0))],
        out_shape=[jax.ShapeDtypeStruct((nb, rows, d), F32),
                   jax.ShapeDtypeStruct((b, hist, dff), F32)],
        scratch_shapes=[pltpu.VMEM((tm + 8, dff), F32)],
        compiler_params=_cparams(("parallel", "arbitrary")),
        name="tail",
    )(x3, oa, ob, p3, p3, conv_prev,
      w["w_pa"], w["w_pb"], w["w_out"], w["ln1_g"], w["ln1_b"],
      w["w_up"], w["w_gate"], w["conv_w"], w["conv_b"], w["w_down"], w["ln2_g"], w["ln2_b"])
    return y.reshape(b, s, d), conv


def kernel(x_prompt, x_sample, cache_a_k, cache_a_v, cache_b_k, cache_b_v, cache_conv, t5_table, w_in, lambda_q1, lambda_k1, lambda_q2, lambda_k2, subln_g, rel_table_b, w_pa, w_pb, w_out, ln1_g, ln1_b, w_up, w_gate, conv_w, conv_b, w_down, ln2_g, ln2_b):
    depth = w_in.shape[0]
    alpha = (2.0 * depth) ** 0.25
    bp, s, d = x_prompt.shape
    bs, t_new, _ = x_sample.shape
    past = cache_a_k.shape[2]
    pb = cache_b_k.shape[2]
    dff = w_up.shape[2]
    keep = min(LEFT_CHUNKS * CHUNK, s)
    assert pb == min(LEFT_CHUNKS * CHUNK, past)

    xp, xs = x_prompt, x_sample
    outs_p = [[] for _ in range(5)]
    outs_s = [[] for _ in range(5)]
    for l in range(depth):
        lam_init = 0.8 - 0.6 * math.exp(-0.3 * l)
        lams = [v[l].astype(F32).reshape(1, DA) for v in (lambda_q1, lambda_k1, lambda_q2, lambda_k2)]
        g_row = subln_g[l].astype(F32).reshape(1, 2 * DA)
        g_col = subln_g[l].astype(F32).reshape(2 * DA, 1)
        w = {
            "w_pa": w_pa[l].astype(BF16), "w_pb": w_pb[l].astype(BF16), "w_out": w_out[l].astype(BF16),
            "ln1_g": ln1_g[l].reshape(1, d), "ln1_b": ln1_b[l].reshape(1, d),
            "w_up": w_up[l].astype(BF16), "w_gate": w_gate[l].astype(BF16),
            "conv_w": conv_w[l], "conv_b": conv_b[l].reshape(1, dff),
            "w_down": w_down[l].astype(BF16),
            "ln2_g": ln2_g[l].reshape(1, d), "ln2_b": ln2_b[l].reshape(1, d),
        }
        w_in_b = w_in[l].astype(BF16)
        da, db = _bias_tiles(t5_table, rel_table_b[l])

        p, kat, va, kbt, vbt = _project_prompt(xp, w_in_b, keep, ATT_T)
        p3 = p.reshape(bp, s, 8 * d)
        oa = _attn_a_prompt(p3, da, lams, g_col, lam_init)
        ob = _attn_b_prompt(p3, db)
        conv0 = jnp.zeros((bp, CONV_W - 1, dff), F32)
        xp, conv_p = _tail(xp, oa, ob, p3, conv0, w, ATT_T, alpha)
        outs_p[0].append(kat.reshape(bp, HA, 2, DA, s).transpose(0, 4, 1, 2, 3))
        outs_p[1].append(va.reshape(bp, s, HA, 2 * DA))
        outs_p[2].append(kbt.reshape(bp, HB, DB, keep).transpose(0, 3, 1, 2))
        outs_p[3].append(vbt.reshape(bp, HB, DB, keep).transpose(0, 3, 1, 2))
        outs_p[4].append(conv_p)

        rows = bs * t_new
        p, ka, va, kb, vb = _project(xs.reshape(rows, d), w_in_b, t_new, t_new,
                                     ATT_T if rows % ATT_T == 0 else t_new)
        p3 = p.reshape(bs, t_new, 8 * d)
        cak_t = cache_a_k[l].transpose(0, 2, 3, 4, 1).reshape(bs, d, past)
        cbk_t = cache_b_k[l].transpose(0, 2, 3, 1).reshape(bs, d, pb)
        cbv_t = cache_b_v[l].transpose(0, 2, 3, 1).reshape(bs, d, pb)
        oa = _attn_a_sample(p3, cak_t, cache_a_v[l].reshape(bs, past * HA, 2 * DA),
                            da, lams, g_row, lam_init)
        ob = _attn_b_sample(p3, cbk_t, cbv_t, db, past)
        pack = ATT_T // t_new if (ATT_T % t_new == 0 and bs % (ATT_T // t_new) == 0) else 1
        xs, conv_s = _tail(xs, oa, ob, p3, cache_conv[l].astype(F32), w, pack * t_new, alpha)
        outs_s[0].append(ka.reshape(bs, t_new, HA, 2, DA))
        outs_s[1].append(va.reshape(bs, t_new, HA, 2 * DA))
        outs_s[2].append(kb.reshape(bs, t_new, HB, DB))
        outs_s[3].append(vb.reshape(bs, t_new, HB, DB))
        outs_s[4].append(conv_s)

    return (xp, xs, *[jnp.stack(o) for o in outs_p], *[jnp.stack(o) for o in outs_s])
```
